```python
import jax, jax.numpy as jnp
from jax import lax
import numpy as np

D_MODEL = 2048
BATCH = 8
SEQ = 2048
DEPTH = 2

HEAD_DIM = 128
MIX_WIDTH = D_MODEL
N_HEADS = MIX_WIDTH // HEAD_DIM
N_HEADS_MOBA = N_HEADS // 4
N_HEADS_DIL = (N_HEADS - N_HEADS_MOBA) // 2
N_HEADS_FOX = N_HEADS - N_HEADS_MOBA - N_HEADS_DIL
DILATED_BRANCHES = ((128, 1), (512, 4), (2048, 16))
BAND_BLOCK = 128
MOBA_BLOCK = 256
MOBA_TOPK = 3
Q_BLOCK = 128
ROPE_THETA = 500000.0
ROT_DIM = HEAD_DIM // 4
D_FF = (7 * D_MODEL) // 2
N_EXPERTS = 8
TOP_K = 2
N_DENSE = (DEPTH + 1) // 2
N_MOE = DEPTH // 2
N_MOD = 6
RMS_EPS = 1e-6
NEG_INF = -1e30
FORGET_BIAS = 2.0
ATTN_SCALE = HEAD_DIM ** -0.5

kernel_name = 'hybrid_dilated_moba_fox_moe_block'


def rms_norm(x, g):
    xf = x.astype(jnp.float32)
    y = xf * lax.rsqrt(jnp.mean(xf * xf, axis=-1, keepdims=True) + RMS_EPS)
    return (y * g.astype(jnp.float32)).astype(x.dtype)


def modulate(h, shift, scale):
    return h * (1 + scale[:, None, :]) + shift[:, None, :]


def partial_rotary(t, positions):
    half = ROT_DIM // 2
    inv_freq = ROPE_THETA ** (-jnp.arange(half, dtype=jnp.float32) / half)
    ang = positions.astype(jnp.float32)[:, None, :, None] * inv_freq
    cos, sin = jnp.cos(ang), jnp.sin(ang)
    tr = t[..., :ROT_DIM].astype(jnp.float32)
    t1, t2 = tr[..., :half], tr[..., half:]
    rot = jnp.concatenate([t1 * cos - t2 * sin, t2 * cos + t1 * sin], axis=-1)
    return jnp.concatenate([rot.astype(t.dtype), t[..., ROT_DIM:]], axis=-1)


def dilated_branch(q, k, v, window, dilation):
    bsz, nh, seq, hd = q.shape
    sub_len = seq // dilation
    band = window // dilation
    n_b = -(-sub_len // BAND_BLOCK)
    pad_len = n_b * BAND_BLOCK - sub_len

    def to_sub(t):
        t = t.astype(jnp.float32).reshape(bsz, nh, sub_len, dilation, hd).transpose(0, 1, 3, 2, 4)
        t = jnp.pad(t, ((0, 0), (0, 0), (0, 0), (0, pad_len), (0, 0)))
        return t.reshape(bsz, nh, dilation, n_b, BAND_BLOCK, hd)

    def with_prev(t):
        prev = jnp.pad(t, ((0, 0), (0, 0), (0, 0), (1, 0), (0, 0), (0, 0)))[:, :, :, :-1]
        return jnp.concatenate([prev, t], axis=4)

    qs = to_sub(q) * ATTN_SCALE
    kb = with_prev(to_sub(k))
    vb = with_prev(to_sub(v))
    blk = jnp.arange(n_b)[:, None, None] * BAND_BLOCK
    q_idx = blk + jnp.arange(BAND_BLOCK)[None, :, None]
    k_idx = blk - BAND_BLOCK + jnp.arange(2 * BAND_BLOCK)[None, None, :]
    dist = q_idx - k_idx
    valid = (dist >= 0) & (dist <= band) & (k_idx >= 0)
    logits = jnp.where(valid, jnp.einsum('bhrnqd,bhrnkd->bhrnqk', qs, kb), NEG_INF)
    m = jnp.max(logits, axis=-1, keepdims=True)
    p = jnp.exp(logits - m)
    den = jnp.sum(p, axis=-1)
    o = jnp.einsum('bhrnqk,bhrnkd->bhrnqd', p, vb) / den[..., None]
    lse = m[..., 0] + jnp.log(den)
    o = o.reshape(bsz, nh, dilation, n_b * BAND_BLOCK, hd)[:, :, :, :sub_len]
    o = o.transpose(0, 1, 3, 2, 4).reshape(bsz, nh, seq, hd)
    lse = lse.reshape(bsz, nh, dilation, n_b * BAND_BLOCK)[..., :sub_len]
    lse = lse.transpose(0, 1, 3, 2).reshape(bsz, nh, seq)
    return o, lse


def dilated_attention(q, k, v):
    outs, lses = [], []
    for window, dilation in DILATED_BRANCHES:
        o, lse = dilated_branch(q, k, v, window, dilation)
        outs.append(o)
        lses.append(lse)
    wts = jax.nn.softmax(jnp.stack(lses, axis=0), axis=0)
    return jnp.einsum('nbhs,nbhsd->bhsd', wts, jnp.stack(outs, axis=0))


def moba_attention(q, k, v):
    bsz, nh, seq, hd = q.shape
    n_kb = -(-seq // MOBA_BLOCK)
    pad = ((0, 0), (0, 0), (0, n_kb * MOBA_BLOCK - seq), (0, 0))
    kb = jnp.pad(k, pad).reshape(bsz, nh, n_kb, MOBA_BLOCK, hd).astype(jnp.float32)
    vb = jnp.pad(v, pad).reshape(bsz, nh, n_kb, MOBA_BLOCK, hd).astype(jnp.float32)
    cur = jnp.arange(seq) // MOBA_BLOCK
    n_sel = min(MOBA_TOPK, n_kb - 1)
    if n_sel > 0:
        k_mean = kb.mean(axis=3)
        gate = jnp.einsum('bhsd,bhnd->bhsn', q.astype(jnp.float32), k_mean)
        fully_past = jnp.arange(n_kb)[None, :] < cur[:, None]
        gate = jnp.where(fully_past, gate, NEG_INF)
        _, sel = lax.top_k(gate, n_sel)
    else:
        sel = jnp.zeros((bsz, nh, seq, 0), jnp.int32)
    sel_valid = jnp.arange(n_sel)[None, :] < cur[:, None]
    n_qb = seq // Q_BLOCK
    qc = (q.astype(jnp.float32) * ATTN_SCALE).reshape(bsz, nh, n_qb, Q_BLOCK, hd).transpose(2, 0, 1, 3, 4)
    selc = sel.reshape(bsz, nh, n_qb, Q_BLOCK, n_sel).transpose(2, 0, 1, 3, 4)
    validc = sel_valid.reshape(n_qb, Q_BLOCK, n_sel)
    b_idx = jnp.arange(bsz)[:, None, None]
    h_idx = jnp.arange(nh)[None, :, None]

    def block_attend(args):
        i, q_blk, sel_blk, valid_blk = args
        start = i * Q_BLOCK
        own = start // MOBA_BLOCK
        k_own = lax.dynamic_index_in_dim(kb, own, axis=2, keepdims=False)
        v_own = lax.dynamic_index_in_dim(vb, own, axis=2, keepdims=False)
        q_pos = start + jnp.arange(Q_BLOCK)
        k_pos = own * MOBA_BLOCK + jnp.arange(MOBA_BLOCK)
        logits = [jnp.where(k_pos[None, :] <= q_pos[:, None],
                            jnp.einsum('bhqd,bhkd->bhqk', q_blk, k_own), NEG_INF)]
        for r in range(n_sel):
            k_sel = kb[b_idx, h_idx, sel_blk[..., r]]
            l_sel = jnp.einsum('bhqd,bhqkd->bhqk', q_blk, k_sel)
            logits.append(jnp.where(valid_blk[None, None, :, r, None], l_sel, NEG_INF))
        p = jax.nn.softmax(jnp.concatenate(logits, axis=-1), axis=-1)
        out = jnp.einsum('bhqk,bhkd->bhqd', p[..., :MOBA_BLOCK], v_own)
        for r in range(n_sel):
            v_sel = vb[b_idx, h_idx, sel_blk[..., r]]
            out = out + jnp.einsum('bhqk,bhqkd->bhqd',
                                   p[..., (r + 1) * MOBA_BLOCK:(r + 2) * MOBA_BLOCK], v_sel)
        return out

    out = lax.map(block_attend, (jnp.arange(n_qb), qc, selc, validc))
    return out.transpose(1, 2, 0, 3, 4).reshape(bsz, nh, seq, hd)


def forgetting_attention(q, k, v, log_f):
    bsz, nh, seq, hd = q.shape
    cum_f = lax.cumsum(log_f, axis=2)
    kf = k.astype(jnp.float32)
    vf = v.astype(jnp.float32)
    n_qb = seq // Q_BLOCK
    qc = (q.astype(jnp.float32) * ATTN_SCALE).reshape(bsz, nh, n_qb, Q_BLOCK, hd).transpose(2, 0, 1, 3, 4)
    fc = cum_f.reshape(bsz, nh, n_qb, Q_BLOCK).transpose(2, 0, 1, 3)
    k_pos = jnp.arange(seq)

    def block_attend(args):
        i, q_blk, f_blk = args
        q_pos = i * Q_BLOCK + jnp.arange(Q_BLOCK)
        logits = jnp.einsum('bhqd,bhkd->bhqk', q_blk, kf) + f_blk[..., None] - cum_f[:, :, None, :]
        logits = jnp.where(k_pos[None, :] <= q_pos[:, None], logits, NEG_INF)
        p = jax.nn.softmax(logits, axis=-1)
        return jnp.einsum('bhqk,bhkd->bhqd', p, vf)

    out = lax.map(block_attend, (jnp.arange(n_qb), qc, fc))
    return out.transpose(1, 2, 0, 3, 4).reshape(bsz, nh, seq, hd)


def token_mixer(h, positions, w_in, b_fgate, w_out):
    bsz, seq, _ = h.shape
    proj = h @ w_in
    qkv = proj[..., :3 * MIX_WIDTH].reshape(bsz, seq, 3, N_HEADS, HEAD_DIM).transpose(2, 0, 3, 1, 4)
    q, k, v = qkv[0], qkv[1], qkv[2]
    log_f = jax.nn.log_sigmoid((proj[..., 3 * MIX_WIDTH:] + b_fgate).astype(jnp.float32)).transpose(0, 2, 1)
    n_rot = N_HEADS_DIL + N_HEADS_MOBA
    q_rot = partial_rotary(q[:, :n_rot], positions)
    k_rot = partial_rotary(k[:, :n_rot], positions)
    o_dil = dilated_attention(q_rot[:, :N_HEADS_DIL], k_rot[:, :N_HEADS_DIL], v[:, :N_HEADS_DIL])
    o_moba = moba_attention(q_rot[:, N_HEADS_DIL:], k_rot[:, N_HEADS_DIL:], v[:, N_HEADS_DIL:n_rot])
    o_fox = forgetting_attention(q[:, n_rot:], k[:, n_rot:], v[:, n_rot:], log_f)
    o = jnp.concatenate([o_dil, o_moba, o_fox], axis=1).astype(h.dtype)
    o = o.transpose(0, 2, 1, 3).reshape(bsz, seq, MIX_WIDTH)
    return o @ w_out


def swiglu(h, w_gate, w_up, w_down):
    return (jax.nn.silu(h @ w_gate) * (h @ w_up)) @ w_down


def moe_swiglu(h, w_router, b_router, w_gate, w_up, w_down):
    logits = (h @ w_router).astype(jnp.float32) + b_router.astype(jnp.float32)
    top_val, top_idx = lax.top_k(logits, TOP_K)
    top_p = jax.nn.softmax(top_val, axis=-1)
    gates = jnp.einsum('bske,bsk->bse', jax.nn.one_hot(top_idx, N_EXPERTS, dtype=jnp.float32), top_p)
    out = jnp.zeros(h.shape, jnp.float32)
    for e in range(N_EXPERTS):
        out = out + gates[..., e:e + 1] * swiglu(h, w_gate[e], w_up[e], w_down[e])
    return out.astype(h.dtype)


def setup_inputs(seed: int = 0) -> dict:
    key = jax.random.key(seed)
    ks = jax.random.split(key, 20)
    f32 = jnp.float32

    def nrm(k, shape, fan_in, mult=1.0):
        return jax.random.normal(k, shape, f32) * (mult * fan_in ** -0.5)

    def gain(k, shape):
        return 1.0 + 0.05 * jax.random.normal(k, shape, f32)

    n_in = 3 * MIX_WIDTH + N_HEADS_FOX
    positions = (jnp.arange(SEQ, dtype=jnp.int32)[None, :]
                 + jax.random.randint(ks[2], (BATCH, 1), 0, 4096, dtype=jnp.int32))
    return {
        'x': jax.random.normal(ks[0], (BATCH, SEQ, D_MODEL), f32),
        'c': jax.random.normal(ks[1], (BATCH, D_MODEL), f32),
        'positions': positions,
        'ln_mix_pre': gain(ks[3], (DEPTH, D_MODEL)),
        'ln_mix_post': gain(ks[4], (DEPTH, D_MODEL)),
        'ln_ffn_pre': gain(ks[5], (DEPTH, D_MODEL)),
        'ln_ffn_post': gain(ks[6], (DEPTH, D_MODEL)),
        'w_mod': nrm(ks[7], (DEPTH, D_MODEL, N_MOD * D_MODEL), D_MODEL, 0.5),
        'b_mod': 0.02 * jax.random.normal(ks[8], (DEPTH, N_MOD * D_MODEL), f32),
        'w_in': nrm(ks[9], (DEPTH, D_MODEL, n_in), D_MODEL),
        'b_fgate': FORGET_BIAS + 0.5 * jax.random.normal(ks[10], (DEPTH, N_HEADS_FOX), f32),
        'w_out': nrm(ks[11], (DEPTH, MIX_WIDTH, D_MODEL), MIX_WIDTH),
        'w_ffn_gate': nrm(ks[12], (N_DENSE, D_MODEL, D_FF), D_MODEL),
        'w_ffn_up': nrm(ks[13], (N_DENSE, D_MODEL, D_FF), D_MODEL),
        'w_ffn_down': nrm(ks[14], (N_DENSE, D_FF, D_MODEL), D_FF),
        'w_router': nrm(ks[15], (N_MOE, D_MODEL, N_EXPERTS), D_MODEL),
        'b_router': 0.01 * jax.random.normal(ks[16], (N_MOE, N_EXPERTS), f32),
        'w_exp_gate': nrm(ks[17], (N_MOE, N_EXPERTS, D_MODEL, D_FF), D_MODEL),
        'w_exp_up': nrm(ks[18], (N_MOE, N_EXPERTS, D_MODEL, D_FF), D_MODEL),
        'w_exp_down': nrm(ks[19], (N_MOE, N_EXPERTS, D_FF, D_MODEL), D_FF),
    }


def reference(x, c, positions, ln_mix_pre, ln_mix_post, ln_ffn_pre, ln_ffn_post, w_mod, b_mod,
              w_in, b_fgate, w_out, w_ffn_gate, w_ffn_up, w_ffn_down, w_router, b_router,
              w_exp_gate, w_exp_up, w_exp_down):
    cond = jax.nn.silu(c)
    for layer in range(DEPTH):
        mod = cond @ w_mod[layer] + b_mod[layer]
        sh_m, sc_m, g_m, sh_f, sc_f, g_f = jnp.split(mod, N_MOD, axis=-1)
        h = modulate(rms_norm(x, ln_mix_pre[layer]), sh_m, sc_m)
        y = token_mixer(h, positions, w_in[layer], b_fgate[layer], w_out[layer])
        x = x + g_m[:, None, :] * rms_norm(y, ln_mix_post[layer])
        h = modulate(rms_norm(x, ln_ffn_pre[layer]), sh_f, sc_f)
        j = layer // 2
        if layer % 2 == 0:
            y = swiglu(h, w_ffn_gate[j], w_ffn_up[j], w_ffn_down[j])
        else:
            y = moe_swiglu(h, w_router[j], b_router[j], w_exp_gate[j], w_exp_up[j], w_exp_down[j])
        x = x + g_f[:, None, :] * rms_norm(y, ln_ffn_post[layer])
    return x
```

```python
import functools

import numpy as np
import jax
import jax.numpy as jnp
from jax import lax
from jax.experimental import pallas as pl
from jax.experimental.pallas import tpu as pltpu

F32 = jnp.float32
MXU_DTYPE = jnp.bfloat16

HEAD_DIM = 128
N_HEADS = 16
N_HEADS_DIL = 6
N_HEADS_MOBA = 4
N_HEADS_FOX = 6
DILATED_BRANCHES = ((128, 1), (512, 4), (2048, 16))
MOBA_BLOCK = 256
MOBA_TOPK = 3
ROPE_THETA = 500000.0
ROT_DIM = HEAD_DIM // 4
ROT_HALF = ROT_DIM // 2
N_EXPERTS = 8
TOP_K = 2
N_MOD = 6
RMS_EPS = 1e-6
NEG_INF = -1e30
ATTN_SCALE = HEAD_DIM ** -0.5
LANES = 128

VMEM_LIMIT_BYTES = 56 * 1024 * 1024

TM_IN = 1024
TN_IN = 1536
TM_OUT = 512
TM_FFN = 512
TF_FFN = 512
TM_MOE = 512
TM_CMB = 256
TQ_ATT = 512
TN_MOD = 1024


def _cparams(*sem):
    return pltpu.CompilerParams(dimension_semantics=sem, vmem_limit_bytes=VMEM_LIMIT_BYTES)


def _mxu(a):
    return a.astype(MXU_DTYPE)


def _dot(a, b):
    return jnp.dot(a, b, preferred_element_type=F32)


def _dot_nt(a, b):
    return lax.dot_general(a, b, (((1,), (1,)), ((), ())), preferred_element_type=F32)


def _split_hi_lo(a):
    hi = a.astype(MXU_DTYPE)
    lo = (a - hi.astype(F32)).astype(MXU_DTYPE)
    return hi, lo


def _rms_norm(x, g):
    ms = jnp.mean(x * x, axis=-1, keepdims=True)
    return x * lax.rsqrt(ms + RMS_EPS) * g


def _sigmoid(x):
    return 1.0 / (1.0 + jnp.exp(-x))


def _mod_kernel(c_ref, w_ref, b_ref, o_ref):
    c = c_ref[...]
    cond = c * _sigmoid(c)
    o_ref[0] = _dot(_mxu(cond), _mxu(w_ref[0])) + b_ref[0]


def _modulation(c, w_mod, b_mod):
    depth, d, n = w_mod.shape
    bsz = c.shape[0]
    rows = max(16, bsz)
    c_pad = jnp.pad(c, ((0, rows - bsz), (0, 0)))
    out = pl.pallas_call(
        _mod_kernel,
        grid=(depth, n // TN_MOD),
        in_specs=[
            pl.BlockSpec((rows, d), lambda l, j: (0, 0)),
            pl.BlockSpec((1, d, TN_MOD), lambda l, j: (l, 0, j)),
            pl.BlockSpec((1, 1, TN_MOD), lambda l, j: (l, 0, j)),
        ],
        out_specs=pl.BlockSpec((1, rows, TN_MOD), lambda l, j: (l, 0, j)),
        out_shape=jax.ShapeDtypeStruct((depth, rows, n), F32),
        compiler_params=_cparams("parallel", "parallel"),
        name="modulation",
    )(c_pad, w_mod, b_mod.reshape(depth, 1, n))
    return out[:, :bsz]


def _rope_table_kernel(pos_ref, freq_ref, cos_ref, sin_ref):
    ang = pos_ref[0].astype(F32) * freq_ref[...]
    lane = lax.broadcasted_iota(jnp.int32, ang.shape, 1)
    cos_ref[0] = jnp.where(lane < ROT_DIM, jnp.cos(ang), 1.0)
    sn = jnp.sin(ang)
    sin_ref[0] = jnp.where(lane < ROT_HALF, -sn, jnp.where(lane < ROT_DIM, sn, 0.0))


def _rope_tables(positions):
    bsz, seq = positions.shape
    inv_freq = ROPE_THETA ** (-np.arange(ROT_HALF, dtype=np.float32) / ROT_HALF)
    freq = np.zeros((1, LANES), np.float32)
    freq[0, :ROT_HALF] = inv_freq
    freq[0, ROT_HALF:ROT_DIM] = inv_freq
    tab = jax.ShapeDtypeStruct((bsz, seq, LANES), F32)
    return pl.pallas_call(
        _rope_table_kernel,
        grid=(bsz,),
        in_specs=[
            pl.BlockSpec((1, seq, 1), lambda b: (b, 0, 0)),
            pl.BlockSpec((1, LANES), lambda b: (0, 0)),
        ],
        out_specs=[pl.BlockSpec((1, seq, LANES), lambda b: (b, 0, 0))] * 2,
        out_shape=[tab, tab],
        compiler_params=_cparams("parallel"),
        name="rope_tables",
    )(positions.reshape(bsz, seq, 1), jnp.asarray(freq))


def _rope(t, cos, sin_signed):
    lane = lax.broadcasted_iota(jnp.int32, t.shape, 1)
    partner = jnp.where(lane < ROT_HALF,
                        pltpu.roll(t, LANES - ROT_HALF, 1),
                        pltpu.roll(t, ROT_HALF, 1))
    return t * cos + partner * sin_signed


def _in_proj_kernel(x_ref, g_ref, sh_ref, sc_ref, w_ref, wf_ref, qkv_ref, fg_ref, h_scr):
    @pl.when(pl.program_id(1) == 0)
    def _():
        h = _rms_norm(x_ref[...], g_ref[...]) * (1.0 + sc_ref[0]) + sh_ref[0]
        hb = _mxu(h)
        h_scr[...] = hb
        fg_ref[...] = _dot(hb, wf_ref[...])

    qkv_ref[...] = _dot(h_scr[...], w_ref[...]).astype(qkv_ref.dtype)


def _in_proj(x2, g, shift, scale, w_qkv, w_f, seq):
    t, d = x2.shape
    n = w_qkv.shape[1]
    per_b = seq // TM_IN
    bmap = lambda i, j: (i // per_b, 0, 0)
    return pl.pallas_call(
        _in_proj_kernel,
        grid=(t // TM_IN, n // TN_IN),
        in_specs=[
            pl.BlockSpec((TM_IN, d), lambda i, j: (i, 0)),
            pl.BlockSpec((1, d), lambda i, j: (0, 0)),
            pl.BlockSpec((1, 1, d), bmap),
            pl.BlockSpec((1, 1, d), bmap),
            pl.BlockSpec((d, TN_IN), lambda i, j: (0, j)),
            pl.BlockSpec((d, LANES), lambda i, j: (0, 0)),
        ],
        out_specs=[
            pl.BlockSpec((TM_IN, TN_IN), lambda i, j: (i, j)),
            pl.BlockSpec((TM_IN, LANES), lambda i, j: (i, 0)),
        ],
        out_shape=[
            jax.ShapeDtypeStruct((t, n), MXU_DTYPE),
            jax.ShapeDtypeStruct((t, LANES), F32),
        ],
        scratch_shapes=[pltpu.VMEM((TM_IN, d), MXU_DTYPE)],
        compiler_params=_cparams("parallel", "arbitrary"),
        name="in_proj",
    )(x2, g, shift, scale, w_qkv, w_f)


def _forget_kernel(fg_ref, b_ref, f_ref):
    z = fg_ref[0] + b_ref[...]
    log_f = jnp.minimum(z, 0.0) - jnp.log(1.0 + jnp.exp(-jnp.abs(z)))
    cum = log_f.T[0:8]
    seq = cum.shape[1]
    lane = lax.broadcasted_iota(jnp.int32, cum.shape, 1)
    shift = 1
    while shift < seq:
        cum = cum + jnp.where(lane >= shift, pltpu.roll(cum, shift, 1), 0.0)
        shift *= 2
    f_ref[0] = cum


def _forget_cumsum(fg3, b_fgate):
    bsz, seq, _ = fg3.shape
    b_pad = jnp.pad(b_fgate, (0, LANES - b_fgate.shape[0])).reshape(1, LANES)
    return pl.pallas_call(
        _forget_kernel,
        grid=(bsz,),
        in_specs=[
            pl.BlockSpec((1, seq, LANES), lambda b: (b, 0, 0)),
            pl.BlockSpec((1, LANES), lambda b: (0, 0)),
        ],
        out_specs=pl.BlockSpec((1, 8, seq), lambda b: (b, 0, 0)),
        out_shape=jax.ShapeDtypeStruct((bsz, 8, seq), F32),
        compiler_params=_cparams("parallel"),
        name="forget_cumsum",
    )(fg3, b_pad)


def _flash_init(tq):
    return (jnp.full((tq, 1), NEG_INF, F32), jnp.zeros((tq, 1), F32),
            jnp.zeros((tq, HEAD_DIM), F32))


def _flash_update(carry, s, v):
    m, l, acc = carry
    m_new = jnp.maximum(m, jnp.max(s, axis=-1, keepdims=True))
    alpha = jnp.exp(m - m_new)
    p = jnp.exp(s - m_new)
    l = alpha * l + jnp.sum(p, axis=-1, keepdims=True)
    acc = alpha * acc + _dot(_mxu(p), v)
    return m_new, l, acc


def _causal_mask(s):
    row = lax.broadcasted_iota(jnp.int32, s.shape, 0)
    col = lax.broadcasted_iota(jnp.int32, s.shape, 1)
    return jnp.where(col <= row, s, NEG_INF)


def _qkv_specs(seq, head0):
    return [
        pl.BlockSpec((1, seq, HEAD_DIM), lambda b, h: (b, 0, head0 + h)),
        pl.BlockSpec((1, seq, HEAD_DIM), lambda b, h: (b, 0, N_HEADS + head0 + h)),
        pl.BlockSpec((1, seq, HEAD_DIM), lambda b, h: (b, 0, 2 * N_HEADS + head0 + h)),
    ]


def _rope_k_into(k_ref, cos_ref, sin_ref, k_scr, blk):
    seq = k_scr.shape[0]
    for j in range(seq // blk):
        rows = slice(j * blk, (j + 1) * blk)
        k_scr[rows] = _mxu(_rope(k_ref[0, rows].astype(F32), cos_ref[0, rows], sin_ref[0, rows]))


def _dilated_bias(tq):
    n_off = 2048 // tq
    off = jnp.arange(n_off, dtype=jnp.int32)[:, None, None] * tq
    delta = off + jnp.arange(tq, dtype=jnp.int32)[None, :, None] - jnp.arange(tq, dtype=jnp.int32)[None, None, :]
    count = jnp.zeros(delta.shape, F32)
    for window, dil in DILATED_BRANCHES:
        count = count + ((delta >= 0) & (delta % dil == 0) & (delta <= window)).astype(F32)
    return jnp.where(count > 0, jnp.log(jnp.maximum(count, 1.0)), NEG_INF)


def _dilated_kernel(q_ref, k_ref, v_ref, cos_ref, sin_ref, bias_ref, o_ref, k_scr):
    tq = TQ_ATT
    seq = k_scr.shape[0]
    _rope_k_into(k_ref, cos_ref, sin_ref, k_scr, tq)
    for qi in range(seq // tq):
        rows = slice(qi * tq, (qi + 1) * tq)
        q = _rope(q_ref[0, rows].astype(F32), cos_ref[0, rows], sin_ref[0, rows])
        q = _mxu(q * ATTN_SCALE)
        carry = _flash_init(tq)
        for kj in range(qi + 1):
            cols = slice(kj * tq, (kj + 1) * tq)
            s = _dot_nt(q, k_scr[cols]) + bias_ref[qi - kj]
            carry = _flash_update(carry, s, v_ref[0, cols])
        _, l, acc = carry
        o_ref[0, rows] = (acc / l).astype(o_ref.dtype)


def _dilated_attention(qkv3, cos, sin):
    bsz, seq, _ = qkv3.shape
    assert seq == 2048, "dilated windows are laid out for a 2048-token sequence"
    bias = _dilated_bias(TQ_ATT)
    tab = pl.BlockSpec((1, seq, LANES), lambda b, h: (b, 0, 0))
    return pl.pallas_call(
        _dilated_kernel,
        grid=(bsz, N_HEADS_DIL),
        in_specs=_qkv_specs(seq, 0) + [tab, tab,
                                       pl.BlockSpec(bias.shape, lambda b, h: (0, 0, 0))],
        out_specs=pl.BlockSpec((1, seq, HEAD_DIM), lambda b, h: (b, 0, h)),
        out_shape=jax.ShapeDtypeStruct((bsz, seq, N_HEADS_DIL * HEAD_DIM), MXU_DTYPE),
        scratch_shapes=[pltpu.VMEM((seq, HEAD_DIM), MXU_DTYPE)],
        compiler_params=_cparams("parallel", "parallel"),
        name="dilated_attention",
    )(qkv3, qkv3, qkv3, cos, sin, bias)


def _moba_kernel(q_ref, k_ref, v_ref, cos_ref, sin_ref, o_ref, k_scr, km_scr):
    blk = MOBA_BLOCK
    seq = k_scr.shape[0]
    n_blk = seq // blk
    km_scr[...] = jnp.zeros(km_scr.shape, F32)
    for j in range(n_blk):
        rows = slice(j * blk, (j + 1) * blk)
        kr = _rope(k_ref[0, rows].astype(F32), cos_ref[0, rows], sin_ref[0, rows])
        k_scr[rows] = _mxu(kr)
        km_scr[j:j + 1, :] = jnp.mean(kr, axis=0, keepdims=True)
    km_hi, km_lo = _split_hi_lo(km_scr[...])
    for qi in range(n_blk):
        rows = slice(qi * blk, (qi + 1) * blk)
        qf = _rope(q_ref[0, rows].astype(F32), cos_ref[0, rows], sin_ref[0, rows])
        if qi > 0:
            q_hi, q_lo = _split_hi_lo(qf)
            gate = _dot_nt(q_hi, km_hi) + _dot_nt(q_lo, km_hi) + _dot_nt(q_hi, km_lo)
            lane = lax.broadcasted_iota(jnp.int32, gate.shape, 1)
            past = lane < qi
            gate = jnp.where(past, gate, NEG_INF)
            rank = jnp.zeros(gate.shape, F32)
            for jp in range(qi):
                g_jp = gate[:, jp:jp + 1]
                ahead = (g_jp > gate) | ((g_jp == gate) & (lane > jp))
                rank = rank + jnp.where(ahead, 1.0, 0.0)
            sel_bias = jnp.where(past & (rank < MOBA_TOPK), 0.0, NEG_INF)
        q = _mxu(qf * ATTN_SCALE)
        carry = _flash_init(blk)
        for kj in range(qi + 1):
            cols = slice(kj * blk, (kj + 1) * blk)
            s = _dot_nt(q, k_scr[cols])
            if kj == qi:
                s = _causal_mask(s)
            else:
                s = s + sel_bias[:, kj:kj + 1]
            carry = _flash_update(carry, s, v_ref[0, cols])
        _, l, acc = carry
        o_ref[0, rows] = (acc / l).astype(o_ref.dtype)


def _moba_attention(qkv3, cos, sin):
    bsz, seq, _ = qkv3.shape
    assert seq % MOBA_BLOCK == 0 and seq // MOBA_BLOCK <= LANES
    tab = pl.BlockSpec((1, seq, LANES), lambda b, h: (b, 0, 0))
    return pl.pallas_call(
        _moba_kernel,
        grid=(bsz, N_HEADS_MOBA),
        in_specs=_qkv_specs(seq, N_HEADS_DIL) + [tab, tab],
        out_specs=pl.BlockSpec((1, seq, HEAD_DIM), lambda b, h: (b, 0, h)),
        out_shape=jax.ShapeDtypeStruct((bsz, seq, N_HEADS_MOBA * HEAD_DIM), MXU_DTYPE),
        scratch_shapes=[pltpu.VMEM((seq, HEAD_DIM), MXU_DTYPE),
                        pltpu.VMEM((LANES, HEAD_DIM), F32)],
        compiler_params=_cparams("parallel", "parallel"),
        name="moba_attention",
    )(qkv3, qkv3, qkv3, cos, sin)


def _fox_kernel(q_ref, k_ref, v_ref, f_ref, o_ref):
    tq = TQ_ATT
    seq = q_ref.shape[1]
    for qi in range(seq // tq):
        rows = slice(qi * tq, (qi + 1) * tq)
        q = _mxu(q_ref[0, rows].astype(F32) * ATTN_SCALE)
        carry = _flash_init(tq)
        for kj in range(qi + 1):
            cols = slice(kj * tq, (kj + 1) * tq)
            s = _dot_nt(q, k_ref[0, cols]) - f_ref[0, 0, :, cols]
            if kj == qi:
                s = _causal_mask(s)
            carry = _flash_update(carry, s, v_ref[0, cols])
        _, l, acc = carry
        o_ref[0, rows] = (acc / l).astype(o_ref.dtype)


def _fox_attention(qkv3, cum_f):
    bsz, seq, _ = qkv3.shape
    f4 = cum_f.reshape(bsz, cum_f.shape[1], 1, seq)
    return pl.pallas_call(
        _fox_kernel,
        grid=(bsz, N_HEADS_FOX),
        in_specs=_qkv_specs(seq, N_HEADS_DIL + N_HEADS_MOBA)
        + [pl.BlockSpec((1, 1, 1, seq), lambda b, h: (b, h, 0, 0))],
        out_specs=pl.BlockSpec((1, seq, HEAD_DIM), lambda b, h: (b, 0, h)),
        out_shape=jax.ShapeDtypeStruct((bsz, seq, N_HEADS_FOX * HEAD_DIM), MXU_DTYPE),
        compiler_params=_cparams("parallel", "parallel"),
        name="fox_attention",
    )(qkv3, qkv3, qkv3, f4)


def _out_proj_kernel(od_ref, om_ref, of_ref, wd_ref, wm_ref, wf_ref, x_ref, g_ref, gate_ref, o_ref):
    y = _dot(od_ref[...], wd_ref[...]) + _dot(om_ref[...], wm_ref[...]) + _dot(of_ref[...], wf_ref[...])
    o_ref[...] = x_ref[...] + gate_ref[0] * _rms_norm(y, g_ref[...])


def _out_proj(o_dil, o_moba, o_fox, w_out, x2, g, gate, seq):
    t, d = x2.shape
    nd, nm = o_dil.shape[1], o_moba.shape[1]
    w_d, w_m, w_f = w_out[:nd], w_out[nd:nd + nm], w_out[nd + nm:]
    per_b = seq // TM_OUT
    row = lambda i: (i, 0)
    fixed = lambda i: (0, 0)
    return pl.pallas_call(
        _out_proj_kernel,
        grid=(t // TM_OUT,),
        in_specs=[
            pl.BlockSpec((TM_OUT, nd), row),
            pl.BlockSpec((TM_OUT, nm), row),
            pl.BlockSpec((TM_OUT, o_fox.shape[1]), row),
            pl.BlockSpec(w_d.shape, fixed),
            pl.BlockSpec(w_m.shape, fixed),
            pl.BlockSpec(w_f.shape, fixed),
            pl.BlockSpec((TM_OUT, d), row),
            pl.BlockSpec((1, d), fixed),
            pl.BlockSpec((1, 1, d), lambda i: (i // per_b, 0, 0)),
        ],
        out_specs=pl.BlockSpec((TM_OUT, d), row),
        out_shape=jax.ShapeDtypeStruct((t, d), F32),
        compiler_params=_cparams("parallel"),
        name="out_proj",
    )(o_dil, o_moba, o_fox, w_d, w_m, w_f, x2, g, gate)


def _swiglu_tile(h, wg, wu, wd):
    g = _dot(h, wg)
    u = _dot(h, wu)
    return _dot(_mxu(g * _sigmoid(g) * u), wd)


def _ffn_kernel(x_ref, gpre_ref, sh_ref, sc_ref, wg_ref, wu_ref, wd_ref, gpost_ref, gate_ref,
                o_ref, h_scr, acc_scr):
    j = pl.program_id(1)

    @pl.when(j == 0)
    def _():
        h = _rms_norm(x_ref[...], gpre_ref[...]) * (1.0 + sc_ref[0]) + sh_ref[0]
        h_scr[...] = _mxu(h)
        acc_scr[...] = jnp.zeros(acc_scr.shape, F32)

    acc_scr[...] += _swiglu_tile(h_scr[...], wg_ref[...], wu_ref[...], wd_ref[...])

    @pl.when(j == pl.num_programs(1) - 1)
    def _():
        o_ref[...] = x_ref[...] + gate_ref[0] * _rms_norm(acc_scr[...], gpost_ref[...])


def _dense_ffn(x2, g_pre, shift, scale, w_gate, w_up, w_down, g_post, gate, seq):
    t, d = x2.shape
    d_ff = w_gate.shape[1]
    per_b = seq // TM_FFN
    row = lambda i, j: (i, 0)
    fixed = lambda i, j: (0, 0)
    bmap = lambda i, j: (i // per_b, 0, 0)
    return pl.pallas_call(
        _ffn_kernel,
        grid=(t // TM_FFN, d_ff // TF_FFN),
        in_specs=[
            pl.BlockSpec((TM_FFN, d), row),
            pl.BlockSpec((1, d), fixed),
            pl.BlockSpec((1, 1, d), bmap),
            pl.BlockSpec((1, 1, d), bmap),
            pl.BlockSpec((d, TF_FFN), lambda i, j: (0, j)),
            pl.BlockSpec((d, TF_FFN), lambda i, j: (0, j)),
            pl.BlockSpec((TF_FFN, d), lambda i, j: (j, 0)),
            pl.BlockSpec((1, d), fixed),
            pl.BlockSpec((1, 1, d), bmap),
        ],
        out_specs=pl.BlockSpec((TM_FFN, d), row),
        out_shape=jax.ShapeDtypeStruct((t, d), F32),
        scratch_shapes=[pltpu.VMEM((TM_FFN, d), MXU_DTYPE), pltpu.VMEM((TM_FFN, d), F32)],
        compiler_params=_cparams("parallel", "arbitrary"),
        name="dense_ffn",
    )(x2, g_pre, shift, scale, w_gate, w_up, w_down, g_post, gate)


def _router_kernel(x_ref, g_ref, sh_ref, sc_ref, whi_ref, wlo_ref, br_ref, h_ref, gw_ref, gi_ref):
    h = _rms_norm(x_ref[...], g_ref[...]) * (1.0 + sc_ref[0]) + sh_ref[0]
    h_ref[...] = h
    h_hi, h_lo = _split_hi_lo(h)
    logits = (_dot(h_hi, whi_ref[...]) + _dot(h_lo, whi_ref[...]) + _dot(h_hi, wlo_ref[...])
              + br_ref[...])
    lane = lax.broadcasted_iota(jnp.int32, logits.shape, 1)
    logits = jnp.where(lane < N_EXPERTS, logits, NEG_INF)
    m1 = jnp.max(logits, axis=-1, keepdims=True)
    i1 = jnp.min(jnp.where(logits == m1, lane, LANES), axis=-1, keepdims=True)
    rest = jnp.where(lane == i1, NEG_INF, logits)
    m2 = jnp.max(rest, axis=-1, keepdims=True)
    i2 = jnp.min(jnp.where(rest == m2, lane, LANES), axis=-1, keepdims=True)
    e2 = jnp.exp(m2 - m1)
    p1 = 1.0 / (1.0 + e2)
    p2 = e2 * p1
    gw_ref[...] = jnp.where(lane == 0, p1, jnp.where(lane == 1, p2, 0.0))
    gi_ref[...] = jnp.where(lane == 0, i1, jnp.where(lane == 1, i2, 0))


def _router(x2, g_pre, shift, scale, w_router, b_router, seq):
    t, d = x2.shape
    tm = TM_OUT
    w_pad = jnp.pad(w_router, ((0, 0), (0, LANES - N_EXPERTS)))
    w_hi = w_pad.astype(MXU_DTYPE)
    w_lo = (w_pad - w_hi.astype(F32)).astype(MXU_DTYPE)
    b_pad = jnp.pad(b_router, (0, LANES - N_EXPERTS)).reshape(1, LANES)
    per_b = seq // tm
    row = lambda i: (i, 0)
    fixed = lambda i: (0, 0)
    bmap = lambda i: (i // per_b, 0, 0)
    return pl.pallas_call(
        _router_kernel,
        grid=(t // tm,),
        in_specs=[
            pl.BlockSpec((tm, d), row),
            pl.BlockSpec((1, d), fixed),
            pl.BlockSpec((1, 1, d), bmap),
            pl.BlockSpec((1, 1, d), bmap),
            pl.BlockSpec((d, LANES), fixed),
            pl.BlockSpec((d, LANES), fixed),
            pl.BlockSpec((1, LANES), fixed),
        ],
        out_specs=[pl.BlockSpec((tm, d), row), pl.BlockSpec((tm, LANES), row),
                   pl.BlockSpec((tm, LANES), row)],
        out_shape=[jax.ShapeDtypeStruct((t, d), F32), jax.ShapeDtypeStruct((t, LANES), F32),
                   jax.ShapeDtypeStruct((t, LANES), jnp.int32)],
        compiler_params=_cparams("parallel"),
        name="router",
    )(x2, g_pre, shift, scale, w_hi, w_lo, b_pad)


def _routing_tables(top_idx, n_tiles):
    t = top_idx.shape[0]
    flat_e = top_idx.reshape(-1)
    onehot = (flat_e[:, None] == jnp.arange(N_EXPERTS, dtype=jnp.int32)[None, :]).astype(jnp.int32)
    csum = jnp.cumsum(onehot, axis=0)
    rank = jnp.take_along_axis(csum, flat_e[:, None], axis=1)[:, 0] - 1
    counts = csum[-1]
    padded = ((counts + TM_MOE - 1) // TM_MOE) * TM_MOE
    ends = jnp.cumsum(padded)
    starts = ends - padded
    pos = (starts[flat_e] + rank).astype(jnp.int32)
    n_rows = n_tiles * TM_MOE
    row_src = jnp.zeros((n_rows,), jnp.int32).at[pos].set(jnp.arange(TOP_K * t, dtype=jnp.int32) // TOP_K)
    tile_start = jnp.arange(n_tiles, dtype=jnp.int32) * TM_MOE
    tile_expert = jnp.minimum(jnp.searchsorted(ends, tile_start, side="right"), N_EXPERTS - 1).astype(jnp.int32)
    n_active = (ends[-1] // TM_MOE).astype(jnp.int32).reshape(1)
    return pos, row_src, tile_expert, n_active


def _row_copy(src_hbm, src_row, dst_vmem, dst_row, sem):
    return pltpu.make_async_copy(src_hbm.at[pl.ds(src_row, 1)], dst_vmem.at[pl.ds(dst_row, 1)], sem)


def _gather_kernel(src_ref, nact_ref, h_hbm, o_ref, buf, sem):
    i = pl.program_id(0)
    tm = buf.shape[0]

    @pl.when(i < nact_ref[0])
    def _():
        base = i * tm

        def issue(r, c):
            _row_copy(h_hbm, src_ref[base + r], buf, r, sem).start()
            return c

        def wait(r, c):
            _row_copy(h_hbm, 0, buf, r, sem).wait()
            return c

        lax.fori_loop(0, tm, issue, 0)
        lax.fori_loop(0, tm, wait, 0)
        o_ref[...] = buf[...].astype(o_ref.dtype)

    @pl.when(i >= nact_ref[0])
    def _():
        o_ref[...] = jnp.zeros(o_ref.shape, o_ref.dtype)


def _gather_rows(h, row_src, n_active, n_tiles):
    d = h.shape[1]
    return pl.pallas_call(
        _gather_kernel,
        grid_spec=pltpu.PrefetchScalarGridSpec(
            num_scalar_prefetch=2,
            grid=(n_tiles,),
            in_specs=[pl.BlockSpec(memory_space=pl.ANY)],
            out_specs=pl.BlockSpec((TM_MOE, d), lambda i, src, nact: (i, 0)),
            scratch_shapes=[pltpu.VMEM((TM_MOE, d), F32), pltpu.SemaphoreType.DMA(())],
        ),
        out_shape=jax.ShapeDtypeStruct((n_tiles * TM_MOE, d), MXU_DTYPE),
        compiler_params=_cparams("arbitrary"),
        name="expert_gather",
    )(row_src, n_active, h)


def _expert_kernel(te_ref, nact_ref, h_ref, wg_ref, wu_ref, wd_ref, o_ref, acc_scr):
    i, j = pl.program_id(0), pl.program_id(1)
    active = i < nact_ref[0]

    @pl.when(j == 0)
    def _():
        acc_scr[...] = jnp.zeros(acc_scr.shape, F32)

    @pl.when(active)
    def _():
        acc_scr[...] += _swiglu_tile(h_ref[...], wg_ref[0], wu_ref[0], wd_ref[0])

    @pl.when(j == pl.num_programs(1) - 1)
    def _():
        o_ref[...] = acc_scr[...]


def _expert_ffn(hs, tile_expert, n_active, w_gate, w_up, w_down):
    n_rows, d = hs.shape
    d_ff = w_gate.shape[2]
    n_tiles = n_rows // TM_MOE
    n_ff = d_ff // TF_FFN

    def ff_step(i, j, nact):
        return jnp.where(i < nact[0], j, n_ff - 1)

    return pl.pallas_call(
        _expert_kernel,
        grid_spec=pltpu.PrefetchScalarGridSpec(
            num_scalar_prefetch=2,
            grid=(n_tiles, n_ff),
            in_specs=[
                pl.BlockSpec((TM_MOE, d), lambda i, j, te, nact: (i, 0)),
                pl.BlockSpec((1, d, TF_FFN), lambda i, j, te, nact: (te[i], 0, ff_step(i, j, nact))),
                pl.BlockSpec((1, d, TF_FFN), lambda i, j, te, nact: (te[i], 0, ff_step(i, j, nact))),
                pl.BlockSpec((1, TF_FFN, d), lambda i, j, te, nact: (te[i], ff_step(i, j, nact), 0)),
            ],
            out_specs=pl.BlockSpec((TM_MOE, d), lambda i, j, te, nact: (i, 0)),
            scratch_shapes=[pltpu.VMEM((TM_MOE, d), F32)],
        ),
        out_shape=jax.ShapeDtypeStruct((n_rows, d), F32),
        compiler_params=_cparams("arbitrary", "arbitrary"),
        name="expert_ffn",
    )(tile_expert, n_active, hs, w_gate, w_up, w_down)


def _combine_kernel(pos_ref, ys_hbm, gw_ref, x_ref, g_ref, gate_ref, o_ref, buf, sem):
    i = pl.program_id(0)
    tm = x_ref.shape[0]
    base = i * tm

    def issue(r, c):
        for slot in range(TOP_K):
            _row_copy(ys_hbm, pos_ref[TOP_K * (base + r) + slot], buf.at[slot], r, sem).start()
        return c

    def wait(r, c):
        for slot in range(TOP_K):
            _row_copy(ys_hbm, 0, buf.at[slot], r, sem).wait()
        return c

    lax.fori_loop(0, tm, issue, 0)
    lax.fori_loop(0, tm, wait, 0)
    gw = gw_ref[...]
    y = gw[:, 0:1] * buf[0]
    for slot in range(1, TOP_K):
        y = y + gw[:, slot:slot + 1] * buf[slot]
    o_ref[...] = x_ref[...] + gate_ref[0] * _rms_norm(y, g_ref[...])


def _combine(ys, pos, gate_w, x2, g_post, gate, seq):
    t, d = x2.shape
    tm = TM_CMB
    per_b = seq // tm
    return pl.pallas_call(
        _combine_kernel,
        grid_spec=pltpu.PrefetchScalarGridSpec(
            num_scalar_prefetch=1,
            grid=(t // tm,),
            in_specs=[
                pl.BlockSpec(memory_space=pl.ANY),
                pl.BlockSpec((tm, LANES), lambda i, pos: (i, 0)),
                pl.BlockSpec((tm, d), lambda i, pos: (i, 0)),
                pl.BlockSpec((1, d), lambda i, pos: (0, 0)),
                pl.BlockSpec((1, 1, d), lambda i, pos: (i // per_b, 0, 0)),
            ],
            out_specs=pl.BlockSpec((tm, d), lambda i, pos: (i, 0)),
            scratch_shapes=[pltpu.VMEM((TOP_K, tm, d), F32), pltpu.SemaphoreType.DMA(())],
        ),
        out_shape=jax.ShapeDtypeStruct((t, d), F32),
        compiler_params=_cparams("arbitrary"),
        name="expert_combine",
    )(pos, ys, gate_w, x2, g_post, gate)


def _moe_ffn(x2, g_pre, shift, scale, w_router, b_router, w_gate, w_up, w_down, g_post, gate, seq):
    t = x2.shape[0]
    n_tiles = (TOP_K * t) // TM_MOE + N_EXPERTS
    h, gate_w, top = _router(x2, g_pre, shift, scale, w_router, b_router, seq)
    pos, row_src, tile_expert, n_active = _routing_tables(top[:, :TOP_K], n_tiles)
    hs = _gather_rows(h, row_src, n_active, n_tiles)
    ys = _expert_ffn(hs, tile_expert, n_active, w_gate, w_up, w_down)
    return _combine(ys, pos, gate_w, x2, g_post, gate, seq)


def kernel(x, c, positions, ln_mix_pre, ln_mix_post, ln_ffn_pre, ln_ffn_post, w_mod, b_mod, w_in, b_fgate, w_out, w_ffn_gate, w_ffn_up, w_ffn_down, w_router, b_router, w_exp_gate, w_exp_up, w_exp_down):
    bsz, seq, d = x.shape
    depth = w_mod.shape[0]
    n_qkv = 3 * N_HEADS * HEAD_DIM
    x2 = x.reshape(bsz * seq, d)
    mod = _modulation(c, w_mod, b_mod).reshape(depth, bsz, N_MOD, 1, d)
    cos, sin = _rope_tables(positions)
    for layer in range(depth):
        sh_m, sc_m, g_m, sh_f, sc_f, g_f = (mod[layer, :, k] for k in range(N_MOD))
        row = lambda a: a[layer].reshape(1, d)
        w_qkv = _mxu(w_in[layer, :, :n_qkv])
        w_f = _mxu(jnp.pad(w_in[layer, :, n_qkv:], ((0, 0), (0, LANES - N_HEADS_FOX))))
        qkv, fg = _in_proj(x2, row(ln_mix_pre), sh_m, sc_m, w_qkv, w_f, seq)
        qkv3 = qkv.reshape(bsz, seq, n_qkv)
        cum_f = _forget_cumsum(fg.reshape(bsz, seq, LANES), b_fgate[layer])
        o_dil = _dilated_attention(qkv3, cos, sin).reshape(bsz * seq, -1)
        o_moba = _moba_attention(qkv3, cos, sin).reshape(bsz * seq, -1)
        o_fox = _fox_attention(qkv3, cum_f).reshape(bsz * seq, -1)
        x2 = _out_proj(o_dil, o_moba, o_fox, _mxu(w_out[layer]), x2, row(ln_mix_post), g_m, seq)
        j = layer // 2
        if layer % 2 == 0:
            x2 = _dense_ffn(x2, row(ln_ffn_pre), sh_f, sc_f, _mxu(w_ffn_gate[j]), _mxu(w_ffn_up[j]),
                            _mxu(w_ffn_down[j]), row(ln_ffn_post), g_f, seq)
        else:
            x2 = _moe_ffn(x2, row(ln_ffn_pre), sh_f, sc_f, w_router[j], b_router[j],
                          _mxu(w_exp_gate[j]), _mxu(w_exp_up[j]), _mxu(w_exp_down[j]),
                          row(ln_ffn_post), g_f, seq)
    return x2.reshape(bsz, seq, d)
```

```python
import functools

import numpy as np
import jax
import jax.numpy as jnp
from jax import lax
from jax.experimental import pallas as pl
from jax.experimental.pallas import tpu as pltpu

F32 = jnp.float32
MXU_DTYPE = jnp.bfloat16

HEAD_DIM = 128
N_HEADS = 16
N_HEADS_DIL = 6
N_HEADS_MOBA = 4
N_HEADS_FOX = 6
DILATED_BRANCHES = ((128, 1), (512, 4), (2048, 16))
MOBA_BLOCK = 256
MOBA_TOPK = 3
ROPE_THETA = 500000.0
ROT_DIM = HEAD_DIM // 4
ROT_HALF = ROT_DIM // 2
N_EXPERTS = 8
TOP_K = 2
N_MOD = 6
RMS_EPS = 1e-6
NEG_INF = -1e30
ATTN_SCALE = HEAD_DIM ** -0.5
LANES = 128

VMEM_LIMIT_BYTES = 56 * 1024 * 1024

TM_IN = 1024
TN_IN = 768
TM_OUT = 512
TM_FFN = 512
TF_FFN = 1024
TM_DSP = 256
DMA_UNROLL = 8
TM_MOE = 512
TM_CMB = 256
TQ_ATT = 512
TN_MOD = 1024


def _cparams(*sem):
    return pltpu.CompilerParams(dimension_semantics=sem, vmem_limit_bytes=VMEM_LIMIT_BYTES)


def _mxu(a):
    return a.astype(MXU_DTYPE)


def _dot(a, b):
    return jnp.dot(a, b, preferred_element_type=F32)


def _dot_nt(a, b):
    return lax.dot_general(a, b, (((1,), (1,)), ((), ())), preferred_element_type=F32)


def _split_hi_lo(a):
    hi = a.astype(MXU_DTYPE)
    lo = (a - hi.astype(F32)).astype(MXU_DTYPE)
    return hi, lo


def _rms_norm(x, g):
    ms = jnp.mean(x * x, axis=-1, keepdims=True)
    return x * lax.rsqrt(ms + RMS_EPS) * g


def _sigmoid(x):
    return 1.0 / (1.0 + jnp.exp(-x))


def _mod_kernel(c_ref, w_ref, b_ref, o_ref):
    c = c_ref[...]
    cond = c * _sigmoid(c)
    o_ref[0] = _dot(_mxu(cond), _mxu(w_ref[0])) + b_ref[0]


def _modulation(c, w_mod, b_mod):
    depth, d, n = w_mod.shape
    bsz = c.shape[0]
    rows = max(16, bsz)
    c_pad = jnp.pad(c, ((0, rows - bsz), (0, 0)))
    out = pl.pallas_call(
        _mod_kernel,
        grid=(depth, n // TN_MOD),
        in_specs=[
            pl.BlockSpec((rows, d), lambda l, j: (0, 0)),
            pl.BlockSpec((1, d, TN_MOD), lambda l, j: (l, 0, j)),
            pl.BlockSpec((1, 1, TN_MOD), lambda l, j: (l, 0, j)),
        ],
        out_specs=pl.BlockSpec((1, rows, TN_MOD), lambda l, j: (l, 0, j)),
        out_shape=jax.ShapeDtypeStruct((depth, rows, n), F32),
        compiler_params=_cparams("parallel", "parallel"),
        name="modulation",
    )(c_pad, w_mod, b_mod.reshape(depth, 1, n))
    return out[:, :bsz]


def _rope_table_kernel(pos_ref, freq_ref, cos_ref, sin_ref):
    ang = pos_ref[0].astype(F32) * freq_ref[...]
    lane = lax.broadcasted_iota(jnp.int32, ang.shape, 1)
    cos_ref[0] = jnp.where(lane < ROT_DIM, jnp.cos(ang), 1.0)
    sn = jnp.sin(ang)
    sin_ref[0] = jnp.where(lane < ROT_HALF, -sn, jnp.where(lane < ROT_DIM, sn, 0.0))


def _rope_tables(positions):
    bsz, seq = positions.shape
    inv_freq = ROPE_THETA ** (-np.arange(ROT_HALF, dtype=np.float32) / ROT_HALF)
    freq = np.zeros((1, LANES), np.float32)
    freq[0, :ROT_HALF] = inv_freq
    freq[0, ROT_HALF:ROT_DIM] = inv_freq
    tab = jax.ShapeDtypeStruct((bsz, seq, LANES), F32)
    return pl.pallas_call(
        _rope_table_kernel,
        grid=(bsz,),
        in_specs=[
            pl.BlockSpec((1, seq, 1), lambda b: (b, 0, 0)),
            pl.BlockSpec((1, LANES), lambda b: (0, 0)),
        ],
        out_specs=[pl.BlockSpec((1, seq, LANES), lambda b: (b, 0, 0))] * 2,
        out_shape=[tab, tab],
        compiler_params=_cparams("parallel"),
        name="rope_tables",
    )(positions.reshape(bsz, seq, 1), jnp.asarray(freq))


def _rope(t, cos, sin_signed):
    lane = lax.broadcasted_iota(jnp.int32, t.shape, 1)
    partner = jnp.where(lane < ROT_HALF,
                        pltpu.roll(t, LANES - ROT_HALF, 1),
                        pltpu.roll(t, ROT_HALF, 1))
    return t * cos + partner * sin_signed


def _in_proj_kernel(x_ref, g_ref, sh_ref, sc_ref, w_ref, wf_ref, qkv_ref, fg_ref, h_scr):
    @pl.when(pl.program_id(1) == 0)
    def _():
        h = _rms_norm(x_ref[...], g_ref[...]) * (1.0 + sc_ref[0]) + sh_ref[0]
        hb = _mxu(h)
        h_scr[...] = hb
        fg_ref[...] = _dot(hb, wf_ref[...])

    qkv_ref[...] = _dot(h_scr[...], _mxu(w_ref[0])).astype(qkv_ref.dtype)


def _in_proj(x2, g, shift, scale, w_in, layer, n, w_f, seq):
    t, d = x2.shape
    per_b = seq // TM_IN
    bmap = lambda i, j: (i // per_b, 0, 0)
    return pl.pallas_call(
        _in_proj_kernel,
        grid=(t // TM_IN, n // TN_IN),
        in_specs=[
            pl.BlockSpec((TM_IN, d), lambda i, j: (i, 0)),
            pl.BlockSpec((1, d), lambda i, j: (0, 0)),
            pl.BlockSpec((1, 1, d), bmap),
            pl.BlockSpec((1, 1, d), bmap),
            pl.BlockSpec((1, d, TN_IN), lambda i, j: (layer, 0, j)),
            pl.BlockSpec((d, LANES), lambda i, j: (0, 0)),
        ],
        out_specs=[
            pl.BlockSpec((TM_IN, TN_IN), lambda i, j: (i, j)),
            pl.BlockSpec((TM_IN, LANES), lambda i, j: (i, 0)),
        ],
        out_shape=[
            jax.ShapeDtypeStruct((t, n), MXU_DTYPE),
            jax.ShapeDtypeStruct((t, LANES), F32),
        ],
        scratch_shapes=[pltpu.VMEM((TM_IN, d), MXU_DTYPE)],
        compiler_params=_cparams("parallel", "arbitrary"),
        name="in_proj",
    )(x2, g, shift, scale, w_in, w_f)


def _forget_kernel(fg_ref, b_ref, f_ref):
    z = fg_ref[0] + b_ref[...]
    log_f = jnp.minimum(z, 0.0) - jnp.log(1.0 + jnp.exp(-jnp.abs(z)))
    cum = log_f.T[0:8]
    seq = cum.shape[1]
    lane = lax.broadcasted_iota(jnp.int32, cum.shape, 1)
    shift = 1
    while shift < seq:
        cum = cum + jnp.where(lane >= shift, pltpu.roll(cum, shift, 1), 0.0)
        shift *= 2
    f_ref[0] = cum


def _forget_cumsum(fg3, b_fgate):
    bsz, seq, _ = fg3.shape
    b_pad = jnp.pad(b_fgate, (0, LANES - b_fgate.shape[0])).reshape(1, LANES)
    return pl.pallas_call(
        _forget_kernel,
        grid=(bsz,),
        in_specs=[
            pl.BlockSpec((1, seq, LANES), lambda b: (b, 0, 0)),
            pl.BlockSpec((1, LANES), lambda b: (0, 0)),
        ],
        out_specs=pl.BlockSpec((1, 8, seq), lambda b: (b, 0, 0)),
        out_shape=jax.ShapeDtypeStruct((bsz, 8, seq), F32),
        compiler_params=_cparams("parallel"),
        name="forget_cumsum",
    )(fg3, b_pad)


def _flash_init(tq):
    return (jnp.full((tq, 1), NEG_INF, F32), jnp.zeros((tq, 1), F32),
            jnp.zeros((tq, HEAD_DIM), F32))


def _flash_update(carry, s, v):
    m, l, acc = carry
    m_new = jnp.maximum(m, jnp.max(s, axis=-1, keepdims=True))
    alpha = jnp.exp(m - m_new)
    p = jnp.exp(s - m_new)
    l = alpha * l + jnp.sum(p, axis=-1, keepdims=True)
    acc = alpha * acc + _dot(_mxu(p), v)
    return m_new, l, acc


def _causal_mask(s):
    row = lax.broadcasted_iota(jnp.int32, s.shape, 0)
    col = lax.broadcasted_iota(jnp.int32, s.shape, 1)
    return jnp.where(col <= row, s, NEG_INF)


def _qkv_specs(seq, head0):
    return [
        pl.BlockSpec((1, seq, HEAD_DIM), lambda b, h: (b, 0, head0 + h)),
        pl.BlockSpec((1, seq, HEAD_DIM), lambda b, h: (b, 0, N_HEADS + head0 + h)),
        pl.BlockSpec((1, seq, HEAD_DIM), lambda b, h: (b, 0, 2 * N_HEADS + head0 + h)),
    ]


def _rope_k_into(k_ref, cos_ref, sin_ref, k_scr, blk):
    seq = k_scr.shape[0]
    for j in range(seq // blk):
        rows = slice(j * blk, (j + 1) * blk)
        k_scr[rows] = _mxu(_rope(k_ref[0, rows].astype(F32), cos_ref[0, rows], sin_ref[0, rows]))


def _dilated_bias(tq):
    n_off = 2048 // tq
    off = jnp.arange(n_off, dtype=jnp.int32)[:, None, None] * tq
    delta = off + jnp.arange(tq, dtype=jnp.int32)[None, :, None] - jnp.arange(tq, dtype=jnp.int32)[None, None, :]
    count = jnp.zeros(delta.shape, F32)
    for window, dil in DILATED_BRANCHES:
        count = count + ((delta >= 0) & (delta % dil == 0) & (delta <= window)).astype(F32)
    return jnp.where(count > 0, jnp.log(jnp.maximum(count, 1.0)), NEG_INF)


def _dilated_kernel(q_ref, k_ref, v_ref, cos_ref, sin_ref, bias_ref, o_ref, k_scr):
    tq = TQ_ATT
    seq = k_scr.shape[0]
    _rope_k_into(k_ref, cos_ref, sin_ref, k_scr, tq)
    for qi in range(seq // tq):
        rows = slice(qi * tq, (qi + 1) * tq)
        q = _rope(q_ref[0, rows].astype(F32), cos_ref[0, rows], sin_ref[0, rows])
        q = _mxu(q * ATTN_SCALE)
        carry = _flash_init(tq)
        for kj in range(qi + 1):
            cols = slice(kj * tq, (kj + 1) * tq)
            s = _dot_nt(q, k_scr[cols]) + bias_ref[qi - kj]
            carry = _flash_update(carry, s, v_ref[0, cols])
        _, l, acc = carry
        o_ref[0, rows] = (acc / l).astype(o_ref.dtype)


def _dilated_attention(qkv3, cos, sin):
    bsz, seq, _ = qkv3.shape
    assert seq == 2048, "dilated windows are laid out for a 2048-token sequence"
    bias = _dilated_bias(TQ_ATT)
    tab = pl.BlockSpec((1, seq, LANES), lambda b, h: (b, 0, 0))
    return pl.pallas_call(
        _dilated_kernel,
        grid=(bsz, N_HEADS_DIL),
        in_specs=_qkv_specs(seq, 0) + [tab, tab,
                                       pl.BlockSpec(bias.shape, lambda b, h: (0, 0, 0))],
        out_specs=pl.BlockSpec((1, seq, HEAD_DIM), lambda b, h: (b, 0, h)),
        out_shape=jax.ShapeDtypeStruct((bsz, seq, N_HEADS_DIL * HEAD_DIM), MXU_DTYPE),
        scratch_shapes=[pltpu.VMEM((seq, HEAD_DIM), MXU_DTYPE)],
        compiler_params=_cparams("parallel", "parallel"),
        name="dilated_attention",
    )(qkv3, qkv3, qkv3, cos, sin, bias)


def _moba_kernel(q_ref, k_ref, v_ref, cos_ref, sin_ref, o_ref, k_scr, km_scr):
    blk = MOBA_BLOCK
    seq = k_scr.shape[0]
    n_blk = seq // blk
    km_scr[...] = jnp.zeros(km_scr.shape, F32)
    for j in range(n_blk):
        rows = slice(j * blk, (j + 1) * blk)
        kr = _rope(k_ref[0, rows].astype(F32), cos_ref[0, rows], sin_ref[0, rows])
        k_scr[rows] = _mxu(kr)
        km_scr[j:j + 1, :] = jnp.mean(kr, axis=0, keepdims=True)
    km_hi, km_lo = _split_hi_lo(km_scr[...])
    for qi in range(n_blk):
        rows = slice(qi * blk, (qi + 1) * blk)
        qf = _rope(q_ref[0, rows].astype(F32), cos_ref[0, rows], sin_ref[0, rows])
        if qi > 0:
            q_hi, q_lo = _split_hi_lo(qf)
            gate = _dot_nt(q_hi, km_hi) + _dot_nt(q_lo, km_hi) + _dot_nt(q_hi, km_lo)
            lane = lax.broadcasted_iota(jnp.int32, gate.shape, 1)
            past = lane < qi
            gate = jnp.where(past, gate, NEG_INF)
            rank = jnp.zeros(gate.shape, F32)
            for jp in range(qi):
                g_jp = gate[:, jp:jp + 1]
                ahead = (g_jp > gate) | ((g_jp == gate) & (lane > jp))
                rank = rank + jnp.where(ahead, 1.0, 0.0)
            sel_bias = jnp.where(past & (rank < MOBA_TOPK), 0.0, NEG_INF)
        q = _mxu(qf * ATTN_SCALE)
        carry = _flash_init(blk)
        for kj in range(qi + 1):
            cols = slice(kj * blk, (kj + 1) * blk)
            s = _dot_nt(q, k_scr[cols])
            if kj == qi:
                s = _causal_mask(s)
            else:
                s = s + sel_bias[:, kj:kj + 1]
            carry = _flash_update(carry, s, v_ref[0, cols])
        _, l, acc = carry
        o_ref[0, rows] = (acc / l).astype(o_ref.dtype)


def _moba_attention(qkv3, cos, sin):
    bsz, seq, _ = qkv3.shape
    assert seq % MOBA_BLOCK == 0 and seq // MOBA_BLOCK <= LANES
    tab = pl.BlockSpec((1, seq, LANES), lambda b, h: (b, 0, 0))
    return pl.pallas_call(
        _moba_kernel,
        grid=(bsz, N_HEADS_MOBA),
        in_specs=_qkv_specs(seq, N_HEADS_DIL) + [tab, tab],
        out_specs=pl.BlockSpec((1, seq, HEAD_DIM), lambda b, h: (b, 0, h)),
        out_shape=jax.ShapeDtypeStruct((bsz, seq, N_HEADS_MOBA * HEAD_DIM), MXU_DTYPE),
        scratch_shapes=[pltpu.VMEM((seq, HEAD_DIM), MXU_DTYPE),
                        pltpu.VMEM((LANES, HEAD_DIM), F32)],
        compiler_params=_cparams("parallel", "parallel"),
        name="moba_attention",
    )(qkv3, qkv3, qkv3, cos, sin)


def _fox_kernel(q_ref, k_ref, v_ref, f_ref, o_ref):
    tq = TQ_ATT
    seq = q_ref.shape[1]
    for qi in range(seq // tq):
        rows = slice(qi * tq, (qi + 1) * tq)
        q = _mxu(q_ref[0, rows].astype(F32) * ATTN_SCALE)
        carry = _flash_init(tq)
        for kj in range(qi + 1):
            cols = slice(kj * tq, (kj + 1) * tq)
            s = _dot_nt(q, k_ref[0, cols]) - f_ref[0, 0, :, cols]
            if kj == qi:
                s = _causal_mask(s)
            carry = _flash_update(carry, s, v_ref[0, cols])
        _, l, acc = carry
        o_ref[0, rows] = (acc / l).astype(o_ref.dtype)


def _fox_attention(qkv3, cum_f):
    bsz, seq, _ = qkv3.shape
    f4 = cum_f.reshape(bsz, cum_f.shape[1], 1, seq)
    return pl.pallas_call(
        _fox_kernel,
        grid=(bsz, N_HEADS_FOX),
        in_specs=_qkv_specs(seq, N_HEADS_DIL + N_HEADS_MOBA)
        + [pl.BlockSpec((1, 1, 1, seq), lambda b, h: (b, h, 0, 0))],
        out_specs=pl.BlockSpec((1, seq, HEAD_DIM), lambda b, h: (b, 0, h)),
        out_shape=jax.ShapeDtypeStruct((bsz, seq, N_HEADS_FOX * HEAD_DIM), MXU_DTYPE),
        compiler_params=_cparams("parallel", "parallel"),
        name="fox_attention",
    )(qkv3, qkv3, qkv3, f4)


def _out_proj_kernel(od_ref, om_ref, of_ref, wd_ref, wm_ref, wf_ref, x_ref, g_ref, gate_ref, o_ref):
    y = _dot(od_ref[...], wd_ref[...]) + _dot(om_ref[...], wm_ref[...]) + _dot(of_ref[...], wf_ref[...])
    o_ref[...] = x_ref[...] + gate_ref[0] * _rms_norm(y, g_ref[...])


def _out_proj(o_dil, o_moba, o_fox, w_out, x2, g, gate, seq):
    t, d = x2.shape
    nd, nm = o_dil.shape[1], o_moba.shape[1]
    w_d, w_m, w_f = w_out[:nd], w_out[nd:nd + nm], w_out[nd + nm:]
    per_b = seq // TM_OUT
    row = lambda i: (i, 0)
    fixed = lambda i: (0, 0)
    return pl.pallas_call(
        _out_proj_kernel,
        grid=(t // TM_OUT,),
        in_specs=[
            pl.BlockSpec((TM_OUT, nd), row),
            pl.BlockSpec((TM_OUT, nm), row),
            pl.BlockSpec((TM_OUT, o_fox.shape[1]), row),
            pl.BlockSpec(w_d.shape, fixed),
            pl.BlockSpec(w_m.shape, fixed),
            pl.BlockSpec(w_f.shape, fixed),
            pl.BlockSpec((TM_OUT, d), row),
            pl.BlockSpec((1, d), fixed),
            pl.BlockSpec((1, 1, d), lambda i: (i // per_b, 0, 0)),
        ],
        out_specs=pl.BlockSpec((TM_OUT, d), row),
        out_shape=jax.ShapeDtypeStruct((t, d), F32),
        compiler_params=_cparams("parallel"),
        name="out_proj",
    )(o_dil, o_moba, o_fox, w_d, w_m, w_f, x2, g, gate)


def _swiglu_tile(h, wg, wu, wd):
    g = _dot(h, wg)
    u = _dot(h, wu)
    return _dot(_mxu(g * _sigmoid(g) * u), wd)


def _ffn_kernel(x_ref, gpre_ref, sh_ref, sc_ref, wg_ref, wu_ref, wd_ref, gpost_ref, gate_ref,
                o_ref, h_scr, acc_scr):
    j = pl.program_id(1)

    @pl.when(j == 0)
    def _():
        h = _rms_norm(x_ref[...], gpre_ref[...]) * (1.0 + sc_ref[0]) + sh_ref[0]
        h_scr[...] = _mxu(h)
        acc_scr[...] = jnp.zeros(acc_scr.shape, F32)

    acc_scr[...] += _swiglu_tile(h_scr[...], wg_ref[...], wu_ref[...], wd_ref[...])

    @pl.when(j == pl.num_programs(1) - 1)
    def _():
        o_ref[...] = x_ref[...] + gate_ref[0] * _rms_norm(acc_scr[...], gpost_ref[...])


def _dense_ffn(x2, g_pre, shift, scale, w_gate, w_up, w_down, g_post, gate, seq):
    t, d = x2.shape
    d_ff = w_gate.shape[1]
    per_b = seq // TM_FFN
    row = lambda i, j: (i, 0)
    fixed = lambda i, j: (0, 0)
    bmap = lambda i, j: (i // per_b, 0, 0)
    return pl.pallas_call(
        _ffn_kernel,
        grid=(t // TM_FFN, d_ff // TF_FFN),
        in_specs=[
            pl.BlockSpec((TM_FFN, d), row),
            pl.BlockSpec((1, d), fixed),
            pl.BlockSpec((1, 1, d), bmap),
            pl.BlockSpec((1, 1, d), bmap),
            pl.BlockSpec((d, TF_FFN), lambda i, j: (0, j)),
            pl.BlockSpec((d, TF_FFN), lambda i, j: (0, j)),
            pl.BlockSpec((TF_FFN, d), lambda i, j: (j, 0)),
            pl.BlockSpec((1, d), fixed),
            pl.BlockSpec((1, 1, d), bmap),
        ],
        out_specs=pl.BlockSpec((TM_FFN, d), row),
        out_shape=jax.ShapeDtypeStruct((t, d), F32),
        scratch_shapes=[pltpu.VMEM((TM_FFN, d), MXU_DTYPE), pltpu.VMEM((TM_FFN, d), F32)],
        compiler_params=_cparams("parallel", "arbitrary"),
        name="dense_ffn",
    )(x2, g_pre, shift, scale, w_gate, w_up, w_down, g_post, gate)


def _router_kernel(x_ref, g_ref, sh_ref, sc_ref, whi_ref, wlo_ref, br_ref, gw_ref, gi_ref):
    h = _rms_norm(x_ref[...], g_ref[...]) * (1.0 + sc_ref[0]) + sh_ref[0]
    h_hi, h_lo = _split_hi_lo(h)
    logits = (_dot(h_hi, whi_ref[...]) + _dot(h_lo, whi_ref[...]) + _dot(h_hi, wlo_ref[...])
              + br_ref[...])
    lane = lax.broadcasted_iota(jnp.int32, logits.shape, 1)
    logits = jnp.where(lane < N_EXPERTS, logits, NEG_INF)
    m1 = jnp.max(logits, axis=-1, keepdims=True)
    i1 = jnp.min(jnp.where(logits == m1, lane, LANES), axis=-1, keepdims=True)
    rest = jnp.where(lane == i1, NEG_INF, logits)
    m2 = jnp.max(rest, axis=-1, keepdims=True)
    i2 = jnp.min(jnp.where(rest == m2, lane, LANES), axis=-1, keepdims=True)
    e2 = jnp.exp(m2 - m1)
    p1 = 1.0 / (1.0 + e2)
    p2 = e2 * p1
    gw_ref[...] = jnp.where(lane == 0, p1, jnp.where(lane == 1, p2, 0.0))
    gi_ref[...] = jnp.where(lane == 0, i1, jnp.where(lane == 1, i2, 0))


def _router(x2, g_pre, shift, scale, w_router, b_router, seq):
    t, d = x2.shape
    tm = TM_OUT
    w_pad = jnp.pad(w_router, ((0, 0), (0, LANES - N_EXPERTS)))
    w_hi = w_pad.astype(MXU_DTYPE)
    w_lo = (w_pad - w_hi.astype(F32)).astype(MXU_DTYPE)
    b_pad = jnp.pad(b_router, (0, LANES - N_EXPERTS)).reshape(1, LANES)
    per_b = seq // tm
    row = lambda i: (i, 0)
    fixed = lambda i: (0, 0)
    bmap = lambda i: (i // per_b, 0, 0)
    return pl.pallas_call(
        _router_kernel,
        grid=(t // tm,),
        in_specs=[
            pl.BlockSpec((tm, d), row),
            pl.BlockSpec((1, d), fixed),
            pl.BlockSpec((1, 1, d), bmap),
            pl.BlockSpec((1, 1, d), bmap),
            pl.BlockSpec((d, LANES), fixed),
            pl.BlockSpec((d, LANES), fixed),
            pl.BlockSpec((1, LANES), fixed),
        ],
        out_specs=[pl.BlockSpec((tm, LANES), row), pl.BlockSpec((tm, LANES), row)],
        out_shape=[jax.ShapeDtypeStruct((t, LANES), F32), jax.ShapeDtypeStruct((t, LANES), jnp.int32)],
        compiler_params=_cparams("parallel"),
        name="router",
    )(x2, g_pre, shift, scale, w_hi, w_lo, b_pad)


def _routing_tables(top_idx, n_tiles):
    flat_e = top_idx.T.reshape(-1)
    experts = jnp.arange(N_EXPERTS, dtype=jnp.int32)
    onehot = (flat_e[:, None] == experts[None, :]).astype(jnp.int32)
    csum = jnp.cumsum(onehot, axis=0)
    rank = jnp.sum(onehot * csum, axis=1) - 1
    counts = csum[-1]
    padded = ((counts + TM_MOE - 1) // TM_MOE) * TM_MOE
    ends = jnp.cumsum(padded)
    starts = ends - padded
    pos = (jnp.sum(onehot * starts[None, :], axis=1) + rank).astype(jnp.int32)
    tile_start = jnp.arange(n_tiles, dtype=jnp.int32) * TM_MOE
    tile_expert = jnp.minimum(jnp.sum((tile_start[:, None] >= ends[None, :]).astype(jnp.int32), axis=1),
                              N_EXPERTS - 1).astype(jnp.int32)
    n_active = (ends[-1] // TM_MOE).astype(jnp.int32).reshape(1)
    pad_start = (starts + counts).astype(jnp.int32)
    pad_len = (padded - counts).astype(jnp.int32)
    return pos, tile_expert, n_active, pad_start, pad_len


def _row_copy(src, src_row, dst, dst_row, sem):
    return pltpu.make_async_copy(src.at[pl.ds(src_row, 1)], dst.at[pl.ds(dst_row, 1)], sem)


def _for_row_groups(n_rows, body):
    def trip(g, c):
        r0 = pl.multiple_of(g * DMA_UNROLL, DMA_UNROLL)
        for k in range(DMA_UNROLL):
            body(r0 + k)
        return c

    lax.fori_loop(0, n_rows // DMA_UNROLL, trip, 0)


def _dispatch_kernel(pos_ref, pstart_ref, plen_ref, nact_ref, x_ref, g_ref, sh_ref, sc_ref, hs_hbm,
                     hbuf, sem, zsem):
    i = pl.program_id(0)
    n = pl.num_programs(0)
    tm = x_ref.shape[0]
    t_total = n * tm
    slot = i % 2

    def wait_slot(s):
        for _ in range(TOP_K):
            pltpu.make_async_copy(hbuf.at[s], hs_hbm.at[pl.ds(0, tm)], sem.at[s]).wait()

    @pl.when(i >= 2)
    def _():
        wait_slot(slot)

    hbuf[slot] = _rms_norm(x_ref[...], g_ref[...]) * (1.0 + sc_ref[0]) + sh_ref[0]

    def send(r):
        for s in range(TOP_K):
            _row_copy(hbuf.at[slot], r, hs_hbm, pos_ref[s * t_total + i * tm + r], sem.at[slot]).start()

    _for_row_groups(tm, send)

    @pl.when(i == n - 1)
    def _():
        wait_slot(slot)

        @pl.when(n >= 2)
        def _():
            wait_slot(1 - slot)

        zeros = hbuf.at[0]
        zeros[...] = jnp.zeros(zeros.shape, F32)
        for e in range(N_EXPERTS):
            def zero_row(k, c):
                _row_copy(zeros, 0, hs_hbm, pstart_ref[e] + k, zsem).start()
                return c

            def zero_row_wait(k, c):
                _row_copy(zeros, 0, hs_hbm, 0, zsem).wait()
                return c

            lax.fori_loop(0, plen_ref[e], zero_row, 0)
            lax.fori_loop(0, plen_ref[e], zero_row_wait, 0)

        tail0 = nact_ref[0] * TM_MOE
        n_tail = (hs_hbm.shape[0] - tail0) // tm

        def zero_block(k, c):
            pltpu.make_async_copy(zeros, hs_hbm.at[pl.ds(tail0 + k * tm, tm)], zsem).start()
            return c

        def zero_block_wait(k, c):
            pltpu.make_async_copy(zeros, hs_hbm.at[pl.ds(0, tm)], zsem).wait()
            return c

        lax.fori_loop(0, n_tail, zero_block, 0)
        lax.fori_loop(0, n_tail, zero_block_wait, 0)


def _dispatch(x2, g_pre, shift, scale, pos, pad_start, pad_len, n_active, n_tiles, seq):
    t, d = x2.shape
    tm = TM_DSP
    assert TM_MOE % tm == 0
    per_b = seq // tm
    return pl.pallas_call(
        _dispatch_kernel,
        grid_spec=pltpu.PrefetchScalarGridSpec(
            num_scalar_prefetch=4,
            grid=(t // tm,),
            in_specs=[
                pl.BlockSpec((tm, d), lambda i, *_: (i, 0)),
                pl.BlockSpec((1, d), lambda i, *_: (0, 0)),
                pl.BlockSpec((1, 1, d), lambda i, *_: (i // per_b, 0, 0)),
                pl.BlockSpec((1, 1, d), lambda i, *_: (i // per_b, 0, 0)),
            ],
            out_specs=pl.BlockSpec(memory_space=pl.ANY),
            scratch_shapes=[pltpu.VMEM((2, tm, d), F32),
                            pltpu.SemaphoreType.DMA((2,)), pltpu.SemaphoreType.DMA(())],
        ),
        out_shape=jax.ShapeDtypeStruct((n_tiles * TM_MOE, d), F32),
        compiler_params=_cparams("arbitrary"),
        name="expert_dispatch",
    )(pos, pad_start, pad_len, n_active, x2, g_pre, shift, scale)


def _expert_kernel(te_ref, nact_ref, h_ref, wg_ref, wu_ref, wd_ref, o_ref, hb_scr, acc_scr):
    i, j = pl.program_id(0), pl.program_id(1)
    active = i < nact_ref[0]

    @pl.when(j == 0)
    def _():
        acc_scr[...] = jnp.zeros(acc_scr.shape, F32)

    @pl.when(active & (j == 0))
    def _():
        hb_scr[...] = _mxu(h_ref[...])

    @pl.when(active)
    def _():
        acc_scr[...] += _swiglu_tile(hb_scr[...], wg_ref[0], wu_ref[0], wd_ref[0])

    @pl.when(j == pl.num_programs(1) - 1)
    def _():
        o_ref[...] = acc_scr[...]


def _expert_ffn(hs, tile_expert, n_active, w_gate, w_up, w_down):
    n_rows, d = hs.shape
    d_ff = w_gate.shape[2]
    n_tiles = n_rows // TM_MOE
    n_ff = d_ff // TF_FFN

    def ff_step(i, j, nact):
        return jnp.where(i < nact[0], j, n_ff - 1)

    def row_tile(i, nact):
        return jnp.minimum(i, nact[0] - 1)

    return pl.pallas_call(
        _expert_kernel,
        grid_spec=pltpu.PrefetchScalarGridSpec(
            num_scalar_prefetch=2,
            grid=(n_tiles, n_ff),
            in_specs=[
                pl.BlockSpec((TM_MOE, d), lambda i, j, te, nact: (row_tile(i, nact), 0)),
                pl.BlockSpec((1, d, TF_FFN), lambda i, j, te, nact: (te[i], 0, ff_step(i, j, nact))),
                pl.BlockSpec((1, d, TF_FFN), lambda i, j, te, nact: (te[i], 0, ff_step(i, j, nact))),
                pl.BlockSpec((1, TF_FFN, d), lambda i, j, te, nact: (te[i], ff_step(i, j, nact), 0)),
            ],
            out_specs=pl.BlockSpec((TM_MOE, d), lambda i, j, te, nact: (i, 0)),
            scratch_shapes=[pltpu.VMEM((TM_MOE, d), MXU_DTYPE), pltpu.VMEM((TM_MOE, d), F32)],
        ),
        out_shape=jax.ShapeDtypeStruct((n_rows, d), F32),
        compiler_params=_cparams("arbitrary", "arbitrary"),
        name="expert_ffn",
    )(tile_expert, n_active, hs, w_gate, w_up, w_down)


def _combine_kernel(pos_ref, ys_hbm, gw_ref, x_ref, g_ref, gate_ref, o_ref, buf, sem):
    i = pl.program_id(0)
    n = pl.num_programs(0)
    tm = x_ref.shape[0]
    t_total = n * tm
    cur = i % 2

    def fetch(step, b):
        def one(r):
            for s in range(TOP_K):
                _row_copy(ys_hbm, pos_ref[s * t_total + step * tm + r], buf.at[b, s], r, sem.at[b]).start()

        _for_row_groups(tm, one)

    @pl.when(i == 0)
    def _():
        fetch(0, 0)

    @pl.when(i + 1 < n)
    def _():
        fetch(i + 1, 1 - cur)

    for s in range(TOP_K):
        pltpu.make_async_copy(ys_hbm.at[pl.ds(0, tm)], buf.at[cur, s], sem.at[cur]).wait()
    gw = gw_ref[...]
    y = gw[:, 0:1] * buf[cur, 0]
    for s in range(1, TOP_K):
        y = y + gw[:, s:s + 1] * buf[cur, s]
    o_ref[...] = x_ref[...] + gate_ref[0] * _rms_norm(y, g_ref[...])


def _combine(ys, pos, gate_w, x2, g_post, gate, seq):
    t, d = x2.shape
    tm = TM_CMB
    per_b = seq // tm
    return pl.pallas_call(
        _combine_kernel,
        grid_spec=pltpu.PrefetchScalarGridSpec(
            num_scalar_prefetch=1,
            grid=(t // tm,),
            in_specs=[
                pl.BlockSpec(memory_space=pl.ANY),
                pl.BlockSpec((tm, LANES), lambda i, pos: (i, 0)),
                pl.BlockSpec((tm, d), lambda i, pos: (i, 0)),
                pl.BlockSpec((1, d), lambda i, pos: (0, 0)),
                pl.BlockSpec((1, 1, d), lambda i, pos: (i // per_b, 0, 0)),
            ],
            out_specs=pl.BlockSpec((tm, d), lambda i, pos: (i, 0)),
            scratch_shapes=[pltpu.VMEM((2, TOP_K, tm, d), F32), pltpu.SemaphoreType.DMA((2,))],
        ),
        out_shape=jax.ShapeDtypeStruct((t, d), F32),
        compiler_params=_cparams("arbitrary"),
        name="expert_combine",
    )(pos, ys, gate_w, x2, g_post, gate)


def _moe_ffn(x2, g_pre, shift, scale, w_router, b_router, w_gate, w_up, w_down, g_post, gate, seq):
    t = x2.shape[0]
    n_tiles = (TOP_K * t) // TM_MOE + N_EXPERTS
    gate_w, top = _router(x2, g_pre, shift, scale, w_router, b_router, seq)
    pos, tile_expert, n_active, pad_start, pad_len = _routing_tables(top[:, :TOP_K], n_tiles)
    hs = _dispatch(x2, g_pre, shift, scale, pos, pad_start, pad_len, n_active, n_tiles, seq)
    ys = _expert_ffn(hs, tile_expert, n_active, w_gate, w_up, w_down)
    return _combine(ys, pos, gate_w, x2, g_post, gate, seq)


def kernel(x, c, positions, ln_mix_pre, ln_mix_post, ln_ffn_pre, ln_ffn_post, w_mod, b_mod, w_in, b_fgate, w_out, w_ffn_gate, w_ffn_up, w_ffn_down, w_router, b_router, w_exp_gate, w_exp_up, w_exp_down):
    bsz, seq, d = x.shape
    depth = w_mod.shape[0]
    n_qkv = 3 * N_HEADS * HEAD_DIM
    x2 = x.reshape(bsz * seq, d)
    mod = _modulation(c, w_mod, b_mod).reshape(depth, bsz, N_MOD, 1, d)
    cos, sin = _rope_tables(positions)
    for layer in range(depth):
        sh_m, sc_m, g_m, sh_f, sc_f, g_f = (mod[layer, :, k] for k in range(N_MOD))
        row = lambda a: a[layer].reshape(1, d)
        w_f = _mxu(jnp.pad(w_in[layer, :, n_qkv:], ((0, 0), (0, LANES - N_HEADS_FOX))))
        qkv, fg = _in_proj(x2, row(ln_mix_pre), sh_m, sc_m, w_in, layer, n_qkv, w_f, seq)
        qkv3 = qkv.reshape(bsz, seq, n_qkv)
        cum_f = _forget_cumsum(fg.reshape(bsz, seq, LANES), b_fgate[layer])
        o_dil = _dilated_attention(qkv3, cos, sin).reshape(bsz * seq, -1)
        o_moba = _moba_attention(qkv3, cos, sin).reshape(bsz * seq, -1)
        o_fox = _fox_attention(qkv3, cum_f).reshape(bsz * seq, -1)
        x2 = _out_proj(o_dil, o_moba, o_fox, _mxu(w_out[layer]), x2, row(ln_mix_post), g_m, seq)
        j = layer // 2
        if layer % 2 == 0:
            x2 = _dense_ffn(x2, row(ln_ffn_pre), sh_f, sc_f, _mxu(w_ffn_gate[j]), _mxu(w_ffn_up[j]),
                            _mxu(w_ffn_down[j]), row(ln_ffn_post), g_f, seq)
        else:
            x2 = _moe_ffn(x2, row(ln_ffn_pre), sh_f, sc_f, w_router[j], b_router[j],
                          _mxu(w_exp_gate[j]), _mxu(w_exp_up[j]), _mxu(w_exp_down[j]),
                          row(ln_ffn_post), g_f, seq)
    return x2.reshape(bsz, seq, d)
```

```python
import functools

import numpy as np
import jax
import jax.numpy as jnp
from jax import lax
from jax.experimental import pallas as pl
from jax.experimental.pallas import tpu as pltpu

F32 = jnp.float32
MXU_DTYPE = jnp.bfloat16

HEAD_DIM = 128
N_HEADS = 16
N_HEADS_DIL = 6
N_HEADS_MOBA = 4
N_HEADS_FOX = 6
DILATED_BRANCHES = ((128, 1), (512, 4), (2048, 16))
MOBA_BLOCK = 256
MOBA_TOPK = 3
ROPE_THETA = 500000.0
ROT_DIM = HEAD_DIM // 4
ROT_HALF = ROT_DIM // 2
N_EXPERTS = 8
TOP_K = 2
N_MOD = 6
RMS_EPS = 1e-6
NEG_INF = -1e30
ATTN_SCALE = HEAD_DIM ** -0.5
LANES = 128

VMEM_LIMIT_BYTES = 56 * 1024 * 1024

TM_IN = 1024
TN_IN = 1536
TM_OUT = 512
TM_FFN = 512
TF_FFN = 1024
TM_DSP = 256
DMA_UNROLL = 8
TM_MOE = 512
TM_CMB = 256
TQ_ATT = 512
TN_MOD = 1024


def _cparams(*sem):
    return pltpu.CompilerParams(dimension_semantics=sem, vmem_limit_bytes=VMEM_LIMIT_BYTES)


def _mxu(a):
    return a.astype(MXU_DTYPE)


def _dot(a, b):
    return jnp.dot(a, b, preferred_element_type=F32)


def _dot_nt(a, b):
    return lax.dot_general(a, b, (((1,), (1,)), ((), ())), preferred_element_type=F32)


def _split_hi_lo(a):
    hi = a.astype(MXU_DTYPE)
    lo = (a - hi.astype(F32)).astype(MXU_DTYPE)
    return hi, lo


def _rms_norm(x, g):
    ms = jnp.mean(x * x, axis=-1, keepdims=True)
    return x * lax.rsqrt(ms + RMS_EPS) * g


def _sigmoid(x):
    return 1.0 / (1.0 + jnp.exp(-x))


def _mod_kernel(c_ref, w_ref, b_ref, o_ref):
    c = c_ref[...]
    cond = c * _sigmoid(c)
    o_ref[0] = _dot(_mxu(cond), _mxu(w_ref[0])) + b_ref[0]


def _modulation(c, w_mod, b_mod):
    depth, d, n = w_mod.shape
    bsz = c.shape[0]
    rows = max(16, bsz)
    c_pad = jnp.pad(c, ((0, rows - bsz), (0, 0)))
    out = pl.pallas_call(
        _mod_kernel,
        grid=(depth, n // TN_MOD),
        in_specs=[
            pl.BlockSpec((rows, d), lambda l, j: (0, 0)),
            pl.BlockSpec((1, d, TN_MOD), lambda l, j: (l, 0, j)),
            pl.BlockSpec((1, 1, TN_MOD), lambda l, j: (l, 0, j)),
        ],
        out_specs=pl.BlockSpec((1, rows, TN_MOD), lambda l, j: (l, 0, j)),
        out_shape=jax.ShapeDtypeStruct((depth, rows, n), F32),
        compiler_params=_cparams("parallel", "parallel"),
        name="modulation",
    )(c_pad, w_mod, b_mod.reshape(depth, 1, n))
    return out[:, :bsz]


def _rope_table_kernel(pos_ref, freq_ref, cos_ref, sin_ref):
    ang = pos_ref[0].astype(F32) * freq_ref[...]
    lane = lax.broadcasted_iota(jnp.int32, ang.shape, 1)
    cos_ref[0] = jnp.where(lane < ROT_DIM, jnp.cos(ang), 1.0)
    sn = jnp.sin(ang)
    sin_ref[0] = jnp.where(lane < ROT_HALF, -sn, jnp.where(lane < ROT_DIM, sn, 0.0))


def _rope_tables(positions):
    bsz, seq = positions.shape
    inv_freq = ROPE_THETA ** (-np.arange(ROT_HALF, dtype=np.float32) / ROT_HALF)
    freq = np.zeros((1, LANES), np.float32)
    freq[0, :ROT_HALF] = inv_freq
    freq[0, ROT_HALF:ROT_DIM] = inv_freq
    tab = jax.ShapeDtypeStruct((bsz, seq, LANES), F32)
    return pl.pallas_call(
        _rope_table_kernel,
        grid=(bsz,),
        in_specs=[
            pl.BlockSpec((1, seq, 1), lambda b: (b, 0, 0)),
            pl.BlockSpec((1, LANES), lambda b: (0, 0)),
        ],
        out_specs=[pl.BlockSpec((1, seq, LANES), lambda b: (b, 0, 0))] * 2,
        out_shape=[tab, tab],
        compiler_params=_cparams("parallel"),
        name="rope_tables",
    )(positions.reshape(bsz, seq, 1), jnp.asarray(freq))


def _rope(t, cos, sin_signed):
    lane = lax.broadcasted_iota(jnp.int32, t.shape, 1)
    partner = jnp.where(lane < ROT_HALF,
                        pltpu.roll(t, LANES - ROT_HALF, 1),
                        pltpu.roll(t, ROT_HALF, 1))
    return t * cos + partner * sin_signed


def _in_proj_kernel(x_ref, g_ref, sh_ref, sc_ref, w_ref, wf_ref, qkv_ref, fg_ref, h_scr):
    @pl.when(pl.program_id(1) == 0)
    def _():
        h = _rms_norm(x_ref[...], g_ref[...]) * (1.0 + sc_ref[0]) + sh_ref[0]
        hb = _mxu(h)
        h_scr[...] = hb
        fg_ref[...] = _dot(hb, wf_ref[...])

    qkv_ref[...] = _dot(h_scr[...], w_ref[...]).astype(qkv_ref.dtype)


def _in_proj(x2, g, shift, scale, w_qkv, w_f, seq):
    t, d = x2.shape
    n = w_qkv.shape[1]
    per_b = seq // TM_IN
    bmap = lambda i, j: (i // per_b, 0, 0)
    return pl.pallas_call(
        _in_proj_kernel,
        grid=(t // TM_IN, n // TN_IN),
        in_specs=[
            pl.BlockSpec((TM_IN, d), lambda i, j: (i, 0)),
            pl.BlockSpec((1, d), lambda i, j: (0, 0)),
            pl.BlockSpec((1, 1, d), bmap),
            pl.BlockSpec((1, 1, d), bmap),
            pl.BlockSpec((d, TN_IN), lambda i, j: (0, j)),
            pl.BlockSpec((d, LANES), lambda i, j: (0, 0)),
        ],
        out_specs=[
            pl.BlockSpec((TM_IN, TN_IN), lambda i, j: (i, j)),
            pl.BlockSpec((TM_IN, LANES), lambda i, j: (i, 0)),
        ],
        out_shape=[
            jax.ShapeDtypeStruct((t, n), MXU_DTYPE),
            jax.ShapeDtypeStruct((t, LANES), F32),
        ],
        scratch_shapes=[pltpu.VMEM((TM_IN, d), MXU_DTYPE)],
        compiler_params=_cparams("parallel", "arbitrary"),
        name="in_proj",
    )(x2, g, shift, scale, w_qkv, w_f)


def _forget_kernel(fg_ref, b_ref, f_ref):
    z = fg_ref[0] + b_ref[...]
    log_f = jnp.minimum(z, 0.0) - jnp.log(1.0 + jnp.exp(-jnp.abs(z)))
    cum = log_f.T[0:8]
    seq = cum.shape[1]
    lane = lax.broadcasted_iota(jnp.int32, cum.shape, 1)
    shift = 1
    while shift < seq:
        cum = cum + jnp.where(lane >= shift, pltpu.roll(cum, shift, 1), 0.0)
        shift *= 2
    f_ref[0] = cum


def _forget_cumsum(fg3, b_fgate):
    bsz, seq, _ = fg3.shape
    b_pad = jnp.pad(b_fgate, (0, LANES - b_fgate.shape[0])).reshape(1, LANES)
    return pl.pallas_call(
        _forget_kernel,
        grid=(bsz,),
        in_specs=[
            pl.BlockSpec((1, seq, LANES), lambda b: (b, 0, 0)),
            pl.BlockSpec((1, LANES), lambda b: (0, 0)),
        ],
        out_specs=pl.BlockSpec((1, 8, seq), lambda b: (b, 0, 0)),
        out_shape=jax.ShapeDtypeStruct((bsz, 8, seq), F32),
        compiler_params=_cparams("parallel"),
        name="forget_cumsum",
    )(fg3, b_pad)


LOG2E = 1.4426950408889634
Q_SCALE = ATTN_SCALE * LOG2E


def _softmax_pv(pieces, v):
    m = pieces[0]
    for s in pieces[1:]:
        m = jnp.maximum(m, s)
    m = jnp.max(m, axis=-1, keepdims=True)
    ps = [jnp.exp2(s - m) for s in pieces]
    l = ps[0]
    for p in ps[1:]:
        l = l + p
    l = jnp.sum(l, axis=-1, keepdims=True)
    p_all = _mxu(ps[0]) if len(ps) == 1 else jnp.concatenate([_mxu(p) for p in ps], axis=1)
    return _dot(p_all, v) / l


def _causal_mask(s):
    row = lax.broadcasted_iota(jnp.int32, s.shape, 0)
    col = lax.broadcasted_iota(jnp.int32, s.shape, 1)
    return jnp.where(col <= row, s, NEG_INF)


def _qkv_specs(seq, head0):
    return [
        pl.BlockSpec((1, seq, HEAD_DIM), lambda b, h: (b, 0, head0 + h)),
        pl.BlockSpec((1, seq, HEAD_DIM), lambda b, h: (b, 0, N_HEADS + head0 + h)),
        pl.BlockSpec((1, seq, HEAD_DIM), lambda b, h: (b, 0, 2 * N_HEADS + head0 + h)),
    ]


def _rope_k_into(k_ref, cos_ref, sin_ref, k_scr, blk):
    seq = k_scr.shape[0]
    for j in range(seq // blk):
        rows = slice(j * blk, (j + 1) * blk)
        k_scr[rows] = _mxu(_rope(k_ref[0, rows].astype(F32), cos_ref[0, rows], sin_ref[0, rows]))


def _dilated_bias(tq):
    n_off = 2048 // tq
    off = jnp.arange(n_off, dtype=jnp.int32)[:, None, None] * tq
    delta = off + jnp.arange(tq, dtype=jnp.int32)[None, :, None] - jnp.arange(tq, dtype=jnp.int32)[None, None, :]
    count = jnp.zeros(delta.shape, F32)
    for window, dil in DILATED_BRANCHES:
        count = count + ((delta >= 0) & (delta % dil == 0) & (delta <= window)).astype(F32)
    return jnp.where(count > 0, jnp.log2(jnp.maximum(count, 1.0)), NEG_INF)


def _dilated_kernel(q_ref, k_ref, v_ref, cos_ref, sin_ref, bias_ref, o_ref, k_scr):
    tq = TQ_ATT
    seq = k_scr.shape[0]
    _rope_k_into(k_ref, cos_ref, sin_ref, k_scr, tq)

    def logits(qi):
        rows = slice(qi * tq, (qi + 1) * tq)
        q = _rope(q_ref[0, rows].astype(F32), cos_ref[0, rows], sin_ref[0, rows])
        return _dot_nt(_mxu(q * Q_SCALE), k_scr[:(qi + 1) * tq])

    n_q = seq // tq
    s_next = logits(0)
    for qi in range(n_q):
        s = s_next
        if qi + 1 < n_q:
            s_next = logits(qi + 1)
        pieces = [s[:, kj * tq:(kj + 1) * tq] + bias_ref[qi - kj] for kj in range(qi + 1)]
        o = _softmax_pv(pieces, v_ref[0, :(qi + 1) * tq])
        o_ref[0, qi * tq:(qi + 1) * tq] = o.astype(o_ref.dtype)


def _dilated_attention(qkv3, cos, sin):
    bsz, seq, _ = qkv3.shape
    assert seq == 2048, "dilated windows are laid out for a 2048-token sequence"
    bias = _dilated_bias(TQ_ATT)
    tab = pl.BlockSpec((1, seq, LANES), lambda b, h: (b, 0, 0))
    return pl.pallas_call(
        _dilated_kernel,
        grid=(bsz, N_HEADS_DIL),
        in_specs=_qkv_specs(seq, 0) + [tab, tab,
                                       pl.BlockSpec(bias.shape, lambda b, h: (0, 0, 0))],
        out_specs=pl.BlockSpec((1, seq, HEAD_DIM), lambda b, h: (b, 0, h)),
        out_shape=jax.ShapeDtypeStruct((bsz, seq, N_HEADS_DIL * HEAD_DIM), MXU_DTYPE),
        scratch_shapes=[pltpu.VMEM((seq, HEAD_DIM), MXU_DTYPE)],
        compiler_params=_cparams("parallel", "parallel"),
        name="dilated_attention",
    )(qkv3, qkv3, qkv3, cos, sin, bias)


def _moba_kernel(q_ref, k_ref, v_ref, cos_ref, sin_ref, o_ref, k_scr, km_scr):
    blk = MOBA_BLOCK
    seq = k_scr.shape[0]
    n_blk = seq // blk
    km_scr[...] = jnp.zeros(km_scr.shape, F32)
    for j in range(n_blk):
        rows = slice(j * blk, (j + 1) * blk)
        kr = _rope(k_ref[0, rows].astype(F32), cos_ref[0, rows], sin_ref[0, rows])
        k_scr[rows] = _mxu(kr)
        km_scr[j:j + 1, :] = jnp.mean(kr, axis=0, keepdims=True)
    km_hi, km_lo = _split_hi_lo(km_scr[...])

    def logits(qi):
        rows = slice(qi * blk, (qi + 1) * blk)
        qf = _rope(q_ref[0, rows].astype(F32), cos_ref[0, rows], sin_ref[0, rows])
        s = _dot_nt(_mxu(qf * Q_SCALE), k_scr[:(qi + 1) * blk])
        if qi == 0:
            return s, None
        q_hi, q_lo = _split_hi_lo(qf)
        gate = _dot_nt(q_hi, km_hi) + _dot_nt(q_lo, km_hi) + _dot_nt(q_hi, km_lo)
        lane = lax.broadcasted_iota(jnp.int32, gate.shape, 1)
        past = lane < qi
        gate = jnp.where(past, gate, NEG_INF)
        rank = jnp.zeros(gate.shape, F32)
        for jp in range(qi):
            g_jp = gate[:, jp:jp + 1]
            ahead = (g_jp > gate) | ((g_jp == gate) & (lane > jp))
            rank = rank + jnp.where(ahead, 1.0, 0.0)
        return s, jnp.where(past & (rank < MOBA_TOPK), 0.0, NEG_INF)

    nxt = logits(0)
    for qi in range(n_blk):
        s, sel_bias = nxt
        if qi + 1 < n_blk:
            nxt = logits(qi + 1)
        pieces = [s[:, kj * blk:(kj + 1) * blk] + sel_bias[:, kj:kj + 1] for kj in range(qi)]
        pieces.append(_causal_mask(s[:, qi * blk:(qi + 1) * blk]))
        o = _softmax_pv(pieces, v_ref[0, :(qi + 1) * blk])
        o_ref[0, qi * blk:(qi + 1) * blk] = o.astype(o_ref.dtype)


def _moba_attention(qkv3, cos, sin):
    bsz, seq, _ = qkv3.shape
    assert seq % MOBA_BLOCK == 0 and seq // MOBA_BLOCK <= LANES
    tab = pl.BlockSpec((1, seq, LANES), lambda b, h: (b, 0, 0))
    return pl.pallas_call(
        _moba_kernel,
        grid=(bsz, N_HEADS_MOBA),
        in_specs=_qkv_specs(seq, N_HEADS_DIL) + [tab, tab],
        out_specs=pl.BlockSpec((1, seq, HEAD_DIM), lambda b, h: (b, 0, h)),
        out_shape=jax.ShapeDtypeStruct((bsz, seq, N_HEADS_MOBA * HEAD_DIM), MXU_DTYPE),
        scratch_shapes=[pltpu.VMEM((seq, HEAD_DIM), MXU_DTYPE),
                        pltpu.VMEM((LANES, HEAD_DIM), F32)],
        compiler_params=_cparams("parallel", "parallel"),
        name="moba_attention",
    )(qkv3, qkv3, qkv3, cos, sin)


def _fox_kernel(q_ref, k_ref, v_ref, f_ref, o_ref):
    tq = TQ_ATT
    seq = q_ref.shape[1]
    def logits(qi):
        q = _mxu(q_ref[0, qi * tq:(qi + 1) * tq].astype(F32) * Q_SCALE)
        return _dot_nt(q, k_ref[0, :(qi + 1) * tq])

    n_q = seq // tq
    s_next = logits(0)
    for qi in range(n_q):
        past = slice(0, (qi + 1) * tq)
        s = s_next - f_ref[0, 0, :, past] * LOG2E
        if qi + 1 < n_q:
            s_next = logits(qi + 1)
        pieces = [s[:, kj * tq:(kj + 1) * tq] for kj in range(qi)]
        pieces.append(_causal_mask(s[:, qi * tq:(qi + 1) * tq]))
        o_ref[0, qi * tq:(qi + 1) * tq] = _softmax_pv(pieces, v_ref[0, past]).astype(o_ref.dtype)


def _fox_attention(qkv3, cum_f):
    bsz, seq, _ = qkv3.shape
    f4 = cum_f.reshape(bsz, cum_f.shape[1], 1, seq)
    return pl.pallas_call(
        _fox_kernel,
        grid=(bsz, N_HEADS_FOX),
        in_specs=_qkv_specs(seq, N_HEADS_DIL + N_HEADS_MOBA)
        + [pl.BlockSpec((1, 1, 1, seq), lambda b, h: (b, h, 0, 0))],
        out_specs=pl.BlockSpec((1, seq, HEAD_DIM), lambda b, h: (b, 0, h)),
        out_shape=jax.ShapeDtypeStruct((bsz, seq, N_HEADS_FOX * HEAD_DIM), MXU_DTYPE),
        compiler_params=_cparams("parallel", "parallel"),
        name="fox_attention",
    )(qkv3, qkv3, qkv3, f4)


def _out_proj_kernel(od_ref, om_ref, of_ref, wd_ref, wm_ref, wf_ref, x_ref, g_ref, gate_ref, o_ref):
    y = _dot(od_ref[...], wd_ref[...]) + _dot(om_ref[...], wm_ref[...]) + _dot(of_ref[...], wf_ref[...])
    o_ref[...] = x_ref[...] + gate_ref[0] * _rms_norm(y, g_ref[...])


def _out_proj(o_dil, o_moba, o_fox, w_out, x2, g, gate, seq):
    t, d = x2.shape
    nd, nm = o_dil.shape[1], o_moba.shape[1]
    w_d, w_m, w_f = w_out[:nd], w_out[nd:nd + nm], w_out[nd + nm:]
    per_b = seq // TM_OUT
    row = lambda i: (i, 0)
    fixed = lambda i: (0, 0)
    return pl.pallas_call(
        _out_proj_kernel,
        grid=(t // TM_OUT,),
        in_specs=[
            pl.BlockSpec((TM_OUT, nd), row),
            pl.BlockSpec((TM_OUT, nm), row),
            pl.BlockSpec((TM_OUT, o_fox.shape[1]), row),
            pl.BlockSpec(w_d.shape, fixed),
            pl.BlockSpec(w_m.shape, fixed),
            pl.BlockSpec(w_f.shape, fixed),
            pl.BlockSpec((TM_OUT, d), row),
            pl.BlockSpec((1, d), fixed),
            pl.BlockSpec((1, 1, d), lambda i: (i // per_b, 0, 0)),
        ],
        out_specs=pl.BlockSpec((TM_OUT, d), row),
        out_shape=jax.ShapeDtypeStruct((t, d), F32),
        compiler_params=_cparams("parallel"),
        name="out_proj",
    )(o_dil, o_moba, o_fox, w_d, w_m, w_f, x2, g, gate)


def _swiglu_tile(h, wg, wu, wd):
    g = _dot(h, wg)
    u = _dot(h, wu)
    return _dot(_mxu(g * _sigmoid(g) * u), wd)


def _ffn_kernel(x_ref, gpre_ref, sh_ref, sc_ref, wg_ref, wu_ref, wd_ref, gpost_ref, gate_ref,
                o_ref, h_scr, acc_scr):
    j = pl.program_id(1)

    @pl.when(j == 0)
    def _():
        h = _rms_norm(x_ref[...], gpre_ref[...]) * (1.0 + sc_ref[0]) + sh_ref[0]
        h_scr[...] = _mxu(h)
        acc_scr[...] = jnp.zeros(acc_scr.shape, F32)

    acc_scr[...] += _swiglu_tile(h_scr[...], wg_ref[...], wu_ref[...], wd_ref[...])

    @pl.when(j == pl.num_programs(1) - 1)
    def _():
        o_ref[...] = x_ref[...] + gate_ref[0] * _rms_norm(acc_scr[...], gpost_ref[...])


def _dense_ffn(x2, g_pre, shift, scale, w_gate, w_up, w_down, g_post, gate, seq):
    t, d = x2.shape
    d_ff = w_gate.shape[1]
    per_b = seq // TM_FFN
    row = lambda i, j: (i, 0)
    fixed = lambda i, j: (0, 0)
    bmap = lambda i, j: (i // per_b, 0, 0)
    return pl.pallas_call(
        _ffn_kernel,
        grid=(t // TM_FFN, d_ff // TF_FFN),
        in_specs=[
            pl.BlockSpec((TM_FFN, d), row),
            pl.BlockSpec((1, d), fixed),
            pl.BlockSpec((1, 1, d), bmap),
            pl.BlockSpec((1, 1, d), bmap),
            pl.BlockSpec((d, TF_FFN), lambda i, j: (0, j)),
            pl.BlockSpec((d, TF_FFN), lambda i, j: (0, j)),
            pl.BlockSpec((TF_FFN, d), lambda i, j: (j, 0)),
            pl.BlockSpec((1, d), fixed),
            pl.BlockSpec((1, 1, d), bmap),
        ],
        out_specs=pl.BlockSpec((TM_FFN, d), row),
        out_shape=jax.ShapeDtypeStruct((t, d), F32),
        scratch_shapes=[pltpu.VMEM((TM_FFN, d), MXU_DTYPE), pltpu.VMEM((TM_FFN, d), F32)],
        compiler_params=_cparams("parallel", "arbitrary"),
        name="dense_ffn",
    )(x2, g_pre, shift, scale, w_gate, w_up, w_down, g_post, gate)


def _router_kernel(x_ref, g_ref, sh_ref, sc_ref, whi_ref, wlo_ref, br_ref, gw_ref, gi_ref):
    h = _rms_norm(x_ref[...], g_ref[...]) * (1.0 + sc_ref[0]) + sh_ref[0]
    h_hi, h_lo = _split_hi_lo(h)
    logits = (_dot(h_hi, whi_ref[...]) + _dot(h_lo, whi_ref[...]) + _dot(h_hi, wlo_ref[...])
              + br_ref[...])
    lane = lax.broadcasted_iota(jnp.int32, logits.shape, 1)
    logits = jnp.where(lane < N_EXPERTS, logits, NEG_INF)
    m1 = jnp.max(logits, axis=-1, keepdims=True)
    i1 = jnp.min(jnp.where(logits == m1, lane, LANES), axis=-1, keepdims=True)
    rest = jnp.where(lane == i1, NEG_INF, logits)
    m2 = jnp.max(rest, axis=-1, keepdims=True)
    i2 = jnp.min(jnp.where(rest == m2, lane, LANES), axis=-1, keepdims=True)
    e2 = jnp.exp(m2 - m1)
    p1 = 1.0 / (1.0 + e2)
    p2 = e2 * p1
    gw_ref[...] = jnp.where(lane == 0, p1, jnp.where(lane == 1, p2, 0.0))
    gi_ref[...] = jnp.where(lane == 0, i1, jnp.where(lane == 1, i2, 0))


def _router(x2, g_pre, shift, scale, w_router, b_router, seq):
    t, d = x2.shape
    tm = TM_OUT
    w_pad = jnp.pad(w_router, ((0, 0), (0, LANES - N_EXPERTS)))
    w_hi = w_pad.astype(MXU_DTYPE)
    w_lo = (w_pad - w_hi.astype(F32)).astype(MXU_DTYPE)
    b_pad = jnp.pad(b_router, (0, LANES - N_EXPERTS)).reshape(1, LANES)
    per_b = seq // tm
    row = lambda i: (i, 0)
    fixed = lambda i: (0, 0)
    bmap = lambda i: (i // per_b, 0, 0)
    return pl.pallas_call(
        _router_kernel,
        grid=(t // tm,),
        in_specs=[
            pl.BlockSpec((tm, d), row),
            pl.BlockSpec((1, d), fixed),
            pl.BlockSpec((1, 1, d), bmap),
            pl.BlockSpec((1, 1, d), bmap),
            pl.BlockSpec((d, LANES), fixed),
            pl.BlockSpec((d, LANES), fixed),
            pl.BlockSpec((1, LANES), fixed),
        ],
        out_specs=[pl.BlockSpec((tm, LANES), row), pl.BlockSpec((tm, LANES), row)],
        out_shape=[jax.ShapeDtypeStruct((t, LANES), F32), jax.ShapeDtypeStruct((t, LANES), jnp.int32)],
        compiler_params=_cparams("parallel"),
        name="router",
    )(x2, g_pre, shift, scale, w_hi, w_lo, b_pad)


def _routing_tables(top_idx, n_tiles):
    flat_e = top_idx.T.reshape(-1)
    experts = jnp.arange(N_EXPERTS, dtype=jnp.int32)
    onehot = (flat_e[:, None] == experts[None, :]).astype(jnp.int32)
    csum = jnp.cumsum(onehot, axis=0)
    rank = jnp.sum(onehot * csum, axis=1) - 1
    counts = csum[-1]
    padded = ((counts + TM_MOE - 1) // TM_MOE) * TM_MOE
    ends = jnp.cumsum(padded)
    starts = ends - padded
    pos = (jnp.sum(onehot * starts[None, :], axis=1) + rank).astype(jnp.int32)
    tile_start = jnp.arange(n_tiles, dtype=jnp.int32) * TM_MOE
    tile_expert = jnp.minimum(jnp.sum((tile_start[:, None] >= ends[None, :]).astype(jnp.int32), axis=1),
                              N_EXPERTS - 1).astype(jnp.int32)
    n_active = (ends[-1] // TM_MOE).astype(jnp.int32).reshape(1)
    pad_start = (starts + counts).astype(jnp.int32)
    pad_len = (padded - counts).astype(jnp.int32)
    return pos, tile_expert, n_active, pad_start, pad_len


def _row_copy(src, src_row, dst, dst_row, sem):
    return pltpu.make_async_copy(src.at[pl.ds(src_row, 1)], dst.at[pl.ds(dst_row, 1)], sem)


def _for_row_groups(n_rows, body):
    def trip(g, c):
        r0 = pl.multiple_of(g * DMA_UNROLL, DMA_UNROLL)
        for k in range(DMA_UNROLL):
            body(r0 + k)
        return c

    lax.fori_loop(0, n_rows // DMA_UNROLL, trip, 0)


def _dispatch_kernel(pos_ref, pstart_ref, plen_ref, nact_ref, x_ref, g_ref, sh_ref, sc_ref, hs_hbm,
                     hbuf, sem, zsem):
    i = pl.program_id(0)
    n = pl.num_programs(0)
    tm = x_ref.shape[0]
    t_total = n * tm
    slot = i % 2

    def wait_slot(s):
        for _ in range(TOP_K):
            pltpu.make_async_copy(hbuf.at[s], hs_hbm.at[pl.ds(0, tm)], sem.at[s]).wait()

    @pl.when(i >= 2)
    def _():
        wait_slot(slot)

    hbuf[slot] = _rms_norm(x_ref[...], g_ref[...]) * (1.0 + sc_ref[0]) + sh_ref[0]

    def send(r):
        for s in range(TOP_K):
            _row_copy(hbuf.at[slot], r, hs_hbm, pos_ref[s * t_total + i * tm + r], sem.at[slot]).start()

    _for_row_groups(tm, send)

    @pl.when(i == n - 1)
    def _():
        wait_slot(slot)

        @pl.when(n >= 2)
        def _():
            wait_slot(1 - slot)

        zeros = hbuf.at[0]
        zeros[...] = jnp.zeros(zeros.shape, F32)
        for e in range(N_EXPERTS):
            def zero_row(k, c):
                _row_copy(zeros, 0, hs_hbm, pstart_ref[e] + k, zsem).start()
                return c

            def zero_row_wait(k, c):
                _row_copy(zeros, 0, hs_hbm, 0, zsem).wait()
                return c

            lax.fori_loop(0, plen_ref[e], zero_row, 0)
            lax.fori_loop(0, plen_ref[e], zero_row_wait, 0)

        tail0 = nact_ref[0] * TM_MOE
        n_tail = (hs_hbm.shape[0] - tail0) // tm

        def zero_block(k, c):
            pltpu.make_async_copy(zeros, hs_hbm.at[pl.ds(tail0 + k * tm, tm)], zsem).start()
            return c

        def zero_block_wait(k, c):
            pltpu.make_async_copy(zeros, hs_hbm.at[pl.ds(0, tm)], zsem).wait()
            return c

        lax.fori_loop(0, n_tail, zero_block, 0)
        lax.fori_loop(0, n_tail, zero_block_wait, 0)


def _dispatch(x2, g_pre, shift, scale, pos, pad_start, pad_len, n_active, n_tiles, seq):
    t, d = x2.shape
    tm = TM_DSP
    assert TM_MOE % tm == 0
    per_b = seq // tm
    return pl.pallas_call(
        _dispatch_kernel,
        grid_spec=pltpu.PrefetchScalarGridSpec(
            num_scalar_prefetch=4,
            grid=(t // tm,),
            in_specs=[
                pl.BlockSpec((tm, d), lambda i, *_: (i, 0)),
                pl.BlockSpec((1, d), lambda i, *_: (0, 0)),
                pl.BlockSpec((1, 1, d), lambda i, *_: (i // per_b, 0, 0)),
                pl.BlockSpec((1, 1, d), lambda i, *_: (i // per_b, 0, 0)),
            ],
            out_specs=pl.BlockSpec(memory_space=pl.ANY),
            scratch_shapes=[pltpu.VMEM((2, tm, d), F32),
                            pltpu.SemaphoreType.DMA((2,)), pltpu.SemaphoreType.DMA(())],
        ),
        out_shape=jax.ShapeDtypeStruct((n_tiles * TM_MOE, d), F32),
        compiler_params=_cparams("arbitrary"),
        name="expert_dispatch",
    )(pos, pad_start, pad_len, n_active, x2, g_pre, shift, scale)


def _expert_kernel(te_ref, nact_ref, h_ref, wg_ref, wu_ref, wd_ref, o_ref, hb_scr, acc_scr):
    i, j = pl.program_id(0), pl.program_id(1)
    active = i < nact_ref[0]

    @pl.when(j == 0)
    def _():
        acc_scr[...] = jnp.zeros(acc_scr.shape, F32)

    @pl.when(active & (j == 0))
    def _():
        hb_scr[...] = _mxu(h_ref[...])

    @pl.when(active)
    def _():
        acc_scr[...] += _swiglu_tile(hb_scr[...], wg_ref[0], wu_ref[0], wd_ref[0])

    @pl.when(j == pl.num_programs(1) - 1)
    def _():
        o_ref[...] = acc_scr[...]


def _expert_ffn(hs, tile_expert, n_active, w_gate, w_up, w_down):
    n_rows, d = hs.shape
    d_ff = w_gate.shape[2]
    n_tiles = n_rows // TM_MOE
    n_ff = d_ff // TF_FFN

    def ff_step(i, j, nact):
        return jnp.where(i < nact[0], j, n_ff - 1)

    def row_tile(i, nact):
        return jnp.minimum(i, nact[0] - 1)

    return pl.pallas_call(
        _expert_kernel,
        grid_spec=pltpu.PrefetchScalarGridSpec(
            num_scalar_prefetch=2,
            grid=(n_tiles, n_ff),
            in_specs=[
                pl.BlockSpec((TM_MOE, d), lambda i, j, te, nact: (row_tile(i, nact), 0)),
                pl.BlockSpec((1, d, TF_FFN), lambda i, j, te, nact: (te[i], 0, ff_step(i, j, nact))),
                pl.BlockSpec((1, d, TF_FFN), lambda i, j, te, nact: (te[i], 0, ff_step(i, j, nact))),
                pl.BlockSpec((1, TF_FFN, d), lambda i, j, te, nact: (te[i], ff_step(i, j, nact), 0)),
            ],
            out_specs=pl.BlockSpec((TM_MOE, d), lambda i, j, te, nact: (i, 0)),
            scratch_shapes=[pltpu.VMEM((TM_MOE, d), MXU_DTYPE), pltpu.VMEM((TM_MOE, d), F32)],
        ),
        out_shape=jax.ShapeDtypeStruct((n_rows, d), F32),
        compiler_params=_cparams("arbitrary", "arbitrary"),
        name="expert_ffn",
    )(tile_expert, n_active, hs, w_gate, w_up, w_down)


def _combine_kernel(pos_ref, ys_hbm, gw_ref, x_ref, g_ref, gate_ref, o_ref, buf, sem):
    i = pl.program_id(0)
    n = pl.num_programs(0)
    tm = x_ref.shape[0]
    t_total = n * tm
    cur = i % 2

    def fetch(step, b):
        def one(r):
            for s in range(TOP_K):
                _row_copy(ys_hbm, pos_ref[s * t_total + step * tm + r], buf.at[b, s], r, sem.at[b]).start()

        _for_row_groups(tm, one)

    @pl.when(i == 0)
    def _():
        fetch(0, 0)

    @pl.when(i + 1 < n)
    def _():
        fetch(i + 1, 1 - cur)

    for s in range(TOP_K):
        pltpu.make_async_copy(ys_hbm.at[pl.ds(0, tm)], buf.at[cur, s], sem.at[cur]).wait()
    gw = gw_ref[...]
    y = gw[:, 0:1] * buf[cur, 0]
    for s in range(1, TOP_K):
        y = y + gw[:, s:s + 1] * buf[cur, s]
    o_ref[...] = x_ref[...] + gate_ref[0] * _rms_norm(y, g_ref[...])


def _combine(ys, pos, gate_w, x2, g_post, gate, seq):
    t, d = x2.shape
    tm = TM_CMB
    per_b = seq // tm
    return pl.pallas_call(
        _combine_kernel,
        grid_spec=pltpu.PrefetchScalarGridSpec(
            num_scalar_prefetch=1,
            grid=(t // tm,),
            in_specs=[
                pl.BlockSpec(memory_space=pl.ANY),
                pl.BlockSpec((tm, LANES), lambda i, pos: (i, 0)),
                pl.BlockSpec((tm, d), lambda i, pos: (i, 0)),
                pl.BlockSpec((1, d), lambda i, pos: (0, 0)),
                pl.BlockSpec((1, 1, d), lambda i, pos: (i // per_b, 0, 0)),
            ],
            out_specs=pl.BlockSpec((tm, d), lambda i, pos: (i, 0)),
            scratch_shapes=[pltpu.VMEM((2, TOP_K, tm, d), F32), pltpu.SemaphoreType.DMA((2,))],
        ),
        out_shape=jax.ShapeDtypeStruct((t, d), F32),
        compiler_params=_cparams("arbitrary"),
        name="expert_combine",
    )(pos, ys, gate_w, x2, g_post, gate)


def _moe_ffn(x2, g_pre, shift, scale, w_router, b_router, w_gate, w_up, w_down, g_post, gate, seq):
    t = x2.shape[0]
    n_tiles = (TOP_K * t) // TM_MOE + N_EXPERTS
    gate_w, top = _router(x2, g_pre, shift, scale, w_router, b_router, seq)
    pos, tile_expert, n_active, pad_start, pad_len = _routing_tables(top[:, :TOP_K], n_tiles)
    hs = _dispatch(x2, g_pre, shift, scale, pos, pad_start, pad_len, n_active, n_tiles, seq)
    ys = _expert_ffn(hs, tile_expert, n_active, w_gate, w_up, w_down)
    return _combine(ys, pos, gate_w, x2, g_post, gate, seq)


def kernel(x, c, positions, ln_mix_pre, ln_mix_post, ln_ffn_pre, ln_ffn_post, w_mod, b_mod, w_in, b_fgate, w_out, w_ffn_gate, w_ffn_up, w_ffn_down, w_router, b_router, w_exp_gate, w_exp_up, w_exp_down):
    bsz, seq, d = x.shape
    depth = w_mod.shape[0]
    n_qkv = 3 * N_HEADS * HEAD_DIM
    x2 = x.reshape(bsz * seq, d)
    mod = _modulation(c, w_mod, b_mod).reshape(depth, bsz, N_MOD, 1, d)
    cos, sin = _rope_tables(positions)
    for layer in range(depth):
        sh_m, sc_m, g_m, sh_f, sc_f, g_f = (mod[layer, :, k] for k in range(N_MOD))
        row = lambda a: a[layer].reshape(1, d)
        w_qkv = _mxu(w_in[layer, :, :n_qkv])
        w_f = _mxu(jnp.pad(w_in[layer, :, n_qkv:], ((0, 0), (0, LANES - N_HEADS_FOX))))
        qkv, fg = _in_proj(x2, row(ln_mix_pre), sh_m, sc_m, w_qkv, w_f, seq)
        qkv3 = qkv.reshape(bsz, seq, n_qkv)
        cum_f = _forget_cumsum(fg.reshape(bsz, seq, LANES), b_fgate[layer])
        o_dil = _dilated_attention(qkv3, cos, sin).reshape(bsz * seq, -1)
        o_moba = _moba_attention(qkv3, cos, sin).reshape(bsz * seq, -1)
        o_fox = _fox_attention(qkv3, cum_f).reshape(bsz * seq, -1)
        x2 = _out_proj(o_dil, o_moba, o_fox, _mxu(w_out[layer]), x2, row(ln_mix_post), g_m, seq)
        j = layer // 2
        if layer % 2 == 0:
            x2 = _dense_ffn(x2, row(ln_ffn_pre), sh_f, sc_f, _mxu(w_ffn_gate[j]), _mxu(w_ffn_up[j]),
                            _mxu(w_ffn_down[j]), row(ln_ffn_post), g_f, seq)
        else:
            x2 = _moe_ffn(x2, row(ln_ffn_pre), sh_f, sc_f, w_router[j], b_router[j],
                          _mxu(w_exp_gate[j]), _mxu(w_exp_up[j]), _mxu(w_exp_down[j]),
                          row(ln_ffn_post), g_f, seq)
    return x2.reshape(bsz, seq, d)
```

```python
import functools

import numpy as np
import jax
import jax.numpy as jnp
from jax import lax
from jax.experimental import pallas as pl
from jax.experimental.pallas import tpu as pltpu

F32 = jnp.float32
MXU_DTYPE = jnp.bfloat16

HEAD_DIM = 128
N_HEADS = 16
N_HEADS_DIL = 6
N_HEADS_MOBA = 4
N_HEADS_FOX = 6
DILATED_BRANCHES = ((128, 1), (512, 4), (2048, 16))
MOBA_BLOCK = 256
MOBA_TOPK = 3
ROPE_THETA = 500000.0
ROT_DIM = HEAD_DIM // 4
ROT_HALF = ROT_DIM // 2
N_EXPERTS = 8
TOP_K = 2
N_MOD = 6
RMS_EPS = 1e-6
NEG_INF = -1e30
ATTN_SCALE = HEAD_DIM ** -0.5
LANES = 128

VMEM_LIMIT_BYTES = 56 * 1024 * 1024

TM_IN = 1024
TN_IN = 1536
TM_OUT = 512
TM_FFN = 512
TF_FFN = 1024
TM_DSP = 256
TF_FIRST = 256
DMA_UNROLL = 8
TM_MOE = 512
TM_CMB = 256
TQ_ATT = 512
TN_MOD = 1024


def _cparams(*sem):
    return pltpu.CompilerParams(dimension_semantics=sem, vmem_limit_bytes=VMEM_LIMIT_BYTES)


def _mxu(a):
    return a.astype(MXU_DTYPE)


def _dot(a, b):
    return jnp.dot(a, b, preferred_element_type=F32)


def _dot_nt(a, b):
    return lax.dot_general(a, b, (((1,), (1,)), ((), ())), preferred_element_type=F32)


def _split_hi_lo(a):
    hi = a.astype(MXU_DTYPE)
    lo = (a - hi.astype(F32)).astype(MXU_DTYPE)
    return hi, lo


def _rms_norm(x, g):
    ms = jnp.mean(x * x, axis=-1, keepdims=True)
    return x * lax.rsqrt(ms + RMS_EPS) * g


def _sigmoid(x):
    return 1.0 / (1.0 + jnp.exp(-x))


def _mod_kernel(c_ref, w_ref, b_ref, o_ref):
    c = c_ref[...]
    cond = c * _sigmoid(c)
    o_ref[0] = _dot(_mxu(cond), _mxu(w_ref[0])) + b_ref[0]


def _modulation(c, w_mod, b_mod):
    depth, d, n = w_mod.shape
    bsz = c.shape[0]
    rows = max(16, bsz)
    c_pad = jnp.pad(c, ((0, rows - bsz), (0, 0)))
    out = pl.pallas_call(
        _mod_kernel,
        grid=(depth, n // TN_MOD),
        in_specs=[
            pl.BlockSpec((rows, d), lambda l, j: (0, 0)),
            pl.BlockSpec((1, d, TN_MOD), lambda l, j: (l, 0, j)),
            pl.BlockSpec((1, 1, TN_MOD), lambda l, j: (l, 0, j)),
        ],
        out_specs=pl.BlockSpec((1, rows, TN_MOD), lambda l, j: (l, 0, j)),
        out_shape=jax.ShapeDtypeStruct((depth, rows, n), F32),
        compiler_params=_cparams("parallel", "parallel"),
        name="modulation",
    )(c_pad, w_mod, b_mod.reshape(depth, 1, n))
    return out[:, :bsz]


def _rope_table_kernel(pos_ref, freq_ref, cos_ref, sin_ref):
    ang = pos_ref[0].astype(F32) * freq_ref[...]
    lane = lax.broadcasted_iota(jnp.int32, ang.shape, 1)
    cos_ref[0] = jnp.where(lane < ROT_DIM, jnp.cos(ang), 1.0)
    sn = jnp.sin(ang)
    sin_ref[0] = jnp.where(lane < ROT_HALF, -sn, jnp.where(lane < ROT_DIM, sn, 0.0))


def _rope_tables(positions):
    bsz, seq = positions.shape
    inv_freq = ROPE_THETA ** (-np.arange(ROT_HALF, dtype=np.float32) / ROT_HALF)
    freq = np.zeros((1, LANES), np.float32)
    freq[0, :ROT_HALF] = inv_freq
    freq[0, ROT_HALF:ROT_DIM] = inv_freq
    tab = jax.ShapeDtypeStruct((bsz, seq, LANES), F32)
    return pl.pallas_call(
        _rope_table_kernel,
        grid=(bsz,),
        in_specs=[
            pl.BlockSpec((1, seq, 1), lambda b: (b, 0, 0)),
            pl.BlockSpec((1, LANES), lambda b: (0, 0)),
        ],
        out_specs=[pl.BlockSpec((1, seq, LANES), lambda b: (b, 0, 0))] * 2,
        out_shape=[tab, tab],
        compiler_params=_cparams("parallel"),
        name="rope_tables",
    )(positions.reshape(bsz, seq, 1), jnp.asarray(freq))


def _rope(t, cos, sin_signed):
    lane = lax.broadcasted_iota(jnp.int32, t.shape, 1)
    partner = jnp.where(lane < ROT_HALF,
                        pltpu.roll(t, LANES - ROT_HALF, 1),
                        pltpu.roll(t, ROT_HALF, 1))
    return t * cos + partner * sin_signed


def _in_proj_kernel(x_ref, g_ref, sh_ref, sc_ref, w_ref, wf_ref, qkv_ref, fg_ref, h_scr):
    @pl.when(pl.program_id(1) == 0)
    def _():
        h = _rms_norm(x_ref[...], g_ref[...]) * (1.0 + sc_ref[0]) + sh_ref[0]
        hb = _mxu(h)
        h_scr[...] = hb
        fg_ref[...] = _dot(hb, wf_ref[...])

    qkv_ref[...] = _dot(h_scr[...], w_ref[...]).astype(qkv_ref.dtype)


def _in_proj(x2, g, shift, scale, w_qkv, w_f, seq):
    t, d = x2.shape
    n = w_qkv.shape[1]
    per_b = seq // TM_IN
    bmap = lambda i, j: (i // per_b, 0, 0)
    return pl.pallas_call(
        _in_proj_kernel,
        grid=(t // TM_IN, n // TN_IN),
        in_specs=[
            pl.BlockSpec((TM_IN, d), lambda i, j: (i, 0)),
            pl.BlockSpec((1, d), lambda i, j: (0, 0)),
            pl.BlockSpec((1, 1, d), bmap),
            pl.BlockSpec((1, 1, d), bmap),
            pl.BlockSpec((d, TN_IN), lambda i, j: (0, j)),
            pl.BlockSpec((d, LANES), lambda i, j: (0, 0)),
        ],
        out_specs=[
            pl.BlockSpec((TM_IN, TN_IN), lambda i, j: (i, j)),
            pl.BlockSpec((TM_IN, LANES), lambda i, j: (i, 0)),
        ],
        out_shape=[
            jax.ShapeDtypeStruct((t, n), MXU_DTYPE),
            jax.ShapeDtypeStruct((t, LANES), F32),
        ],
        scratch_shapes=[pltpu.VMEM((TM_IN, d), MXU_DTYPE)],
        compiler_params=_cparams("parallel", "arbitrary"),
        name="in_proj",
    )(x2, g, shift, scale, w_qkv, w_f)


def _forget_kernel(fg_ref, b_ref, f_ref):
    z = fg_ref[0] + b_ref[...]
    log_f = jnp.minimum(z, 0.0) - jnp.log(1.0 + jnp.exp(-jnp.abs(z)))
    cum = log_f.T[0:8]
    seq = cum.shape[1]
    lane = lax.broadcasted_iota(jnp.int32, cum.shape, 1)
    shift = 1
    while shift < seq:
        cum = cum + jnp.where(lane >= shift, pltpu.roll(cum, shift, 1), 0.0)
        shift *= 2
    f_ref[0] = cum


def _forget_cumsum(fg3, b_fgate):
    bsz, seq, _ = fg3.shape
    b_pad = jnp.pad(b_fgate, (0, LANES - b_fgate.shape[0])).reshape(1, LANES)
    return pl.pallas_call(
        _forget_kernel,
        grid=(bsz,),
        in_specs=[
            pl.BlockSpec((1, seq, LANES), lambda b: (b, 0, 0)),
            pl.BlockSpec((1, LANES), lambda b: (0, 0)),
        ],
        out_specs=pl.BlockSpec((1, 8, seq), lambda b: (b, 0, 0)),
        out_shape=jax.ShapeDtypeStruct((bsz, 8, seq), F32),
        compiler_params=_cparams("parallel"),
        name="forget_cumsum",
    )(fg3, b_pad)


LOG2E = 1.4426950408889634
Q_SCALE = ATTN_SCALE * LOG2E


def _softmax_pv(pieces, v):
    m = pieces[0]
    for s in pieces[1:]:
        m = jnp.maximum(m, s)
    m = jnp.max(m, axis=-1, keepdims=True)
    ps = [jnp.exp2(s - m) for s in pieces]
    l = ps[0]
    for p in ps[1:]:
        l = l + p
    l = jnp.sum(l, axis=-1, keepdims=True)
    p_all = _mxu(ps[0]) if len(ps) == 1 else jnp.concatenate([_mxu(p) for p in ps], axis=1)
    return _dot(p_all, v) / l


def _causal_mask(s):
    row = lax.broadcasted_iota(jnp.int32, s.shape, 0)
    col = lax.broadcasted_iota(jnp.int32, s.shape, 1)
    return jnp.where(col <= row, s, NEG_INF)


def _qkv_specs(seq, head0):
    return [
        pl.BlockSpec((1, seq, HEAD_DIM), lambda b, h: (b, 0, head0 + h)),
        pl.BlockSpec((1, seq, HEAD_DIM), lambda b, h: (b, 0, N_HEADS + head0 + h)),
        pl.BlockSpec((1, seq, HEAD_DIM), lambda b, h: (b, 0, 2 * N_HEADS + head0 + h)),
    ]


def _rope_k_into(k_ref, cos_ref, sin_ref, k_scr, blk):
    seq = k_scr.shape[0]
    for j in range(seq // blk):
        rows = slice(j * blk, (j + 1) * blk)
        k_scr[rows] = _mxu(_rope(k_ref[0, rows].astype(F32), cos_ref[0, rows], sin_ref[0, rows]))


def _dilated_bias(tq):
    n_off = 2048 // tq
    off = jnp.arange(n_off, dtype=jnp.int32)[:, None, None] * tq
    delta = off + jnp.arange(tq, dtype=jnp.int32)[None, :, None] - jnp.arange(tq, dtype=jnp.int32)[None, None, :]
    count = jnp.zeros(delta.shape, F32)
    for window, dil in DILATED_BRANCHES:
        count = count + ((delta >= 0) & (delta % dil == 0) & (delta <= window)).astype(F32)
    return jnp.where(count > 0, jnp.log2(jnp.maximum(count, 1.0)), NEG_INF)


def _dilated_kernel(q_ref, k_ref, v_ref, cos_ref, sin_ref, bias_ref, o_ref, k_scr):
    tq = TQ_ATT
    seq = k_scr.shape[0]
    _rope_k_into(k_ref, cos_ref, sin_ref, k_scr, tq)

    def logits(qi):
        rows = slice(qi * tq, (qi + 1) * tq)
        q = _rope(q_ref[0, rows].astype(F32), cos_ref[0, rows], sin_ref[0, rows])
        return _dot_nt(_mxu(q * Q_SCALE), k_scr[:(qi + 1) * tq])

    n_q = seq // tq
    s_next = logits(0)
    for qi in range(n_q):
        s = s_next
        if qi + 1 < n_q:
            s_next = logits(qi + 1)
        pieces = [s[:, kj * tq:(kj + 1) * tq] + bias_ref[qi - kj] for kj in range(qi + 1)]
        o = _softmax_pv(pieces, v_ref[0, :(qi + 1) * tq])
        o_ref[0, qi * tq:(qi + 1) * tq] = o.astype(o_ref.dtype)


def _dilated_attention(qkv3, cos, sin):
    bsz, seq, _ = qkv3.shape
    assert seq == 2048, "dilated windows are laid out for a 2048-token sequence"
    bias = _dilated_bias(TQ_ATT)
    tab = pl.BlockSpec((1, seq, LANES), lambda b, h: (b, 0, 0))
    return pl.pallas_call(
        _dilated_kernel,
        grid=(bsz, N_HEADS_DIL),
        in_specs=_qkv_specs(seq, 0) + [tab, tab,
                                       pl.BlockSpec(bias.shape, lambda b, h: (0, 0, 0))],
        out_specs=pl.BlockSpec((1, seq, HEAD_DIM), lambda b, h: (b, 0, h)),
        out_shape=jax.ShapeDtypeStruct((bsz, seq, N_HEADS_DIL * HEAD_DIM), MXU_DTYPE),
        scratch_shapes=[pltpu.VMEM((seq, HEAD_DIM), MXU_DTYPE)],
        compiler_params=_cparams("parallel", "parallel"),
        name="dilated_attention",
    )(qkv3, qkv3, qkv3, cos, sin, bias)


def _moba_kernel(q_ref, k_ref, v_ref, cos_ref, sin_ref, o_ref, k_scr, km_scr):
    blk = MOBA_BLOCK
    seq = k_scr.shape[0]
    n_blk = seq // blk
    km_scr[...] = jnp.zeros(km_scr.shape, F32)
    for j in range(n_blk):
        rows = slice(j * blk, (j + 1) * blk)
        kr = _rope(k_ref[0, rows].astype(F32), cos_ref[0, rows], sin_ref[0, rows])
        k_scr[rows] = _mxu(kr)
        km_scr[j:j + 1, :] = jnp.mean(kr, axis=0, keepdims=True)
    km_hi, km_lo = _split_hi_lo(km_scr[...])

    def logits(qi):
        rows = slice(qi * blk, (qi + 1) * blk)
        qf = _rope(q_ref[0, rows].astype(F32), cos_ref[0, rows], sin_ref[0, rows])
        s = _dot_nt(_mxu(qf * Q_SCALE), k_scr[:(qi + 1) * blk])
        if qi == 0:
            return s, None
        q_hi, q_lo = _split_hi_lo(qf)
        gate = _dot_nt(q_hi, km_hi) + _dot_nt(q_lo, km_hi) + _dot_nt(q_hi, km_lo)
        lane = lax.broadcasted_iota(jnp.int32, gate.shape, 1)
        past = lane < qi
        gate = jnp.where(past, gate, NEG_INF)
        rank = jnp.zeros(gate.shape, F32)
        for jp in range(qi):
            g_jp = gate[:, jp:jp + 1]
            ahead = (g_jp > gate) | ((g_jp == gate) & (lane > jp))
            rank = rank + jnp.where(ahead, 1.0, 0.0)
        return s, jnp.where(past & (rank < MOBA_TOPK), 0.0, NEG_INF)

    nxt = logits(0)
    for qi in range(n_blk):
        s, sel_bias = nxt
        if qi + 1 < n_blk:
            nxt = logits(qi + 1)
        pieces = [s[:, kj * blk:(kj + 1) * blk] + sel_bias[:, kj:kj + 1] for kj in range(qi)]
        pieces.append(_causal_mask(s[:, qi * blk:(qi + 1) * blk]))
        o = _softmax_pv(pieces, v_ref[0, :(qi + 1) * blk])
        o_ref[0, qi * blk:(qi + 1) * blk] = o.astype(o_ref.dtype)


def _moba_attention(qkv3, cos, sin):
    bsz, seq, _ = qkv3.shape
    assert seq % MOBA_BLOCK == 0 and seq // MOBA_BLOCK <= LANES
    tab = pl.BlockSpec((1, seq, LANES), lambda b, h: (b, 0, 0))
    return pl.pallas_call(
        _moba_kernel,
        grid=(bsz, N_HEADS_MOBA),
        in_specs=_qkv_specs(seq, N_HEADS_DIL) + [tab, tab],
        out_specs=pl.BlockSpec((1, seq, HEAD_DIM), lambda b, h: (b, 0, h)),
        out_shape=jax.ShapeDtypeStruct((bsz, seq, N_HEADS_MOBA * HEAD_DIM), MXU_DTYPE),
        scratch_shapes=[pltpu.VMEM((seq, HEAD_DIM), MXU_DTYPE),
                        pltpu.VMEM((LANES, HEAD_DIM), F32)],
        compiler_params=_cparams("parallel", "parallel"),
        name="moba_attention",
    )(qkv3, qkv3, qkv3, cos, sin)


def _fox_kernel(q_ref, k_ref, v_ref, f_ref, o_ref):
    tq = TQ_ATT
    seq = q_ref.shape[1]
    def logits(qi):
        q = _mxu(q_ref[0, qi * tq:(qi + 1) * tq].astype(F32) * Q_SCALE)
        return _dot_nt(q, k_ref[0, :(qi + 1) * tq])

    n_q = seq // tq
    s_next = logits(0)
    for qi in range(n_q):
        past = slice(0, (qi + 1) * tq)
        s = s_next - f_ref[0, 0, :, past] * LOG2E
        if qi + 1 < n_q:
            s_next = logits(qi + 1)
        pieces = [s[:, kj * tq:(kj + 1) * tq] for kj in range(qi)]
        pieces.append(_causal_mask(s[:, qi * tq:(qi + 1) * tq]))
        o_ref[0, qi * tq:(qi + 1) * tq] = _softmax_pv(pieces, v_ref[0, past]).astype(o_ref.dtype)


def _fox_attention(qkv3, cum_f):
    bsz, seq, _ = qkv3.shape
    f4 = cum_f.reshape(bsz, cum_f.shape[1], 1, seq)
    return pl.pallas_call(
        _fox_kernel,
        grid=(bsz, N_HEADS_FOX),
        in_specs=_qkv_specs(seq, N_HEADS_DIL + N_HEADS_MOBA)
        + [pl.BlockSpec((1, 1, 1, seq), lambda b, h: (b, h, 0, 0))],
        out_specs=pl.BlockSpec((1, seq, HEAD_DIM), lambda b, h: (b, 0, h)),
        out_shape=jax.ShapeDtypeStruct((bsz, seq, N_HEADS_FOX * HEAD_DIM), MXU_DTYPE),
        compiler_params=_cparams("parallel", "parallel"),
        name="fox_attention",
    )(qkv3, qkv3, qkv3, f4)


def _out_proj_kernel(od_ref, om_ref, of_ref, wd_ref, wm_ref, wf_ref, x_ref, g_ref, gate_ref, o_ref):
    y = _dot(od_ref[...], wd_ref[...]) + _dot(om_ref[...], wm_ref[...]) + _dot(of_ref[...], wf_ref[...])
    o_ref[...] = x_ref[...] + gate_ref[0] * _rms_norm(y, g_ref[...])


def _out_proj(o_dil, o_moba, o_fox, w_out, x2, g, gate, seq):
    t, d = x2.shape
    nd, nm = o_dil.shape[1], o_moba.shape[1]
    w_d, w_m, w_f = w_out[:nd], w_out[nd:nd + nm], w_out[nd + nm:]
    per_b = seq // TM_OUT
    row = lambda i: (i, 0)
    fixed = lambda i: (0, 0)
    return pl.pallas_call(
        _out_proj_kernel,
        grid=(t // TM_OUT,),
        in_specs=[
            pl.BlockSpec((TM_OUT, nd), row),
            pl.BlockSpec((TM_OUT, nm), row),
            pl.BlockSpec((TM_OUT, o_fox.shape[1]), row),
            pl.BlockSpec(w_d.shape, fixed),
            pl.BlockSpec(w_m.shape, fixed),
            pl.BlockSpec(w_f.shape, fixed),
            pl.BlockSpec((TM_OUT, d), row),
            pl.BlockSpec((1, d), fixed),
            pl.BlockSpec((1, 1, d), lambda i: (i // per_b, 0, 0)),
        ],
        out_specs=pl.BlockSpec((TM_OUT, d), row),
        out_shape=jax.ShapeDtypeStruct((t, d), F32),
        compiler_params=_cparams("parallel"),
        name="out_proj",
    )(o_dil, o_moba, o_fox, w_d, w_m, w_f, x2, g, gate)


def _swiglu_tile(h, wg, wu, wd):
    g = _dot(h, wg)
    u = _dot(h, wu)
    return _dot(_mxu(g * _sigmoid(g) * u), wd)


def _ffn_kernel(x_ref, gpre_ref, sh_ref, sc_ref, wg_ref, wu_ref, wd_ref, gpost_ref, gate_ref,
                o_ref, h_scr, acc_scr):
    j = pl.program_id(1)

    @pl.when(j == 0)
    def _():
        h = _rms_norm(x_ref[...], gpre_ref[...]) * (1.0 + sc_ref[0]) + sh_ref[0]
        h_scr[...] = _mxu(h)
        acc_scr[...] = jnp.zeros(acc_scr.shape, F32)

    acc_scr[...] += _swiglu_tile(h_scr[...], wg_ref[...], wu_ref[...], wd_ref[...])

    @pl.when(j == pl.num_programs(1) - 1)
    def _():
        o_ref[...] = x_ref[...] + gate_ref[0] * _rms_norm(acc_scr[...], gpost_ref[...])


def _dense_ffn(x2, g_pre, shift, scale, w_gate, w_up, w_down, g_post, gate, seq):
    t, d = x2.shape
    d_ff = w_gate.shape[1]
    per_b = seq // TM_FFN
    row = lambda i, j: (i, 0)
    fixed = lambda i, j: (0, 0)
    bmap = lambda i, j: (i // per_b, 0, 0)
    return pl.pallas_call(
        _ffn_kernel,
        grid=(t // TM_FFN, d_ff // TF_FFN),
        in_specs=[
            pl.BlockSpec((TM_FFN, d), row),
            pl.BlockSpec((1, d), fixed),
            pl.BlockSpec((1, 1, d), bmap),
            pl.BlockSpec((1, 1, d), bmap),
            pl.BlockSpec((d, TF_FFN), lambda i, j: (0, j)),
            pl.BlockSpec((d, TF_FFN), lambda i, j: (0, j)),
            pl.BlockSpec((TF_FFN, d), lambda i, j: (j, 0)),
            pl.BlockSpec((1, d), fixed),
            pl.BlockSpec((1, 1, d), bmap),
        ],
        out_specs=pl.BlockSpec((TM_FFN, d), row),
        out_shape=jax.ShapeDtypeStruct((t, d), F32),
        scratch_shapes=[pltpu.VMEM((TM_FFN, d), MXU_DTYPE), pltpu.VMEM((TM_FFN, d), F32)],
        compiler_params=_cparams("parallel", "arbitrary"),
        name="dense_ffn",
    )(x2, g_pre, shift, scale, w_gate, w_up, w_down, g_post, gate)


def _router_kernel(x_ref, g_ref, sh_ref, sc_ref, whi_ref, wlo_ref, br_ref, gw_ref, gi_ref):
    h = _rms_norm(x_ref[...], g_ref[...]) * (1.0 + sc_ref[0]) + sh_ref[0]
    h_hi, h_lo = _split_hi_lo(h)
    logits = (_dot(h_hi, whi_ref[...]) + _dot(h_lo, whi_ref[...]) + _dot(h_hi, wlo_ref[...])
              + br_ref[...])
    lane = lax.broadcasted_iota(jnp.int32, logits.shape, 1)
    logits = jnp.where(lane < N_EXPERTS, logits, NEG_INF)
    m1 = jnp.max(logits, axis=-1, keepdims=True)
    i1 = jnp.min(jnp.where(logits == m1, lane, LANES), axis=-1, keepdims=True)
    rest = jnp.where(lane == i1, NEG_INF, logits)
    m2 = jnp.max(rest, axis=-1, keepdims=True)
    i2 = jnp.min(jnp.where(rest == m2, lane, LANES), axis=-1, keepdims=True)
    e2 = jnp.exp(m2 - m1)
    p1 = 1.0 / (1.0 + e2)
    p2 = e2 * p1
    gw_ref[...] = jnp.where(lane == 0, p1, jnp.where(lane == 1, p2, 0.0))
    gi_ref[...] = jnp.where(lane == 0, i1, jnp.where(lane == 1, i2, 0))


def _router(x2, g_pre, shift, scale, w_router, b_router, seq):
    t, d = x2.shape
    tm = TM_OUT
    w_pad = jnp.pad(w_router, ((0, 0), (0, LANES - N_EXPERTS)))
    w_hi = w_pad.astype(MXU_DTYPE)
    w_lo = (w_pad - w_hi.astype(F32)).astype(MXU_DTYPE)
    b_pad = jnp.pad(b_router, (0, LANES - N_EXPERTS)).reshape(1, LANES)
    per_b = seq // tm
    row = lambda i: (i, 0)
    fixed = lambda i: (0, 0)
    bmap = lambda i: (i // per_b, 0, 0)
    return pl.pallas_call(
        _router_kernel,
        grid=(t // tm,),
        in_specs=[
            pl.BlockSpec((tm, d), row),
            pl.BlockSpec((1, d), fixed),
            pl.BlockSpec((1, 1, d), bmap),
            pl.BlockSpec((1, 1, d), bmap),
            pl.BlockSpec((d, LANES), fixed),
            pl.BlockSpec((d, LANES), fixed),
            pl.BlockSpec((1, LANES), fixed),
        ],
        out_specs=[pl.BlockSpec((tm, LANES), row), pl.BlockSpec((tm, LANES), row)],
        out_shape=[jax.ShapeDtypeStruct((t, LANES), F32), jax.ShapeDtypeStruct((t, LANES), jnp.int32)],
        compiler_params=_cparams("parallel"),
        name="router",
    )(x2, g_pre, shift, scale, w_hi, w_lo, b_pad)


def _routing_tables(top_idx, n_tiles):
    flat_e = top_idx.T.reshape(-1)
    experts = jnp.arange(N_EXPERTS, dtype=jnp.int32)
    onehot = (flat_e[:, None] == experts[None, :]).astype(jnp.int32)
    csum = jnp.cumsum(onehot, axis=0)
    rank = jnp.sum(onehot * csum, axis=1) - 1
    counts = csum[-1]
    padded = ((counts + TM_MOE - 1) // TM_MOE) * TM_MOE
    ends = jnp.cumsum(padded)
    starts = ends - padded
    pos = (jnp.sum(onehot * starts[None, :], axis=1) + rank).astype(jnp.int32)
    tile_start = jnp.arange(n_tiles, dtype=jnp.int32) * TM_MOE
    tile_expert = jnp.minimum(jnp.sum((tile_start[:, None] >= ends[None, :]).astype(jnp.int32), axis=1),
                              N_EXPERTS - 1).astype(jnp.int32)
    n_active = (ends[-1] // TM_MOE).astype(jnp.int32).reshape(1)
    pad_start = (starts + counts).astype(jnp.int32)
    pad_len = (padded - counts).astype(jnp.int32)
    tiles = jnp.arange(n_tiles, dtype=jnp.int32)
    first_tile = (starts // TM_MOE).astype(jnp.int32)
    has_rows = (counts > 0).astype(jnp.int32)
    active = tiles < n_active[0]
    first_of_tile = jnp.sum(jnp.where(tile_expert[:, None] == experts[None, :], first_tile[None, :], 0), axis=1)
    is_first = active & (tiles == first_of_tile)
    frozen = is_first | ~active
    prev_live = jnp.max(jnp.where((tiles[None, :] <= tiles[:, None]) & ~frozen[None, :], tiles[None, :], -1),
                        axis=1)
    weight_expert = jnp.sum(jnp.where(tiles[None, :] == jnp.maximum(prev_live, 0)[:, None],
                                      tile_expert[None, :], 0), axis=1)
    return (pos, tile_expert, n_active, pad_start, pad_len, first_tile, has_rows,
            is_first.astype(jnp.int32), frozen.astype(jnp.int32), weight_expert.astype(jnp.int32))


def _row_copy(src, src_row, dst, dst_row, sem):
    return pltpu.make_async_copy(src.at[pl.ds(src_row, 1)], dst.at[pl.ds(dst_row, 1)], sem)


def _for_row_groups(n_rows, body):
    def trip(g, c):
        r0 = pl.multiple_of(g * DMA_UNROLL, DMA_UNROLL)
        for k in range(DMA_UNROLL):
            body(r0 + k)
        return c

    lax.fori_loop(0, n_rows // DMA_UNROLL, trip, 0)


def _dispatch_kernel(pos_ref, pstart_ref, plen_ref, nact_ref, x_ref, g_ref, sh_ref, sc_ref, hs_hbm,
                     hbuf, sem, zsem):
    i = pl.program_id(0)
    n = pl.num_programs(0)
    tm = x_ref.shape[0]
    t_total = n * tm
    slot = i % 2

    def wait_slot(s):
        for _ in range(TOP_K):
            pltpu.make_async_copy(hbuf.at[s], hs_hbm.at[pl.ds(0, tm)], sem.at[s]).wait()

    @pl.when(i >= 2)
    def _():
        wait_slot(slot)

    hbuf[slot] = _rms_norm(x_ref[...], g_ref[...]) * (1.0 + sc_ref[0]) + sh_ref[0]

    def send(r):
        for s in range(TOP_K):
            _row_copy(hbuf.at[slot], r, hs_hbm, pos_ref[s * t_total + i * tm + r], sem.at[slot]).start()

    _for_row_groups(tm, send)

    @pl.when(i == n - 1)
    def _():
        wait_slot(slot)

        @pl.when(n >= 2)
        def _():
            wait_slot(1 - slot)

        zeros = hbuf.at[0]
        zeros[...] = jnp.zeros(zeros.shape, F32)
        for e in range(N_EXPERTS):
            def zero_row(k, c):
                _row_copy(zeros, 0, hs_hbm, pstart_ref[e] + k, zsem).start()
                return c

            def zero_row_wait(k, c):
                _row_copy(zeros, 0, hs_hbm, 0, zsem).wait()
                return c

            lax.fori_loop(0, plen_ref[e], zero_row, 0)
            lax.fori_loop(0, plen_ref[e], zero_row_wait, 0)

        tail0 = nact_ref[0] * TM_MOE
        n_tail = (hs_hbm.shape[0] - tail0) // tm

        def zero_block(k, c):
            pltpu.make_async_copy(zeros, hs_hbm.at[pl.ds(tail0 + k * tm, tm)], zsem).start()
            return c

        def zero_block_wait(k, c):
            pltpu.make_async_copy(zeros, hs_hbm.at[pl.ds(0, tm)], zsem).wait()
            return c

        lax.fori_loop(0, n_tail, zero_block, 0)
        lax.fori_loop(0, n_tail, zero_block_wait, 0)


def _dispatch(x2, g_pre, shift, scale, pos, pad_start, pad_len, n_active, n_tiles, seq):
    t, d = x2.shape
    tm = TM_DSP
    assert TM_MOE % tm == 0
    per_b = seq // tm
    return pl.pallas_call(
        _dispatch_kernel,
        grid_spec=pltpu.PrefetchScalarGridSpec(
            num_scalar_prefetch=4,
            grid=(t // tm,),
            in_specs=[
                pl.BlockSpec((tm, d), lambda i, *_: (i, 0)),
                pl.BlockSpec((1, d), lambda i, *_: (0, 0)),
                pl.BlockSpec((1, 1, d), lambda i, *_: (i // per_b, 0, 0)),
                pl.BlockSpec((1, 1, d), lambda i, *_: (i // per_b, 0, 0)),
            ],
            out_specs=pl.BlockSpec(memory_space=pl.ANY),
            scratch_shapes=[pltpu.VMEM((2, tm, d), F32),
                            pltpu.SemaphoreType.DMA((2,)), pltpu.SemaphoreType.DMA(())],
        ),
        out_shape=jax.ShapeDtypeStruct((n_tiles * TM_MOE, d), F32),
        compiler_params=_cparams("arbitrary"),
        name="expert_dispatch",
    )(pos, pad_start, pad_len, n_active, x2, g_pre, shift, scale)


def _expert_first_kernel(ft_ref, has_ref, h_ref, wg_ref, wu_ref, wd_ref,
                         y_ref, qg_ref, qu_ref, qd_ref, hb_scr, acc_scr):
    e, j = pl.program_id(0), pl.program_id(1)
    wg, wu, wd = _mxu(wg_ref[0]), _mxu(wu_ref[0]), _mxu(wd_ref[0])
    qg_ref[0] = wg
    qu_ref[0] = wu
    qd_ref[0] = wd

    @pl.when(j == 0)
    def _():
        acc_scr[...] = jnp.zeros(acc_scr.shape, F32)
        hb_scr[...] = _mxu(h_ref[...])

    @pl.when(has_ref[e] > 0)
    def _():
        acc_scr[...] += _swiglu_tile(hb_scr[...], wg, wu, wd)

    @pl.when(j == pl.num_programs(1) - 1)
    def _():
        y_ref[...] = acc_scr[...]


def _expert_first(hs, first_tile, has_rows, w_gate, w_up, w_down):
    n_rows, d = hs.shape
    n_exp, _, d_ff = w_gate.shape
    tf = TF_FIRST
    last_tile = n_rows // TM_MOE - 1
    col = lambda e, j, ft, has: (e, 0, j)
    rowb = lambda e, j, ft, has: (e, j, 0)
    cast = lambda a: jax.ShapeDtypeStruct(a.shape, MXU_DTYPE)
    return pl.pallas_call(
        _expert_first_kernel,
        grid_spec=pltpu.PrefetchScalarGridSpec(
            num_scalar_prefetch=2,
            grid=(n_exp, d_ff // tf),
            in_specs=[
                pl.BlockSpec((TM_MOE, d), lambda e, j, ft, has: (jnp.minimum(ft[e], last_tile), 0)),
                pl.BlockSpec((1, d, tf), col),
                pl.BlockSpec((1, d, tf), col),
                pl.BlockSpec((1, tf, d), rowb),
            ],
            out_specs=[
                pl.BlockSpec((TM_MOE, d), lambda e, j, ft, has: (e, 0)),
                pl.BlockSpec((1, d, tf), col),
                pl.BlockSpec((1, d, tf), col),
                pl.BlockSpec((1, tf, d), rowb),
            ],
            scratch_shapes=[pltpu.VMEM((TM_MOE, d), MXU_DTYPE), pltpu.VMEM((TM_MOE, d), F32)],
        ),
        out_shape=[jax.ShapeDtypeStruct((n_exp * TM_MOE, d), F32), cast(w_gate), cast(w_up), cast(w_down)],
        compiler_params=_cparams("arbitrary", "arbitrary"),
        name="expert_first",
    )(first_tile, has_rows, hs, w_gate, w_up, w_down)


def _expert_kernel(te_ref, nact_ref, first_ref, frozen_ref, we_ref, h_ref, yf_ref, wg_ref, wu_ref, wd_ref,
                   o_ref, hb_scr, acc_scr):
    i, j = pl.program_id(0), pl.program_id(1)
    compute = frozen_ref[i] == 0

    @pl.when(j == 0)
    def _():
        acc_scr[...] = jnp.zeros(acc_scr.shape, F32)

    @pl.when(compute & (j == 0))
    def _():
        hb_scr[...] = _mxu(h_ref[...])

    @pl.when(compute)
    def _():
        acc_scr[...] += _swiglu_tile(hb_scr[...], wg_ref[0], wu_ref[0], wd_ref[0])

    last = j == pl.num_programs(1) - 1

    @pl.when(last & (first_ref[i] == 0))
    def _():
        o_ref[...] = acc_scr[...]

    @pl.when(last & (first_ref[i] != 0))
    def _():
        o_ref[...] = yf_ref[...]


def _expert_ffn(hs, y_first, tile_expert, n_active, is_first, frozen, weight_expert, w_gate, w_up, w_down):
    n_rows, d = hs.shape
    d_ff = w_gate.shape[2]
    n_tiles = n_rows // TM_MOE
    n_ff = d_ff // TF_FFN

    def ff_step(i, j, frozen):
        return jnp.where(frozen[i] != 0, n_ff - 1, j)

    def row_tile(i, nact):
        return jnp.minimum(i, nact[0] - 1)

    return pl.pallas_call(
        _expert_kernel,
        grid_spec=pltpu.PrefetchScalarGridSpec(
            num_scalar_prefetch=5,
            grid=(n_tiles, n_ff),
            in_specs=[
                pl.BlockSpec((TM_MOE, d), lambda i, j, te, nact, fi, fr, we: (row_tile(i, nact), 0)),
                pl.BlockSpec((TM_MOE, d), lambda i, j, te, nact, fi, fr, we: (te[i], 0),
                             pipeline_mode=pl.Buffered(1)),
                pl.BlockSpec((1, d, TF_FFN), lambda i, j, te, nact, fi, fr, we: (we[i], 0, ff_step(i, j, fr))),
                pl.BlockSpec((1, d, TF_FFN), lambda i, j, te, nact, fi, fr, we: (we[i], 0, ff_step(i, j, fr))),
                pl.BlockSpec((1, TF_FFN, d), lambda i, j, te, nact, fi, fr, we: (we[i], ff_step(i, j, fr), 0)),
            ],
            out_specs=pl.BlockSpec((TM_MOE, d), lambda i, j, te, nact, fi, fr, we: (i, 0)),
            scratch_shapes=[pltpu.VMEM((TM_MOE, d), MXU_DTYPE), pltpu.VMEM((TM_MOE, d), F32)],
        ),
        out_shape=jax.ShapeDtypeStruct((n_rows, d), F32),
        compiler_params=_cparams("arbitrary", "arbitrary"),
        name="expert_ffn",
    )(tile_expert, n_active, is_first, frozen, weight_expert, hs, y_first, w_gate, w_up, w_down)


def _combine_kernel(pos_ref, ys_hbm, gw_ref, x_ref, g_ref, gate_ref, o_ref, buf, sem):
    i = pl.program_id(0)
    n = pl.num_programs(0)
    tm = x_ref.shape[0]
    t_total = n * tm
    cur = i % 2

    def fetch(step, b):
        def one(r):
            for s in range(TOP_K):
                _row_copy(ys_hbm, pos_ref[s * t_total + step * tm + r], buf.at[b, s], r, sem.at[b]).start()

        _for_row_groups(tm, one)

    @pl.when(i == 0)
    def _():
        fetch(0, 0)

    @pl.when(i + 1 < n)
    def _():
        fetch(i + 1, 1 - cur)

    for s in range(TOP_K):
        pltpu.make_async_copy(ys_hbm.at[pl.ds(0, tm)], buf.at[cur, s], sem.at[cur]).wait()
    gw = gw_ref[...]
    y = gw[:, 0:1] * buf[cur, 0]
    for s in range(1, TOP_K):
        y = y + gw[:, s:s + 1] * buf[cur, s]
    o_ref[...] = x_ref[...] + gate_ref[0] * _rms_norm(y, g_ref[...])


def _combine(ys, pos, gate_w, x2, g_post, gate, seq):
    t, d = x2.shape
    tm = TM_CMB
    per_b = seq // tm
    return pl.pallas_call(
        _combine_kernel,
        grid_spec=pltpu.PrefetchScalarGridSpec(
            num_scalar_prefetch=1,
            grid=(t // tm,),
            in_specs=[
                pl.BlockSpec(memory_space=pl.ANY),
                pl.BlockSpec((tm, LANES), lambda i, pos: (i, 0)),
                pl.BlockSpec((tm, d), lambda i, pos: (i, 0)),
                pl.BlockSpec((1, d), lambda i, pos: (0, 0)),
                pl.BlockSpec((1, 1, d), lambda i, pos: (i // per_b, 0, 0)),
            ],
            out_specs=pl.BlockSpec((tm, d), lambda i, pos: (i, 0)),
            scratch_shapes=[pltpu.VMEM((2, TOP_K, tm, d), F32), pltpu.SemaphoreType.DMA((2,))],
        ),
        out_shape=jax.ShapeDtypeStruct((t, d), F32),
        compiler_params=_cparams("arbitrary"),
        name="expert_combine",
    )(pos, ys, gate_w, x2, g_post, gate)


def _moe_ffn(x2, g_pre, shift, scale, w_router, b_router, w_gate, w_up, w_down, g_post, gate, seq):
    t = x2.shape[0]
    n_tiles = (TOP_K * t) // TM_MOE + N_EXPERTS
    gate_w, top = _router(x2, g_pre, shift, scale, w_router, b_router, seq)
    (pos, tile_expert, n_active, pad_start, pad_len, first_tile, has_rows, is_first, frozen,
     weight_expert) = _routing_tables(top[:, :TOP_K], n_tiles)
    hs = _dispatch(x2, g_pre, shift, scale, pos, pad_start, pad_len, n_active, n_tiles, seq)
    y_first, q_gate, q_up, q_down = _expert_first(hs, first_tile, has_rows, w_gate, w_up, w_down)
    ys = _expert_ffn(hs, y_first, tile_expert, n_active, is_first, frozen, weight_expert, q_gate, q_up, q_down)
    return _combine(ys, pos, gate_w, x2, g_post, gate, seq)


def kernel(x, c, positions, ln_mix_pre, ln_mix_post, ln_ffn_pre, ln_ffn_post, w_mod, b_mod, w_in, b_fgate, w_out, w_ffn_gate, w_ffn_up, w_ffn_down, w_router, b_router, w_exp_gate, w_exp_up, w_exp_down):
    bsz, seq, d = x.shape
    depth = w_mod.shape[0]
    n_qkv = 3 * N_HEADS * HEAD_DIM
    x2 = x.reshape(bsz * seq, d)
    mod = _modulation(c, w_mod, b_mod).reshape(depth, bsz, N_MOD, 1, d)
    cos, sin = _rope_tables(positions)
    for layer in range(depth):
        sh_m, sc_m, g_m, sh_f, sc_f, g_f = (mod[layer, :, k] for k in range(N_MOD))
        row = lambda a: a[layer].reshape(1, d)
        w_qkv = _mxu(w_in[layer, :, :n_qkv])
        w_f = _mxu(jnp.pad(w_in[layer, :, n_qkv:], ((0, 0), (0, LANES - N_HEADS_FOX))))
        qkv, fg = _in_proj(x2, row(ln_mix_pre), sh_m, sc_m, w_qkv, w_f, seq)
        qkv3 = qkv.reshape(bsz, seq, n_qkv)
        cum_f = _forget_cumsum(fg.reshape(bsz, seq, LANES), b_fgate[layer])
        o_dil = _dilated_attention(qkv3, cos, sin).reshape(bsz * seq, -1)
        o_moba = _moba_attention(qkv3, cos, sin).reshape(bsz * seq, -1)
        o_fox = _fox_attention(qkv3, cum_f).reshape(bsz * seq, -1)
        x2 = _out_proj(o_dil, o_moba, o_fox, _mxu(w_out[layer]), x2, row(ln_mix_post), g_m, seq)
        j = layer // 2
        if layer % 2 == 0:
            x2 = _dense_ffn(x2, row(ln_ffn_pre), sh_f, sc_f, _mxu(w_ffn_gate[j]), _mxu(w_ffn_up[j]),
                            _mxu(w_ffn_down[j]), row(ln_ffn_post), g_f, seq)
        else:
            x2 = _moe_ffn(x2, row(ln_ffn_pre), sh_f, sc_f, w_router[j], b_router[j],
                          w_exp_gate[j], w_exp_up[j], w_exp_down[j], row(ln_ffn_post), g_f, seq)
    return x2.reshape(bsz, seq, d)
```

```python
import functools

import numpy as np
import jax
import jax.numpy as jnp
from jax import lax
from jax.experimental import pallas as pl
from jax.experimental.pallas import tpu as pltpu

F32 = jnp.float32
MXU_DTYPE = jnp.bfloat16

HEAD_DIM = 128
N_HEADS = 16
N_HEADS_DIL = 6
N_HEADS_MOBA = 4
N_HEADS_FOX = 6
DILATED_BRANCHES = ((128, 1), (512, 4), (2048, 16))
MOBA_BLOCK = 256
MOBA_TOPK = 3
ROPE_THETA = 500000.0
ROT_DIM = HEAD_DIM // 4
ROT_HALF = ROT_DIM // 2
N_EXPERTS = 8
TOP_K = 2
N_MOD = 6
RMS_EPS = 1e-6
NEG_INF = -1e30
ATTN_SCALE = HEAD_DIM ** -0.5
LANES = 128

VMEM_LIMIT_BYTES = 56 * 1024 * 1024

TM_IN = 1024
TN_IN = 1536
TM_OUT = 512
TM_FFN = 512
TF_FFN = 1024
TM_DSP = 256
TF_FIRST = 256
FIRST_TILES = 2
DMA_UNROLL = 8
TM_MOE = 512
TM_CMB = 256
TQ_ATT = 512
TN_MOD = 1024


def _cparams(*sem):
    return pltpu.CompilerParams(dimension_semantics=sem, vmem_limit_bytes=VMEM_LIMIT_BYTES)


def _mxu(a):
    return a.astype(MXU_DTYPE)


def _dot(a, b):
    return jnp.dot(a, b, preferred_element_type=F32)


def _dot_nt(a, b):
    return lax.dot_general(a, b, (((1,), (1,)), ((), ())), preferred_element_type=F32)


def _split_hi_lo(a):
    hi = a.astype(MXU_DTYPE)
    lo = (a - hi.astype(F32)).astype(MXU_DTYPE)
    return hi, lo


def _rms_norm(x, g):
    ms = jnp.mean(x * x, axis=-1, keepdims=True)
    return x * lax.rsqrt(ms + RMS_EPS) * g


def _sigmoid(x):
    return 1.0 / (1.0 + jnp.exp(-x))


def _mod_kernel(c_ref, w_ref, b_ref, o_ref):
    c = c_ref[...]
    cond = c * _sigmoid(c)
    o_ref[0] = _dot(_mxu(cond), _mxu(w_ref[0])) + b_ref[0]


def _modulation(c, w_mod, b_mod):
    depth, d, n = w_mod.shape
    bsz = c.shape[0]
    rows = max(16, bsz)
    c_pad = jnp.pad(c, ((0, rows - bsz), (0, 0)))
    out = pl.pallas_call(
        _mod_kernel,
        grid=(depth, n // TN_MOD),
        in_specs=[
            pl.BlockSpec((rows, d), lambda l, j: (0, 0)),
            pl.BlockSpec((1, d, TN_MOD), lambda l, j: (l, 0, j)),
            pl.BlockSpec((1, 1, TN_MOD), lambda l, j: (l, 0, j)),
        ],
        out_specs=pl.BlockSpec((1, rows, TN_MOD), lambda l, j: (l, 0, j)),
        out_shape=jax.ShapeDtypeStruct((depth, rows, n), F32),
        compiler_params=_cparams("parallel", "parallel"),
        name="modulation",
    )(c_pad, w_mod, b_mod.reshape(depth, 1, n))
    return out[:, :bsz]


def _rope_table_kernel(pos_ref, freq_ref, cos_ref, sin_ref):
    ang = pos_ref[0].astype(F32) * freq_ref[...]
    lane = lax.broadcasted_iota(jnp.int32, ang.shape, 1)
    cos_ref[0] = jnp.where(lane < ROT_DIM, jnp.cos(ang), 1.0)
    sn = jnp.sin(ang)
    sin_ref[0] = jnp.where(lane < ROT_HALF, -sn, jnp.where(lane < ROT_DIM, sn, 0.0))


def _rope_tables(positions):
    bsz, seq = positions.shape
    inv_freq = ROPE_THETA ** (-np.arange(ROT_HALF, dtype=np.float32) / ROT_HALF)
    freq = np.zeros((1, LANES), np.float32)
    freq[0, :ROT_HALF] = inv_freq
    freq[0, ROT_HALF:ROT_DIM] = inv_freq
    tab = jax.ShapeDtypeStruct((bsz, seq, LANES), F32)
    return pl.pallas_call(
        _rope_table_kernel,
        grid=(bsz,),
        in_specs=[
            pl.BlockSpec((1, seq, 1), lambda b: (b, 0, 0)),
            pl.BlockSpec((1, LANES), lambda b: (0, 0)),
        ],
        out_specs=[pl.BlockSpec((1, seq, LANES), lambda b: (b, 0, 0))] * 2,
        out_shape=[tab, tab],
        compiler_params=_cparams("parallel"),
        name="rope_tables",
    )(positions.reshape(bsz, seq, 1), jnp.asarray(freq))


def _rope(t, cos, sin_signed):
    lane = lax.broadcasted_iota(jnp.int32, t.shape, 1)
    partner = jnp.where(lane < ROT_HALF,
                        pltpu.roll(t, LANES - ROT_HALF, 1),
                        pltpu.roll(t, ROT_HALF, 1))
    return t * cos + partner * sin_signed


def _in_proj_kernel(x_ref, g_ref, sh_ref, sc_ref, w_ref, wf_ref, qkv_ref, fg_ref, h_scr):
    @pl.when(pl.program_id(1) == 0)
    def _():
        h = _rms_norm(x_ref[...], g_ref[...]) * (1.0 + sc_ref[0]) + sh_ref[0]
        hb = _mxu(h)
        h_scr[...] = hb
        fg_ref[...] = _dot(hb, wf_ref[...])

    qkv_ref[...] = _dot(h_scr[...], w_ref[...]).astype(qkv_ref.dtype)


def _in_proj(x2, g, shift, scale, w_qkv, w_f, seq):
    t, d = x2.shape
    n = w_qkv.shape[1]
    per_b = seq // TM_IN
    bmap = lambda i, j: (i // per_b, 0, 0)
    return pl.pallas_call(
        _in_proj_kernel,
        grid=(t // TM_IN, n // TN_IN),
        in_specs=[
            pl.BlockSpec((TM_IN, d), lambda i, j: (i, 0)),
            pl.BlockSpec((1, d), lambda i, j: (0, 0)),
            pl.BlockSpec((1, 1, d), bmap),
            pl.BlockSpec((1, 1, d), bmap),
            pl.BlockSpec((d, TN_IN), lambda i, j: (0, j)),
            pl.BlockSpec((d, LANES), lambda i, j: (0, 0)),
        ],
        out_specs=[
            pl.BlockSpec((TM_IN, TN_IN), lambda i, j: (i, j)),
            pl.BlockSpec((TM_IN, LANES), lambda i, j: (i, 0)),
        ],
        out_shape=[
            jax.ShapeDtypeStruct((t, n), MXU_DTYPE),
            jax.ShapeDtypeStruct((t, LANES), F32),
        ],
        scratch_shapes=[pltpu.VMEM((TM_IN, d), MXU_DTYPE)],
        compiler_params=_cparams("parallel", "arbitrary"),
        name="in_proj",
    )(x2, g, shift, scale, w_qkv, w_f)


def _forget_kernel(fg_ref, b_ref, f_ref):
    z = fg_ref[0] + b_ref[...]
    log_f = jnp.minimum(z, 0.0) - jnp.log(1.0 + jnp.exp(-jnp.abs(z)))
    cum = log_f.T[0:8]
    seq = cum.shape[1]
    lane = lax.broadcasted_iota(jnp.int32, cum.shape, 1)
    shift = 1
    while shift < seq:
        cum = cum + jnp.where(lane >= shift, pltpu.roll(cum, shift, 1), 0.0)
        shift *= 2
    f_ref[0] = cum


def _forget_cumsum(fg3, b_fgate):
    bsz, seq, _ = fg3.shape
    b_pad = jnp.pad(b_fgate, (0, LANES - b_fgate.shape[0])).reshape(1, LANES)
    return pl.pallas_call(
        _forget_kernel,
        grid=(bsz,),
        in_specs=[
            pl.BlockSpec((1, seq, LANES), lambda b: (b, 0, 0)),
            pl.BlockSpec((1, LANES), lambda b: (0, 0)),
        ],
        out_specs=pl.BlockSpec((1, 8, seq), lambda b: (b, 0, 0)),
        out_shape=jax.ShapeDtypeStruct((bsz, 8, seq), F32),
        compiler_params=_cparams("parallel"),
        name="forget_cumsum",
    )(fg3, b_pad)


LOG2E = 1.4426950408889634
Q_SCALE = ATTN_SCALE * LOG2E


def _softmax_pv(pieces, v):
    m = pieces[0]
    for s in pieces[1:]:
        m = jnp.maximum(m, s)
    m = jnp.max(m, axis=-1, keepdims=True)
    ps = [jnp.exp2(s - m) for s in pieces]
    l = ps[0]
    for p in ps[1:]:
        l = l + p
    l = jnp.sum(l, axis=-1, keepdims=True)
    p_all = _mxu(ps[0]) if len(ps) == 1 else jnp.concatenate([_mxu(p) for p in ps], axis=1)
    return _dot(p_all, v) / l


def _causal_mask(s):
    row = lax.broadcasted_iota(jnp.int32, s.shape, 0)
    col = lax.broadcasted_iota(jnp.int32, s.shape, 1)
    return jnp.where(col <= row, s, NEG_INF)


def _qkv_specs(seq, head0):
    return [
        pl.BlockSpec((1, seq, HEAD_DIM), lambda b, h: (b, 0, head0 + h)),
        pl.BlockSpec((1, seq, HEAD_DIM), lambda b, h: (b, 0, N_HEADS + head0 + h)),
        pl.BlockSpec((1, seq, HEAD_DIM), lambda b, h: (b, 0, 2 * N_HEADS + head0 + h)),
    ]


def _rope_k_into(k_ref, cos_ref, sin_ref, k_scr, blk):
    seq = k_scr.shape[0]
    for j in range(seq // blk):
        rows = slice(j * blk, (j + 1) * blk)
        k_scr[rows] = _mxu(_rope(k_ref[0, rows].astype(F32), cos_ref[0, rows], sin_ref[0, rows]))


def _dilated_bias(tq):
    n_off = 2048 // tq
    off = jnp.arange(n_off, dtype=jnp.int32)[:, None, None] * tq
    delta = off + jnp.arange(tq, dtype=jnp.int32)[None, :, None] - jnp.arange(tq, dtype=jnp.int32)[None, None, :]
    count = jnp.zeros(delta.shape, F32)
    for window, dil in DILATED_BRANCHES:
        count = count + ((delta >= 0) & (delta % dil == 0) & (delta <= window)).astype(F32)
    return jnp.where(count > 0, jnp.log2(jnp.maximum(count, 1.0)), NEG_INF)


def _dilated_kernel(q_ref, k_ref, v_ref, cos_ref, sin_ref, bias_ref, o_ref, k_scr):
    tq = TQ_ATT
    seq = k_scr.shape[0]
    _rope_k_into(k_ref, cos_ref, sin_ref, k_scr, tq)

    def logits(qi):
        rows = slice(qi * tq, (qi + 1) * tq)
        q = _rope(q_ref[0, rows].astype(F32), cos_ref[0, rows], sin_ref[0, rows])
        return _dot_nt(_mxu(q * Q_SCALE), k_scr[:(qi + 1) * tq])

    n_q = seq // tq
    s_next = logits(0)
    for qi in range(n_q):
        s = s_next
        if qi + 1 < n_q:
            s_next = logits(qi + 1)
        pieces = [s[:, kj * tq:(kj + 1) * tq] + bias_ref[qi - kj] for kj in range(qi + 1)]
        o = _softmax_pv(pieces, v_ref[0, :(qi + 1) * tq])
        o_ref[0, qi * tq:(qi + 1) * tq] = o.astype(o_ref.dtype)


def _dilated_attention(qkv3, cos, sin):
    bsz, seq, _ = qkv3.shape
    assert seq == 2048, "dilated windows are laid out for a 2048-token sequence"
    bias = _dilated_bias(TQ_ATT)
    tab = pl.BlockSpec((1, seq, LANES), lambda b, h: (b, 0, 0))
    return pl.pallas_call(
        _dilated_kernel,
        grid=(bsz, N_HEADS_DIL),
        in_specs=_qkv_specs(seq, 0) + [tab, tab,
                                       pl.BlockSpec(bias.shape, lambda b, h: (0, 0, 0))],
        out_specs=pl.BlockSpec((1, seq, HEAD_DIM), lambda b, h: (b, 0, h)),
        out_shape=jax.ShapeDtypeStruct((bsz, seq, N_HEADS_DIL * HEAD_DIM), MXU_DTYPE),
        scratch_shapes=[pltpu.VMEM((seq, HEAD_DIM), MXU_DTYPE)],
        compiler_params=_cparams("parallel", "parallel"),
        name="dilated_attention",
    )(qkv3, qkv3, qkv3, cos, sin, bias)


def _moba_kernel(q_ref, k_ref, v_ref, cos_ref, sin_ref, o_ref, k_scr, km_scr):
    blk = MOBA_BLOCK
    seq = k_scr.shape[0]
    n_blk = seq // blk
    km_scr[...] = jnp.zeros(km_scr.shape, F32)
    for j in range(n_blk):
        rows = slice(j * blk, (j + 1) * blk)
        kr = _rope(k_ref[0, rows].astype(F32), cos_ref[0, rows], sin_ref[0, rows])
        k_scr[rows] = _mxu(kr)
        km_scr[j:j + 1, :] = jnp.mean(kr, axis=0, keepdims=True)
    km_hi, km_lo = _split_hi_lo(km_scr[...])

    def logits(qi):
        rows = slice(qi * blk, (qi + 1) * blk)
        qf = _rope(q_ref[0, rows].astype(F32), cos_ref[0, rows], sin_ref[0, rows])
        s = _dot_nt(_mxu(qf * Q_SCALE), k_scr[:(qi + 1) * blk])
        if qi == 0:
            return s, None
        q_hi, q_lo = _split_hi_lo(qf)
        gate = _dot_nt(q_hi, km_hi) + _dot_nt(q_lo, km_hi) + _dot_nt(q_hi, km_lo)
        lane = lax.broadcasted_iota(jnp.int32, gate.shape, 1)
        past = lane < qi
        gate = jnp.where(past, gate, NEG_INF)
        rank = jnp.zeros(gate.shape, F32)
        for jp in range(qi):
            g_jp = gate[:, jp:jp + 1]
            ahead = (g_jp > gate) | ((g_jp == gate) & (lane > jp))
            rank = rank + jnp.where(ahead, 1.0, 0.0)
        return s, jnp.where(past & (rank < MOBA_TOPK), 0.0, NEG_INF)

    nxt = logits(0)
    for qi in range(n_blk):
        s, sel_bias = nxt
        if qi + 1 < n_blk:
            nxt = logits(qi + 1)
        pieces = [s[:, kj * blk:(kj + 1) * blk] + sel_bias[:, kj:kj + 1] for kj in range(qi)]
        pieces.append(_causal_mask(s[:, qi * blk:(qi + 1) * blk]))
        o = _softmax_pv(pieces, v_ref[0, :(qi + 1) * blk])
        o_ref[0, qi * blk:(qi + 1) * blk] = o.astype(o_ref.dtype)


def _moba_attention(qkv3, cos, sin):
    bsz, seq, _ = qkv3.shape
    assert seq % MOBA_BLOCK == 0 and seq // MOBA_BLOCK <= LANES
    tab = pl.BlockSpec((1, seq, LANES), lambda b, h: (b, 0, 0))
    return pl.pallas_call(
        _moba_kernel,
        grid=(bsz, N_HEADS_MOBA),
        in_specs=_qkv_specs(seq, N_HEADS_DIL) + [tab, tab],
        out_specs=pl.BlockSpec((1, seq, HEAD_DIM), lambda b, h: (b, 0, h)),
        out_shape=jax.ShapeDtypeStruct((bsz, seq, N_HEADS_MOBA * HEAD_DIM), MXU_DTYPE),
        scratch_shapes=[pltpu.VMEM((seq, HEAD_DIM), MXU_DTYPE),
                        pltpu.VMEM((LANES, HEAD_DIM), F32)],
        compiler_params=_cparams("parallel", "parallel"),
        name="moba_attention",
    )(qkv3, qkv3, qkv3, cos, sin)


def _fox_kernel(q_ref, k_ref, v_ref, f_ref, o_ref):
    tq = TQ_ATT
    seq = q_ref.shape[1]
    def logits(qi):
        q = _mxu(q_ref[0, qi * tq:(qi + 1) * tq].astype(F32) * Q_SCALE)
        return _dot_nt(q, k_ref[0, :(qi + 1) * tq])

    n_q = seq // tq
    s_next = logits(0)
    for qi in range(n_q):
        past = slice(0, (qi + 1) * tq)
        s = s_next - f_ref[0, 0, :, past] * LOG2E
        if qi + 1 < n_q:
            s_next = logits(qi + 1)
        pieces = [s[:, kj * tq:(kj + 1) * tq] for kj in range(qi)]
        pieces.append(_causal_mask(s[:, qi * tq:(qi + 1) * tq]))
        o_ref[0, qi * tq:(qi + 1) * tq] = _softmax_pv(pieces, v_ref[0, past]).astype(o_ref.dtype)


def _fox_attention(qkv3, cum_f):
    bsz, seq, _ = qkv3.shape
    f4 = cum_f.reshape(bsz, cum_f.shape[1], 1, seq)
    return pl.pallas_call(
        _fox_kernel,
        grid=(bsz, N_HEADS_FOX),
        in_specs=_qkv_specs(seq, N_HEADS_DIL + N_HEADS_MOBA)
        + [pl.BlockSpec((1, 1, 1, seq), lambda b, h: (b, h, 0, 0))],
        out_specs=pl.BlockSpec((1, seq, HEAD_DIM), lambda b, h: (b, 0, h)),
        out_shape=jax.ShapeDtypeStruct((bsz, seq, N_HEADS_FOX * HEAD_DIM), MXU_DTYPE),
        compiler_params=_cparams("parallel", "parallel"),
        name="fox_attention",
    )(qkv3, qkv3, qkv3, f4)


def _out_proj_kernel(od_ref, om_ref, of_ref, wd_ref, wm_ref, wf_ref, x_ref, g_ref, gate_ref, o_ref):
    y = _dot(od_ref[...], wd_ref[...]) + _dot(om_ref[...], wm_ref[...]) + _dot(of_ref[...], wf_ref[...])
    o_ref[...] = x_ref[...] + gate_ref[0] * _rms_norm(y, g_ref[...])


def _out_proj(o_dil, o_moba, o_fox, w_out, x2, g, gate, seq):
    t, d = x2.shape
    nd, nm = o_dil.shape[1], o_moba.shape[1]
    w_d, w_m, w_f = w_out[:nd], w_out[nd:nd + nm], w_out[nd + nm:]
    per_b = seq // TM_OUT
    row = lambda i: (i, 0)
    fixed = lambda i: (0, 0)
    return pl.pallas_call(
        _out_proj_kernel,
        grid=(t // TM_OUT,),
        in_specs=[
            pl.BlockSpec((TM_OUT, nd), row),
            pl.BlockSpec((TM_OUT, nm), row),
            pl.BlockSpec((TM_OUT, o_fox.shape[1]), row),
            pl.BlockSpec(w_d.shape, fixed),
            pl.BlockSpec(w_m.shape, fixed),
            pl.BlockSpec(w_f.shape, fixed),
            pl.BlockSpec((TM_OUT, d), row),
            pl.BlockSpec((1, d), fixed),
            pl.BlockSpec((1, 1, d), lambda i: (i // per_b, 0, 0)),
        ],
        out_specs=pl.BlockSpec((TM_OUT, d), row),
        out_shape=jax.ShapeDtypeStruct((t, d), F32),
        compiler_params=_cparams("parallel"),
        name="out_proj",
    )(o_dil, o_moba, o_fox, w_d, w_m, w_f, x2, g, gate)


def _swiglu_tile(h, wg, wu, wd):
    g = _dot(h, wg)
    u = _dot(h, wu)
    return _dot(_mxu(g * _sigmoid(g) * u), wd)


def _ffn_kernel(x_ref, gpre_ref, sh_ref, sc_ref, wg_ref, wu_ref, wd_ref, gpost_ref, gate_ref,
                o_ref, h_scr, acc_scr):
    j = pl.program_id(1)

    @pl.when(j == 0)
    def _():
        h = _rms_norm(x_ref[...], gpre_ref[...]) * (1.0 + sc_ref[0]) + sh_ref[0]
        h_scr[...] = _mxu(h)
        acc_scr[...] = jnp.zeros(acc_scr.shape, F32)

    acc_scr[...] += _swiglu_tile(h_scr[...], wg_ref[...], wu_ref[...], wd_ref[...])

    @pl.when(j == pl.num_programs(1) - 1)
    def _():
        o_ref[...] = x_ref[...] + gate_ref[0] * _rms_norm(acc_scr[...], gpost_ref[...])


def _dense_ffn(x2, g_pre, shift, scale, w_gate, w_up, w_down, g_post, gate, seq):
    t, d = x2.shape
    d_ff = w_gate.shape[1]
    per_b = seq // TM_FFN
    row = lambda i, j: (i, 0)
    fixed = lambda i, j: (0, 0)
    bmap = lambda i, j: (i // per_b, 0, 0)
    return pl.pallas_call(
        _ffn_kernel,
        grid=(t // TM_FFN, d_ff // TF_FFN),
        in_specs=[
            pl.BlockSpec((TM_FFN, d), row),
            pl.BlockSpec((1, d), fixed),
            pl.BlockSpec((1, 1, d), bmap),
            pl.BlockSpec((1, 1, d), bmap),
            pl.BlockSpec((d, TF_FFN), lambda i, j: (0, j)),
            pl.BlockSpec((d, TF_FFN), lambda i, j: (0, j)),
            pl.BlockSpec((TF_FFN, d), lambda i, j: (j, 0)),
            pl.BlockSpec((1, d), fixed),
            pl.BlockSpec((1, 1, d), bmap),
        ],
        out_specs=pl.BlockSpec((TM_FFN, d), row),
        out_shape=jax.ShapeDtypeStruct((t, d), F32),
        scratch_shapes=[pltpu.VMEM((TM_FFN, d), MXU_DTYPE), pltpu.VMEM((TM_FFN, d), F32)],
        compiler_params=_cparams("parallel", "arbitrary"),
        name="dense_ffn",
    )(x2, g_pre, shift, scale, w_gate, w_up, w_down, g_post, gate)


def _router_kernel(x_ref, g_ref, sh_ref, sc_ref, whi_ref, wlo_ref, br_ref, gw_ref, gi_ref):
    h = _rms_norm(x_ref[...], g_ref[...]) * (1.0 + sc_ref[0]) + sh_ref[0]
    h_hi, h_lo = _split_hi_lo(h)
    logits = (_dot(h_hi, whi_ref[...]) + _dot(h_lo, whi_ref[...]) + _dot(h_hi, wlo_ref[...])
              + br_ref[...])
    lane = lax.broadcasted_iota(jnp.int32, logits.shape, 1)
    logits = jnp.where(lane < N_EXPERTS, logits, NEG_INF)
    m1 = jnp.max(logits, axis=-1, keepdims=True)
    i1 = jnp.min(jnp.where(logits == m1, lane, LANES), axis=-1, keepdims=True)
    rest = jnp.where(lane == i1, NEG_INF, logits)
    m2 = jnp.max(rest, axis=-1, keepdims=True)
    i2 = jnp.min(jnp.where(rest == m2, lane, LANES), axis=-1, keepdims=True)
    e2 = jnp.exp(m2 - m1)
    p1 = 1.0 / (1.0 + e2)
    p2 = e2 * p1
    gw_ref[...] = jnp.where(lane == 0, p1, jnp.where(lane == 1, p2, 0.0))
    gi_ref[...] = jnp.where(lane == 0, i1, jnp.where(lane == 1, i2, 0))


def _router(x2, g_pre, shift, scale, w_router, b_router, seq):
    t, d = x2.shape
    tm = TM_OUT
    w_pad = jnp.pad(w_router, ((0, 0), (0, LANES - N_EXPERTS)))
    w_hi = w_pad.astype(MXU_DTYPE)
    w_lo = (w_pad - w_hi.astype(F32)).astype(MXU_DTYPE)
    b_pad = jnp.pad(b_router, (0, LANES - N_EXPERTS)).reshape(1, LANES)
    per_b = seq // tm
    row = lambda i: (i, 0)
    fixed = lambda i: (0, 0)
    bmap = lambda i: (i // per_b, 0, 0)
    return pl.pallas_call(
        _router_kernel,
        grid=(t // tm,),
        in_specs=[
            pl.BlockSpec((tm, d), row),
            pl.BlockSpec((1, d), fixed),
            pl.BlockSpec((1, 1, d), bmap),
            pl.BlockSpec((1, 1, d), bmap),
            pl.BlockSpec((d, LANES), fixed),
            pl.BlockSpec((d, LANES), fixed),
            pl.BlockSpec((1, LANES), fixed),
        ],
        out_specs=[pl.BlockSpec((tm, LANES), row), pl.BlockSpec((tm, LANES), row)],
        out_shape=[jax.ShapeDtypeStruct((t, LANES), F32), jax.ShapeDtypeStruct((t, LANES), jnp.int32)],
        compiler_params=_cparams("parallel"),
        name="router",
    )(x2, g_pre, shift, scale, w_hi, w_lo, b_pad)


def _routing_tables(top_idx, n_tiles):
    flat_e = top_idx.T.reshape(-1)
    experts = jnp.arange(N_EXPERTS, dtype=jnp.int32)
    onehot = (flat_e[:, None] == experts[None, :]).astype(jnp.int32)
    csum = jnp.cumsum(onehot, axis=0)
    rank = jnp.sum(onehot * csum, axis=1) - 1
    counts = csum[-1]
    padded = ((counts + TM_MOE - 1) // TM_MOE) * TM_MOE
    ends = jnp.cumsum(padded)
    starts = ends - padded
    pos = (jnp.sum(onehot * starts[None, :], axis=1) + rank).astype(jnp.int32)
    tile_start = jnp.arange(n_tiles, dtype=jnp.int32) * TM_MOE
    tile_expert = jnp.minimum(jnp.sum((tile_start[:, None] >= ends[None, :]).astype(jnp.int32), axis=1),
                              N_EXPERTS - 1).astype(jnp.int32)
    n_active = (ends[-1] // TM_MOE).astype(jnp.int32).reshape(1)
    pad_start = (starts + counts).astype(jnp.int32)
    pad_len = (padded - counts).astype(jnp.int32)
    tiles = jnp.arange(n_tiles, dtype=jnp.int32)
    first_tile = (starts // TM_MOE).astype(jnp.int32)
    has_rows = (counts > 0).astype(jnp.int32)
    active = tiles < n_active[0]
    first_of_tile = jnp.sum(jnp.where(tile_expert[:, None] == experts[None, :], first_tile[None, :], 0), axis=1)
    nth_of_expert = tiles - first_of_tile
    is_first = active & (nth_of_expert < FIRST_TILES)
    first_block = tile_expert * FIRST_TILES + jnp.clip(nth_of_expert, 0, FIRST_TILES - 1)
    frozen = is_first | ~active
    prev_live = jnp.max(jnp.where((tiles[None, :] <= tiles[:, None]) & ~frozen[None, :], tiles[None, :], -1),
                        axis=1)
    weight_expert = jnp.sum(jnp.where(tiles[None, :] == jnp.maximum(prev_live, 0)[:, None],
                                      tile_expert[None, :], 0), axis=1)
    return (pos, tile_expert, n_active, pad_start, pad_len, first_tile, has_rows,
            is_first.astype(jnp.int32), first_block.astype(jnp.int32), frozen.astype(jnp.int32),
            weight_expert.astype(jnp.int32))


def _row_copy(src, src_row, dst, dst_row, sem):
    return pltpu.make_async_copy(src.at[pl.ds(src_row, 1)], dst.at[pl.ds(dst_row, 1)], sem)


def _for_row_groups(n_rows, body):
    def trip(g, c):
        r0 = pl.multiple_of(g * DMA_UNROLL, DMA_UNROLL)
        for k in range(DMA_UNROLL):
            body(r0 + k)
        return c

    lax.fori_loop(0, n_rows // DMA_UNROLL, trip, 0)


def _dispatch_kernel(pos_ref, pstart_ref, plen_ref, nact_ref, x_ref, g_ref, sh_ref, sc_ref, hs_hbm,
                     hbuf, sem, zsem):
    i = pl.program_id(0)
    n = pl.num_programs(0)
    tm = x_ref.shape[0]
    t_total = n * tm
    slot = i % 2

    def wait_slot(s):
        for _ in range(TOP_K):
            pltpu.make_async_copy(hbuf.at[s], hs_hbm.at[pl.ds(0, tm)], sem.at[s]).wait()

    @pl.when(i >= 2)
    def _():
        wait_slot(slot)

    hbuf[slot] = _rms_norm(x_ref[...], g_ref[...]) * (1.0 + sc_ref[0]) + sh_ref[0]

    def send(r):
        for s in range(TOP_K):
            _row_copy(hbuf.at[slot], r, hs_hbm, pos_ref[s * t_total + i * tm + r], sem.at[slot]).start()

    _for_row_groups(tm, send)

    @pl.when(i == n - 1)
    def _():
        wait_slot(slot)

        @pl.when(n >= 2)
        def _():
            wait_slot(1 - slot)

        zeros = hbuf.at[0]
        zeros[...] = jnp.zeros(zeros.shape, F32)
        for e in range(N_EXPERTS):
            def zero_row(k, c):
                _row_copy(zeros, 0, hs_hbm, pstart_ref[e] + k, zsem).start()
                return c

            def zero_row_wait(k, c):
                _row_copy(zeros, 0, hs_hbm, 0, zsem).wait()
                return c

            lax.fori_loop(0, plen_ref[e], zero_row, 0)
            lax.fori_loop(0, plen_ref[e], zero_row_wait, 0)

        tail0 = nact_ref[0] * TM_MOE
        n_tail = (hs_hbm.shape[0] - tail0) // tm

        def zero_block(k, c):
            pltpu.make_async_copy(zeros, hs_hbm.at[pl.ds(tail0 + k * tm, tm)], zsem).start()
            return c

        def zero_block_wait(k, c):
            pltpu.make_async_copy(zeros, hs_hbm.at[pl.ds(0, tm)], zsem).wait()
            return c

        lax.fori_loop(0, n_tail, zero_block, 0)
        lax.fori_loop(0, n_tail, zero_block_wait, 0)


def _dispatch(x2, g_pre, shift, scale, pos, pad_start, pad_len, n_active, n_tiles, seq):
    t, d = x2.shape
    tm = TM_DSP
    assert TM_MOE % tm == 0
    per_b = seq // tm
    return pl.pallas_call(
        _dispatch_kernel,
        grid_spec=pltpu.PrefetchScalarGridSpec(
            num_scalar_prefetch=4,
            grid=(t // tm,),
            in_specs=[
                pl.BlockSpec((tm, d), lambda i, *_: (i, 0)),
                pl.BlockSpec((1, d), lambda i, *_: (0, 0)),
                pl.BlockSpec((1, 1, d), lambda i, *_: (i // per_b, 0, 0)),
                pl.BlockSpec((1, 1, d), lambda i, *_: (i // per_b, 0, 0)),
            ],
            out_specs=pl.BlockSpec(memory_space=pl.ANY),
            scratch_shapes=[pltpu.VMEM((2, tm, d), F32),
                            pltpu.SemaphoreType.DMA((2,)), pltpu.SemaphoreType.DMA(())],
        ),
        out_shape=jax.ShapeDtypeStruct((n_tiles * TM_MOE, d), F32),
        compiler_params=_cparams("arbitrary"),
        name="expert_dispatch",
    )(pos, pad_start, pad_len, n_active, x2, g_pre, shift, scale)


def _expert_first_kernel(ft_ref, has_ref, *refs):
    h_refs = refs[:FIRST_TILES]
    wg_ref, wu_ref, wd_ref, y_ref, qg_ref, qu_ref, qd_ref, hb_scr = refs[FIRST_TILES:]
    e, j = pl.program_id(0), pl.program_id(1)
    tm = h_refs[0].shape[0]
    wg, wu, wd = _mxu(wg_ref[0]), _mxu(wu_ref[0]), _mxu(wd_ref[0])
    qg_ref[0] = wg
    qu_ref[0] = wu
    qd_ref[0] = wd

    @pl.when(j == 0)
    def _():
        y_ref[...] = jnp.zeros(y_ref.shape, F32)
        for k, h_ref in enumerate(h_refs):
            hb_scr[k * tm:(k + 1) * tm] = _mxu(h_ref[...])

    @pl.when(has_ref[e] > 0)
    def _():
        y_ref[...] += _swiglu_tile(hb_scr[...], wg, wu, wd)


def _expert_first(hs, first_tile, has_rows, w_gate, w_up, w_down):
    n_rows, d = hs.shape
    n_exp, _, d_ff = w_gate.shape
    tf = TF_FIRST
    last_tile = n_rows // TM_MOE - 1
    col = lambda e, j, ft, has: (e, 0, j)
    rowb = lambda e, j, ft, has: (e, j, 0)
    cast = lambda a: jax.ShapeDtypeStruct(a.shape, MXU_DTYPE)

    def h_spec(k):
        return pl.BlockSpec((TM_MOE, d), lambda e, j, ft, has: (jnp.minimum(ft[e] + k, last_tile), 0),
                            pipeline_mode=pl.Buffered(1))

    return pl.pallas_call(
        _expert_first_kernel,
        grid_spec=pltpu.PrefetchScalarGridSpec(
            num_scalar_prefetch=2,
            grid=(n_exp, d_ff // tf),
            in_specs=[h_spec(k) for k in range(FIRST_TILES)] + [
                pl.BlockSpec((1, d, tf), col),
                pl.BlockSpec((1, d, tf), col),
                pl.BlockSpec((1, tf, d), rowb),
            ],
            out_specs=[
                pl.BlockSpec((FIRST_TILES * TM_MOE, d), lambda e, j, ft, has: (e, 0)),
                pl.BlockSpec((1, d, tf), col),
                pl.BlockSpec((1, d, tf), col),
                pl.BlockSpec((1, tf, d), rowb),
            ],
            scratch_shapes=[pltpu.VMEM((FIRST_TILES * TM_MOE, d), MXU_DTYPE)],
        ),
        out_shape=[jax.ShapeDtypeStruct((n_exp * FIRST_TILES * TM_MOE, d), F32),
                   cast(w_gate), cast(w_up), cast(w_down)],
        compiler_params=_cparams("arbitrary", "arbitrary"),
        name="expert_first",
    )(first_tile, has_rows, *([hs] * FIRST_TILES), w_gate, w_up, w_down)


def _expert_kernel(te_ref, nact_ref, first_ref, fblk_ref, frozen_ref, we_ref, h_ref, yf_ref,
                   wg_ref, wu_ref, wd_ref, o_ref, hb_scr):
    i, j = pl.program_id(0), pl.program_id(1)
    compute = frozen_ref[i] == 0

    @pl.when(j == 0)
    def _():
        o_ref[...] = jnp.zeros(o_ref.shape, F32)

    @pl.when(compute & (j == 0))
    def _():
        hb_scr[...] = _mxu(h_ref[...])

    @pl.when(compute)
    def _():
        o_ref[...] += _swiglu_tile(hb_scr[...], wg_ref[0], wu_ref[0], wd_ref[0])

    @pl.when((j == pl.num_programs(1) - 1) & (first_ref[i] != 0))
    def _():
        o_ref[...] = yf_ref[...]


def _expert_ffn(hs, y_first, tile_expert, n_active, is_first, first_block, frozen, weight_expert,
                w_gate, w_up, w_down):
    n_rows, d = hs.shape
    d_ff = w_gate.shape[2]
    n_tiles = n_rows // TM_MOE
    n_ff = d_ff // TF_FFN

    def ff_step(i, j, frozen):
        return jnp.where(frozen[i] != 0, n_ff - 1, j)

    def row_tile(i, nact):
        return jnp.minimum(i, nact[0] - 1)

    return pl.pallas_call(
        _expert_kernel,
        grid_spec=pltpu.PrefetchScalarGridSpec(
            num_scalar_prefetch=6,
            grid=(n_tiles, n_ff),
            in_specs=[
                pl.BlockSpec((TM_MOE, d), lambda i, j, te, nact, fi, fb, fr, we: (row_tile(i, nact), 0)),
                pl.BlockSpec((TM_MOE, d), lambda i, j, te, nact, fi, fb, fr, we: (fb[i], 0),
                             pipeline_mode=pl.Buffered(1)),
                pl.BlockSpec((1, d, TF_FFN), lambda i, j, te, nact, fi, fb, fr, we: (we[i], 0, ff_step(i, j, fr))),
                pl.BlockSpec((1, d, TF_FFN), lambda i, j, te, nact, fi, fb, fr, we: (we[i], 0, ff_step(i, j, fr))),
                pl.BlockSpec((1, TF_FFN, d), lambda i, j, te, nact, fi, fb, fr, we: (we[i], ff_step(i, j, fr), 0)),
            ],
            out_specs=pl.BlockSpec((TM_MOE, d), lambda i, j, te, nact, fi, fb, fr, we: (i, 0)),
            scratch_shapes=[pltpu.VMEM((TM_MOE, d), MXU_DTYPE)],
        ),
        out_shape=jax.ShapeDtypeStruct((n_rows, d), F32),
        compiler_params=_cparams("arbitrary", "arbitrary"),
        name="expert_ffn",
    )(tile_expert, n_active, is_first, first_block, frozen, weight_expert, hs, y_first, w_gate, w_up, w_down)


def _combine_kernel(pos_ref, ys_hbm, gw_ref, x_ref, g_ref, gate_ref, o_ref, buf, sem):
    i = pl.program_id(0)
    n = pl.num_programs(0)
    tm = x_ref.shape[0]
    t_total = n * tm
    cur = i % 2

    def fetch(step, b):
        def one(r):
            for s in range(TOP_K):
                _row_copy(ys_hbm, pos_ref[s * t_total + step * tm + r], buf.at[b, s], r, sem.at[b]).start()

        _for_row_groups(tm, one)

    @pl.when(i == 0)
    def _():
        fetch(0, 0)

    @pl.when(i + 1 < n)
    def _():
        fetch(i + 1, 1 - cur)

    for s in range(TOP_K):
        pltpu.make_async_copy(ys_hbm.at[pl.ds(0, tm)], buf.at[cur, s], sem.at[cur]).wait()
    gw = gw_ref[...]
    y = gw[:, 0:1] * buf[cur, 0]
    for s in range(1, TOP_K):
        y = y + gw[:, s:s + 1] * buf[cur, s]
    o_ref[...] = x_ref[...] + gate_ref[0] * _rms_norm(y, g_ref[...])


def _combine(ys, pos, gate_w, x2, g_post, gate, seq):
    t, d = x2.shape
    tm = TM_CMB
    per_b = seq // tm
    return pl.pallas_call(
        _combine_kernel,
        grid_spec=pltpu.PrefetchScalarGridSpec(
            num_scalar_prefetch=1,
            grid=(t // tm,),
            in_specs=[
                pl.BlockSpec(memory_space=pl.ANY),
                pl.BlockSpec((tm, LANES), lambda i, pos: (i, 0)),
                pl.BlockSpec((tm, d), lambda i, pos: (i, 0)),
                pl.BlockSpec((1, d), lambda i, pos: (0, 0)),
                pl.BlockSpec((1, 1, d), lambda i, pos: (i // per_b, 0, 0)),
            ],
            out_specs=pl.BlockSpec((tm, d), lambda i, pos: (i, 0)),
            scratch_shapes=[pltpu.VMEM((2, TOP_K, tm, d), F32), pltpu.SemaphoreType.DMA((2,))],
        ),
        out_shape=jax.ShapeDtypeStruct((t, d), F32),
        compiler_params=_cparams("arbitrary"),
        name="expert_combine",
    )(pos, ys, gate_w, x2, g_post, gate)


def _moe_ffn(x2, g_pre, shift, scale, w_router, b_router, w_gate, w_up, w_down, g_post, gate, seq):
    t = x2.shape[0]
    n_tiles = (TOP_K * t) // TM_MOE + N_EXPERTS
    gate_w, top = _router(x2, g_pre, shift, scale, w_router, b_router, seq)
    (pos, tile_expert, n_active, pad_start, pad_len, first_tile, has_rows, is_first, first_block, frozen,
     weight_expert) = _routing_tables(top[:, :TOP_K], n_tiles)
    hs = _dispatch(x2, g_pre, shift, scale, pos, pad_start, pad_len, n_active, n_tiles, seq)
    y_first, q_gate, q_up, q_down = _expert_first(hs, first_tile, has_rows, w_gate, w_up, w_down)
    ys = _expert_ffn(hs, y_first, tile_expert, n_active, is_first, first_block, frozen, weight_expert,
                     q_gate, q_up, q_down)
    return _combine(ys, pos, gate_w, x2, g_post, gate, seq)


def kernel(x, c, positions, ln_mix_pre, ln_mix_post, ln_ffn_pre, ln_ffn_post, w_mod, b_mod, w_in, b_fgate, w_out, w_ffn_gate, w_ffn_up, w_ffn_down, w_router, b_router, w_exp_gate, w_exp_up, w_exp_down):
    bsz, seq, d = x.shape
    depth = w_mod.shape[0]
    n_qkv = 3 * N_HEADS * HEAD_DIM
    x2 = x.reshape(bsz * seq, d)
    mod = _modulation(c, w_mod, b_mod).reshape(depth, bsz, N_MOD, 1, d)
    cos, sin = _rope_tables(positions)
    for layer in range(depth):
        sh_m, sc_m, g_m, sh_f, sc_f, g_f = (mod[layer, :, k] for k in range(N_MOD))
        row = lambda a: a[layer].reshape(1, d)
        w_qkv = _mxu(w_in[layer, :, :n_qkv])
        w_f = _mxu(jnp.pad(w_in[layer, :, n_qkv:], ((0, 0), (0, LANES - N_HEADS_FOX))))
        qkv, fg = _in_proj(x2, row(ln_mix_pre), sh_m, sc_m, w_qkv, w_f, seq)
        qkv3 = qkv.reshape(bsz, seq, n_qkv)
        cum_f = _forget_cumsum(fg.reshape(bsz, seq, LANES), b_fgate[layer])
        o_dil = _dilated_attention(qkv3, cos, sin).reshape(bsz * seq, -1)
        o_moba = _moba_attention(qkv3, cos, sin).reshape(bsz * seq, -1)
        o_fox = _fox_attention(qkv3, cum_f).reshape(bsz * seq, -1)
        x2 = _out_proj(o_dil, o_moba, o_fox, _mxu(w_out[layer]), x2, row(ln_mix_post), g_m, seq)
        j = layer // 2
        if layer % 2 == 0:
            x2 = _dense_ffn(x2, row(ln_ffn_pre), sh_f, sc_f, _mxu(w_ffn_gate[j]), _mxu(w_ffn_up[j]),
                            _mxu(w_ffn_down[j]), row(ln_ffn_post), g_f, seq)
        else:
            x2 = _moe_ffn(x2, row(ln_ffn_pre), sh_f, sc_f, w_router[j], b_router[j],
                          w_exp_gate[j], w_exp_up[j], w_exp_down[j], row(ln_ffn_post), g_f, seq)
    return x2.reshape(bsz, seq, d)
```

```python
import functools

import numpy as np
import jax
import jax.numpy as jnp
from jax import lax
from jax.experimental import pallas as pl
from jax.experimental.pallas import tpu as pltpu

F32 = jnp.float32
MXU_DTYPE = jnp.bfloat16

HEAD_DIM = 128
N_HEADS = 16
N_HEADS_DIL = 6
N_HEADS_MOBA = 4
N_HEADS_FOX = 6
DILATED_BRANCHES = ((128, 1), (512, 4), (2048, 16))
MOBA_BLOCK = 256
MOBA_TOPK = 3
ROPE_THETA = 500000.0
ROT_DIM = HEAD_DIM // 4
ROT_HALF = ROT_DIM // 2
N_EXPERTS = 8
TOP_K = 2
N_MOD = 6
RMS_EPS = 1e-6
NEG_INF = -1e30
ATTN_SCALE = HEAD_DIM ** -0.5
LANES = 128

VMEM_LIMIT_BYTES = 56 * 1024 * 1024

TM_IN = 1024
TN_IN = 1536
TM_OUT = 512
TM_FFN = 512
TF_FFN = 1024
TM_DSP = 256
TF_FIRST = 256
FIRST_TILES = 2
DMA_UNROLL = 8
TM_MOE = 512
TM_CMB = 256
TQ_FOX = 256
TQ_DIL = 128
TN_MOD = 1024


def _cparams(*sem):
    return pltpu.CompilerParams(dimension_semantics=sem, vmem_limit_bytes=VMEM_LIMIT_BYTES)


def _mxu(a):
    return a.astype(MXU_DTYPE)


def _dot(a, b):
    return jnp.dot(a, b, preferred_element_type=F32)


def _dot_nt(a, b):
    return lax.dot_general(a, b, (((1,), (1,)), ((), ())), preferred_element_type=F32)


def _split_hi_lo(a):
    hi = a.astype(MXU_DTYPE)
    lo = (a - hi.astype(F32)).astype(MXU_DTYPE)
    return hi, lo


def _rms_norm(x, g):
    ms = jnp.mean(x * x, axis=-1, keepdims=True)
    return x * lax.rsqrt(ms + RMS_EPS) * g


def _sigmoid(x):
    return 1.0 / (1.0 + jnp.exp(-x))


def _mod_kernel(c_ref, w_ref, b_ref, o_ref):
    c = c_ref[...]
    cond = c * _sigmoid(c)
    o_ref[0] = _dot(_mxu(cond), _mxu(w_ref[0])) + b_ref[0]


def _modulation(c, w_mod, b_mod):
    depth, d, n = w_mod.shape
    bsz = c.shape[0]
    rows = max(16, bsz)
    c_pad = jnp.pad(c, ((0, rows - bsz), (0, 0)))
    out = pl.pallas_call(
        _mod_kernel,
        grid=(depth, n // TN_MOD),
        in_specs=[
            pl.BlockSpec((rows, d), lambda l, j: (0, 0)),
            pl.BlockSpec((1, d, TN_MOD), lambda l, j: (l, 0, j)),
            pl.BlockSpec((1, 1, TN_MOD), lambda l, j: (l, 0, j)),
        ],
        out_specs=pl.BlockSpec((1, rows, TN_MOD), lambda l, j: (l, 0, j)),
        out_shape=jax.ShapeDtypeStruct((depth, rows, n), F32),
        compiler_params=_cparams("parallel", "parallel"),
        name="modulation",
    )(c_pad, w_mod, b_mod.reshape(depth, 1, n))
    return out[:, :bsz]


def _rope_table_kernel(pos_ref, freq_ref, cos_ref, sin_ref):
    ang = pos_ref[0].astype(F32) * freq_ref[...]
    lane = lax.broadcasted_iota(jnp.int32, ang.shape, 1)
    cos_ref[0] = jnp.where(lane < ROT_DIM, jnp.cos(ang), 1.0)
    sn = jnp.sin(ang)
    sin_ref[0] = jnp.where(lane < ROT_HALF, -sn, jnp.where(lane < ROT_DIM, sn, 0.0))


def _rope_tables(positions):
    bsz, seq = positions.shape
    inv_freq = ROPE_THETA ** (-np.arange(ROT_HALF, dtype=np.float32) / ROT_HALF)
    freq = np.zeros((1, LANES), np.float32)
    freq[0, :ROT_HALF] = inv_freq
    freq[0, ROT_HALF:ROT_DIM] = inv_freq
    tab = jax.ShapeDtypeStruct((bsz, seq, LANES), F32)
    return pl.pallas_call(
        _rope_table_kernel,
        grid=(bsz,),
        in_specs=[
            pl.BlockSpec((1, seq, 1), lambda b: (b, 0, 0)),
            pl.BlockSpec((1, LANES), lambda b: (0, 0)),
        ],
        out_specs=[pl.BlockSpec((1, seq, LANES), lambda b: (b, 0, 0))] * 2,
        out_shape=[tab, tab],
        compiler_params=_cparams("parallel"),
        name="rope_tables",
    )(positions.reshape(bsz, seq, 1), jnp.asarray(freq))


def _rope(t, cos, sin_signed):
    lane = lax.broadcasted_iota(jnp.int32, t.shape, 1)
    partner = jnp.where(lane < ROT_HALF,
                        pltpu.roll(t, LANES - ROT_HALF, 1),
                        pltpu.roll(t, ROT_HALF, 1))
    return t * cos + partner * sin_signed


def _in_proj_kernel(x_ref, g_ref, sh_ref, sc_ref, w_ref, wf_ref, qkv_ref, fg_ref, h_scr):
    @pl.when(pl.program_id(1) == 0)
    def _():
        h = _rms_norm(x_ref[...], g_ref[...]) * (1.0 + sc_ref[0]) + sh_ref[0]
        hb = _mxu(h)
        h_scr[...] = hb
        fg_ref[...] = _dot(hb, wf_ref[...])

    qkv_ref[...] = _dot(h_scr[...], w_ref[...]).astype(qkv_ref.dtype)


def _in_proj(x2, g, shift, scale, w_qkv, w_f, seq):
    t, d = x2.shape
    n = w_qkv.shape[1]
    per_b = seq // TM_IN
    bmap = lambda i, j: (i // per_b, 0, 0)
    return pl.pallas_call(
        _in_proj_kernel,
        grid=(t // TM_IN, n // TN_IN),
        in_specs=[
            pl.BlockSpec((TM_IN, d), lambda i, j: (i, 0)),
            pl.BlockSpec((1, d), lambda i, j: (0, 0)),
            pl.BlockSpec((1, 1, d), bmap),
            pl.BlockSpec((1, 1, d), bmap),
            pl.BlockSpec((d, TN_IN), lambda i, j: (0, j)),
            pl.BlockSpec((d, LANES), lambda i, j: (0, 0)),
        ],
        out_specs=[
            pl.BlockSpec((TM_IN, TN_IN), lambda i, j: (i, j)),
            pl.BlockSpec((TM_IN, LANES), lambda i, j: (i, 0)),
        ],
        out_shape=[
            jax.ShapeDtypeStruct((t, n), MXU_DTYPE),
            jax.ShapeDtypeStruct((t, LANES), F32),
        ],
        scratch_shapes=[pltpu.VMEM((TM_IN, d), MXU_DTYPE)],
        compiler_params=_cparams("parallel", "arbitrary"),
        name="in_proj",
    )(x2, g, shift, scale, w_qkv, w_f)


def _forget_kernel(fg_ref, b_ref, f_ref):
    z = fg_ref[0] + b_ref[...]
    log_f = jnp.minimum(z, 0.0) - jnp.log(1.0 + jnp.exp(-jnp.abs(z)))
    cum = log_f.T[0:8]
    seq = cum.shape[1]
    lane = lax.broadcasted_iota(jnp.int32, cum.shape, 1)
    shift = 1
    while shift < seq:
        cum = cum + jnp.where(lane >= shift, pltpu.roll(cum, shift, 1), 0.0)
        shift *= 2
    f_ref[0] = cum


def _forget_cumsum(fg3, b_fgate):
    bsz, seq, _ = fg3.shape
    b_pad = jnp.pad(b_fgate, (0, LANES - b_fgate.shape[0])).reshape(1, LANES)
    return pl.pallas_call(
        _forget_kernel,
        grid=(bsz,),
        in_specs=[
            pl.BlockSpec((1, seq, LANES), lambda b: (b, 0, 0)),
            pl.BlockSpec((1, LANES), lambda b: (0, 0)),
        ],
        out_specs=pl.BlockSpec((1, 8, seq), lambda b: (b, 0, 0)),
        out_shape=jax.ShapeDtypeStruct((bsz, 8, seq), F32),
        compiler_params=_cparams("parallel"),
        name="forget_cumsum",
    )(fg3, b_pad)


LOG2E = 1.4426950408889634
Q_SCALE = ATTN_SCALE * LOG2E


def _softmax_pv(pieces, v):
    m = pieces[0]
    for s in pieces[1:]:
        m = jnp.maximum(m, s)
    m = jnp.max(m, axis=-1, keepdims=True)
    ps = [jnp.exp2(s - m) for s in pieces]
    l = ps[0]
    for p in ps[1:]:
        l = l + p
    l = jnp.sum(l, axis=-1, keepdims=True)
    p_all = _mxu(ps[0]) if len(ps) == 1 else jnp.concatenate([_mxu(p) for p in ps], axis=1)
    return _dot(p_all, v) / l


def _causal_mask(s):
    row = lax.broadcasted_iota(jnp.int32, s.shape, 0)
    col = lax.broadcasted_iota(jnp.int32, s.shape, 1)
    return jnp.where(col <= row, s, NEG_INF)


def _qkv_specs(seq, head0):
    return [
        pl.BlockSpec((1, seq, HEAD_DIM), lambda b, h: (b, 0, head0 + h)),
        pl.BlockSpec((1, seq, HEAD_DIM), lambda b, h: (b, 0, N_HEADS + head0 + h)),
        pl.BlockSpec((1, seq, HEAD_DIM), lambda b, h: (b, 0, 2 * N_HEADS + head0 + h)),
    ]


def _rope_k_into(k_ref, cos_ref, sin_ref, k_scr, blk):
    seq = k_scr.shape[0]
    for j in range(seq // blk):
        rows = slice(j * blk, (j + 1) * blk)
        k_scr[rows] = _mxu(_rope(k_ref[0, rows].astype(F32), cos_ref[0, rows], sin_ref[0, rows]))


def _dilated_bias(tq):
    n_off = 2048 // tq
    off = jnp.arange(n_off, dtype=jnp.int32)[:, None, None] * tq
    delta = off + jnp.arange(tq, dtype=jnp.int32)[None, :, None] - jnp.arange(tq, dtype=jnp.int32)[None, None, :]
    count = jnp.zeros(delta.shape, F32)
    for window, dil in DILATED_BRANCHES:
        count = count + ((delta >= 0) & (delta % dil == 0) & (delta <= window)).astype(F32)
    return jnp.where(count > 0, jnp.log2(jnp.maximum(count, 1.0)), NEG_INF)


def _dilated_kernel(q_ref, k_ref, v_ref, cos_ref, sin_ref, bias_ref, o_ref, k_scr):
    tq = TQ_DIL
    seq = k_scr.shape[0]
    _rope_k_into(k_ref, cos_ref, sin_ref, k_scr, 512)

    def logits(qi):
        rows = slice(qi * tq, (qi + 1) * tq)
        q = _rope(q_ref[0, rows].astype(F32), cos_ref[0, rows], sin_ref[0, rows])
        return _dot_nt(_mxu(q * Q_SCALE), k_scr[:(qi + 1) * tq])

    n_q = seq // tq
    s_next = logits(0)
    for qi in range(n_q):
        s = s_next
        if qi + 1 < n_q:
            s_next = logits(qi + 1)
        pieces = [s[:, kj * tq:(kj + 1) * tq] + bias_ref[qi - kj] for kj in range(qi + 1)]
        o = _softmax_pv(pieces, v_ref[0, :(qi + 1) * tq])
        o_ref[0, qi * tq:(qi + 1) * tq] = o.astype(o_ref.dtype)


def _dilated_attention(qkv3, cos, sin):
    bsz, seq, _ = qkv3.shape
    assert seq == 2048, "dilated windows are laid out for a 2048-token sequence"
    bias = _dilated_bias(TQ_DIL)
    tab = pl.BlockSpec((1, seq, LANES), lambda b, h: (b, 0, 0))
    return pl.pallas_call(
        _dilated_kernel,
        grid=(bsz, N_HEADS_DIL),
        in_specs=_qkv_specs(seq, 0) + [tab, tab,
                                       pl.BlockSpec(bias.shape, lambda b, h: (0, 0, 0))],
        out_specs=pl.BlockSpec((1, seq, HEAD_DIM), lambda b, h: (b, 0, h)),
        out_shape=jax.ShapeDtypeStruct((bsz, seq, N_HEADS_DIL * HEAD_DIM), MXU_DTYPE),
        scratch_shapes=[pltpu.VMEM((seq, HEAD_DIM), MXU_DTYPE)],
        compiler_params=_cparams("parallel", "parallel"),
        name="dilated_attention",
    )(qkv3, qkv3, qkv3, cos, sin, bias)


def _moba_kernel(q_ref, k_ref, v_ref, cos_ref, sin_ref, o_ref, k_scr, km_scr):
    blk = MOBA_BLOCK
    seq = k_scr.shape[0]
    n_blk = seq // blk
    km_scr[...] = jnp.zeros(km_scr.shape, F32)
    for j in range(n_blk):
        rows = slice(j * blk, (j + 1) * blk)
        kr = _rope(k_ref[0, rows].astype(F32), cos_ref[0, rows], sin_ref[0, rows])
        k_scr[rows] = _mxu(kr)
        km_scr[j:j + 1, :] = jnp.mean(kr, axis=0, keepdims=True)
    km_hi, km_lo = _split_hi_lo(km_scr[...])

    def logits(qi):
        rows = slice(qi * blk, (qi + 1) * blk)
        qf = _rope(q_ref[0, rows].astype(F32), cos_ref[0, rows], sin_ref[0, rows])
        s = _dot_nt(_mxu(qf * Q_SCALE), k_scr[:(qi + 1) * blk])
        if qi == 0:
            return s, None
        q_hi, q_lo = _split_hi_lo(qf)
        gate = (_dot_nt(km_hi, q_hi) + _dot_nt(km_hi, q_lo) + _dot_nt(km_lo, q_hi))[:n_blk]
        row = lax.broadcasted_iota(jnp.int32, gate.shape, 0)
        gate = jnp.where(row < qi, gate, NEG_INF)
        rank = jnp.zeros(gate.shape, F32)
        for jp in range(qi):
            g_jp = gate[jp:jp + 1, :]
            tie_ahead = jnp.where(row > jp, 1.0, 0.0)
            rank = rank + jnp.where(g_jp > gate, 1.0, 0.0) + jnp.where(g_jp == gate, tie_ahead, 0.0)
        bias_t = jnp.where(row < qi, jnp.where(rank < MOBA_TOPK, 0.0, NEG_INF), NEG_INF)
        bias_t = jnp.concatenate([bias_t, jnp.full((LANES - n_blk, blk), NEG_INF, F32)], axis=0)
        return s, bias_t.T

    nxt = logits(0)
    for qi in range(n_blk):
        s, sel_bias = nxt
        if qi + 1 < n_blk:
            nxt = logits(qi + 1)
        pieces = [s[:, kj * blk:(kj + 1) * blk] + sel_bias[:, kj:kj + 1] for kj in range(qi)]
        pieces.append(_causal_mask(s[:, qi * blk:(qi + 1) * blk]))
        o = _softmax_pv(pieces, v_ref[0, :(qi + 1) * blk])
        o_ref[0, qi * blk:(qi + 1) * blk] = o.astype(o_ref.dtype)


def _moba_attention(qkv3, cos, sin):
    bsz, seq, _ = qkv3.shape
    assert seq % MOBA_BLOCK == 0 and seq // MOBA_BLOCK <= LANES
    tab = pl.BlockSpec((1, seq, LANES), lambda b, h: (b, 0, 0))
    return pl.pallas_call(
        _moba_kernel,
        grid=(bsz, N_HEADS_MOBA),
        in_specs=_qkv_specs(seq, N_HEADS_DIL) + [tab, tab],
        out_specs=pl.BlockSpec((1, seq, HEAD_DIM), lambda b, h: (b, 0, h)),
        out_shape=jax.ShapeDtypeStruct((bsz, seq, N_HEADS_MOBA * HEAD_DIM), MXU_DTYPE),
        scratch_shapes=[pltpu.VMEM((seq, HEAD_DIM), MXU_DTYPE),
                        pltpu.VMEM((LANES, HEAD_DIM), F32)],
        compiler_params=_cparams("parallel", "parallel"),
        name="moba_attention",
    )(qkv3, qkv3, qkv3, cos, sin)


def _fox_kernel(q_ref, k_ref, v_ref, f_ref, o_ref):
    tq = TQ_FOX
    seq = q_ref.shape[1]
    def logits(qi):
        q = _mxu(q_ref[0, qi * tq:(qi + 1) * tq].astype(F32) * Q_SCALE)
        return _dot_nt(q, k_ref[0, :(qi + 1) * tq])

    n_q = seq // tq
    s_next = logits(0)
    for qi in range(n_q):
        past = slice(0, (qi + 1) * tq)
        s = s_next - f_ref[0, 0, :, past] * LOG2E
        if qi + 1 < n_q:
            s_next = logits(qi + 1)
        pieces = [s[:, kj * tq:(kj + 1) * tq] for kj in range(qi)]
        pieces.append(_causal_mask(s[:, qi * tq:(qi + 1) * tq]))
        o_ref[0, qi * tq:(qi + 1) * tq] = _softmax_pv(pieces, v_ref[0, past]).astype(o_ref.dtype)


def _fox_attention(qkv3, cum_f):
    bsz, seq, _ = qkv3.shape
    f4 = cum_f.reshape(bsz, cum_f.shape[1], 1, seq)
    return pl.pallas_call(
        _fox_kernel,
        grid=(bsz, N_HEADS_FOX),
        in_specs=_qkv_specs(seq, N_HEADS_DIL + N_HEADS_MOBA)
        + [pl.BlockSpec((1, 1, 1, seq), lambda b, h: (b, h, 0, 0))],
        out_specs=pl.BlockSpec((1, seq, HEAD_DIM), lambda b, h: (b, 0, h)),
        out_shape=jax.ShapeDtypeStruct((bsz, seq, N_HEADS_FOX * HEAD_DIM), MXU_DTYPE),
        compiler_params=_cparams("parallel", "parallel"),
        name="fox_attention",
    )(qkv3, qkv3, qkv3, f4)


def _out_proj_kernel(od_ref, om_ref, of_ref, wd_ref, wm_ref, wf_ref, x_ref, g_ref, gate_ref, o_ref):
    y = _dot(od_ref[...], wd_ref[...]) + _dot(om_ref[...], wm_ref[...]) + _dot(of_ref[...], wf_ref[...])
    o_ref[...] = x_ref[...] + gate_ref[0] * _rms_norm(y, g_ref[...])


def _out_proj(o_dil, o_moba, o_fox, w_out, x2, g, gate, seq):
    t, d = x2.shape
    nd, nm = o_dil.shape[1], o_moba.shape[1]
    w_d, w_m, w_f = w_out[:nd], w_out[nd:nd + nm], w_out[nd + nm:]
    per_b = seq // TM_OUT
    row = lambda i: (i, 0)
    fixed = lambda i: (0, 0)
    return pl.pallas_call(
        _out_proj_kernel,
        grid=(t // TM_OUT,),
        in_specs=[
            pl.BlockSpec((TM_OUT, nd), row),
            pl.BlockSpec((TM_OUT, nm), row),
            pl.BlockSpec((TM_OUT, o_fox.shape[1]), row),
            pl.BlockSpec(w_d.shape, fixed),
            pl.BlockSpec(w_m.shape, fixed),
            pl.BlockSpec(w_f.shape, fixed),
            pl.BlockSpec((TM_OUT, d), row),
            pl.BlockSpec((1, d), fixed),
            pl.BlockSpec((1, 1, d), lambda i: (i // per_b, 0, 0)),
        ],
        out_specs=pl.BlockSpec((TM_OUT, d), row),
        out_shape=jax.ShapeDtypeStruct((t, d), F32),
        compiler_params=_cparams("parallel"),
        name="out_proj",
    )(o_dil, o_moba, o_fox, w_d, w_m, w_f, x2, g, gate)


def _swiglu_tile(h, wg, wu, wd):
    g = _dot(h, wg)
    u = _dot(h, wu)
    return _dot(_mxu(g * _sigmoid(g) * u), wd)


def _ffn_kernel(x_ref, gpre_ref, sh_ref, sc_ref, wg_ref, wu_ref, wd_ref, gpost_ref, gate_ref,
                o_ref, h_scr, acc_scr):
    j = pl.program_id(1)

    @pl.when(j == 0)
    def _():
        h = _rms_norm(x_ref[...], gpre_ref[...]) * (1.0 + sc_ref[0]) + sh_ref[0]
        h_scr[...] = _mxu(h)
        acc_scr[...] = jnp.zeros(acc_scr.shape, F32)

    acc_scr[...] += _swiglu_tile(h_scr[...], wg_ref[...], wu_ref[...], wd_ref[...])

    @pl.when(j == pl.num_programs(1) - 1)
    def _():
        o_ref[...] = x_ref[...] + gate_ref[0] * _rms_norm(acc_scr[...], gpost_ref[...])


def _dense_ffn(x2, g_pre, shift, scale, w_gate, w_up, w_down, g_post, gate, seq):
    t, d = x2.shape
    d_ff = w_gate.shape[1]
    per_b = seq // TM_FFN
    row = lambda i, j: (i, 0)
    fixed = lambda i, j: (0, 0)
    bmap = lambda i, j: (i // per_b, 0, 0)
    return pl.pallas_call(
        _ffn_kernel,
        grid=(t // TM_FFN, d_ff // TF_FFN),
        in_specs=[
            pl.BlockSpec((TM_FFN, d), row),
            pl.BlockSpec((1, d), fixed),
            pl.BlockSpec((1, 1, d), bmap),
            pl.BlockSpec((1, 1, d), bmap),
            pl.BlockSpec((d, TF_FFN), lambda i, j: (0, j)),
            pl.BlockSpec((d, TF_FFN), lambda i, j: (0, j)),
            pl.BlockSpec((TF_FFN, d), lambda i, j: (j, 0)),
            pl.BlockSpec((1, d), fixed),
            pl.BlockSpec((1, 1, d), bmap),
        ],
        out_specs=pl.BlockSpec((TM_FFN, d), row),
        out_shape=jax.ShapeDtypeStruct((t, d), F32),
        scratch_shapes=[pltpu.VMEM((TM_FFN, d), MXU_DTYPE), pltpu.VMEM((TM_FFN, d), F32)],
        compiler_params=_cparams("parallel", "arbitrary"),
        name="dense_ffn",
    )(x2, g_pre, shift, scale, w_gate, w_up, w_down, g_post, gate)


def _router_kernel(x_ref, g_ref, sh_ref, sc_ref, whi_ref, wlo_ref, br_ref, gw_ref, gi_ref):
    h = _rms_norm(x_ref[...], g_ref[...]) * (1.0 + sc_ref[0]) + sh_ref[0]
    h_hi, h_lo = _split_hi_lo(h)
    logits = (_dot(h_hi, whi_ref[...]) + _dot(h_lo, whi_ref[...]) + _dot(h_hi, wlo_ref[...])
              + br_ref[...])
    lane = lax.broadcasted_iota(jnp.int32, logits.shape, 1)
    logits = jnp.where(lane < N_EXPERTS, logits, NEG_INF)
    m1 = jnp.max(logits, axis=-1, keepdims=True)
    i1 = jnp.min(jnp.where(logits == m1, lane, LANES), axis=-1, keepdims=True)
    rest = jnp.where(lane == i1, NEG_INF, logits)
    m2 = jnp.max(rest, axis=-1, keepdims=True)
    i2 = jnp.min(jnp.where(rest == m2, lane, LANES), axis=-1, keepdims=True)
    e2 = jnp.exp(m2 - m1)
    p1 = 1.0 / (1.0 + e2)
    p2 = e2 * p1
    gw_ref[...] = jnp.where(lane == 0, p1, jnp.where(lane == 1, p2, 0.0))
    gi_ref[...] = jnp.where(lane == 0, i1, jnp.where(lane == 1, i2, 0))


def _router(x2, g_pre, shift, scale, w_router, b_router, seq):
    t, d = x2.shape
    tm = TM_OUT
    w_pad = jnp.pad(w_router, ((0, 0), (0, LANES - N_EXPERTS)))
    w_hi = w_pad.astype(MXU_DTYPE)
    w_lo = (w_pad - w_hi.astype(F32)).astype(MXU_DTYPE)
    b_pad = jnp.pad(b_router, (0, LANES - N_EXPERTS)).reshape(1, LANES)
    per_b = seq // tm
    row = lambda i: (i, 0)
    fixed = lambda i: (0, 0)
    bmap = lambda i: (i // per_b, 0, 0)
    return pl.pallas_call(
        _router_kernel,
        grid=(t // tm,),
        in_specs=[
            pl.BlockSpec((tm, d), row),
            pl.BlockSpec((1, d), fixed),
            pl.BlockSpec((1, 1, d), bmap),
            pl.BlockSpec((1, 1, d), bmap),
            pl.BlockSpec((d, LANES), fixed),
            pl.BlockSpec((d, LANES), fixed),
            pl.BlockSpec((1, LANES), fixed),
        ],
        out_specs=[pl.BlockSpec((tm, LANES), row), pl.BlockSpec((tm, LANES), row)],
        out_shape=[jax.ShapeDtypeStruct((t, LANES), F32), jax.ShapeDtypeStruct((t, LANES), jnp.int32)],
        compiler_params=_cparams("parallel"),
        name="router",
    )(x2, g_pre, shift, scale, w_hi, w_lo, b_pad)


def _routing_tables(top_idx, n_tiles):
    flat_e = top_idx.T.reshape(-1)
    experts = jnp.arange(N_EXPERTS, dtype=jnp.int32)
    onehot = (flat_e[:, None] == experts[None, :]).astype(jnp.int32)
    csum = jnp.cumsum(onehot, axis=0)
    rank = jnp.sum(onehot * csum, axis=1) - 1
    counts = csum[-1]
    padded = ((counts + TM_MOE - 1) // TM_MOE) * TM_MOE
    ends = jnp.cumsum(padded)
    starts = ends - padded
    pos = (jnp.sum(onehot * starts[None, :], axis=1) + rank).astype(jnp.int32)
    tile_start = jnp.arange(n_tiles, dtype=jnp.int32) * TM_MOE
    tile_expert = jnp.minimum(jnp.sum((tile_start[:, None] >= ends[None, :]).astype(jnp.int32), axis=1),
                              N_EXPERTS - 1).astype(jnp.int32)
    n_active = (ends[-1] // TM_MOE).astype(jnp.int32).reshape(1)
    pad_start = (starts + counts).astype(jnp.int32)
    pad_len = (padded - counts).astype(jnp.int32)
    tiles = jnp.arange(n_tiles, dtype=jnp.int32)
    first_tile = (starts // TM_MOE).astype(jnp.int32)
    has_rows = (counts > 0).astype(jnp.int32)
    active = tiles < n_active[0]
    first_of_tile = jnp.sum(jnp.where(tile_expert[:, None] == experts[None, :], first_tile[None, :], 0), axis=1)
    nth_of_expert = tiles - first_of_tile
    is_first = active & (nth_of_expert < FIRST_TILES)
    first_block = tile_expert * FIRST_TILES + jnp.clip(nth_of_expert, 0, FIRST_TILES - 1)
    frozen = is_first | ~active
    prev_live = jnp.max(jnp.where((tiles[None, :] <= tiles[:, None]) & ~frozen[None, :], tiles[None, :], -1),
                        axis=1)
    weight_expert = jnp.sum(jnp.where(tiles[None, :] == jnp.maximum(prev_live, 0)[:, None],
                                      tile_expert[None, :], 0), axis=1)
    return (pos, tile_expert, n_active, pad_start, pad_len, first_tile, has_rows,
            is_first.astype(jnp.int32), first_block.astype(jnp.int32), frozen.astype(jnp.int32),
            weight_expert.astype(jnp.int32))


def _row_copy(src, src_row, dst, dst_row, sem):
    return pltpu.make_async_copy(src.at[pl.ds(src_row, 1)], dst.at[pl.ds(dst_row, 1)], sem)


def _for_row_groups(n_rows, body):
    def trip(g, c):
        r0 = pl.multiple_of(g * DMA_UNROLL, DMA_UNROLL)
        for k in range(DMA_UNROLL):
            body(r0 + k)
        return c

    lax.fori_loop(0, n_rows // DMA_UNROLL, trip, 0)


def _dispatch_kernel(pos_ref, pstart_ref, plen_ref, nact_ref, x_ref, g_ref, sh_ref, sc_ref, hs_hbm,
                     hbuf, sem, zsem):
    i = pl.program_id(0)
    n = pl.num_programs(0)
    tm = x_ref.shape[0]
    t_total = n * tm
    slot = i % 2

    def wait_slot(s):
        for _ in range(TOP_K):
            pltpu.make_async_copy(hbuf.at[s], hs_hbm.at[pl.ds(0, tm)], sem.at[s]).wait()

    @pl.when(i >= 2)
    def _():
        wait_slot(slot)

    hbuf[slot] = _rms_norm(x_ref[...], g_ref[...]) * (1.0 + sc_ref[0]) + sh_ref[0]

    def send(r):
        for s in range(TOP_K):
            _row_copy(hbuf.at[slot], r, hs_hbm, pos_ref[s * t_total + i * tm + r], sem.at[slot]).start()

    _for_row_groups(tm, send)

    @pl.when(i == n - 1)
    def _():
        wait_slot(slot)

        @pl.when(n >= 2)
        def _():
            wait_slot(1 - slot)

        zeros = hbuf.at[0]
        zeros[...] = jnp.zeros(zeros.shape, F32)
        for e in range(N_EXPERTS):
            def zero_row(k, c):
                _row_copy(zeros, 0, hs_hbm, pstart_ref[e] + k, zsem).start()
                return c

            def zero_row_wait(k, c):
                _row_copy(zeros, 0, hs_hbm, 0, zsem).wait()
                return c

            lax.fori_loop(0, plen_ref[e], zero_row, 0)
            lax.fori_loop(0, plen_ref[e], zero_row_wait, 0)

        tail0 = nact_ref[0] * TM_MOE
        n_tail = (hs_hbm.shape[0] - tail0) // tm

        def zero_block(k, c):
            pltpu.make_async_copy(zeros, hs_hbm.at[pl.ds(tail0 + k * tm, tm)], zsem).start()
            return c

        def zero_block_wait(k, c):
            pltpu.make_async_copy(zeros, hs_hbm.at[pl.ds(0, tm)], zsem).wait()
            return c

        lax.fori_loop(0, n_tail, zero_block, 0)
        lax.fori_loop(0, n_tail, zero_block_wait, 0)


def _dispatch(x2, g_pre, shift, scale, pos, pad_start, pad_len, n_active, n_tiles, seq):
    t, d = x2.shape
    tm = TM_DSP
    assert TM_MOE % tm == 0
    per_b = seq // tm
    return pl.pallas_call(
        _dispatch_kernel,
        grid_spec=pltpu.PrefetchScalarGridSpec(
            num_scalar_prefetch=4,
            grid=(t // tm,),
            in_specs=[
                pl.BlockSpec((tm, d), lambda i, *_: (i, 0)),
                pl.BlockSpec((1, d), lambda i, *_: (0, 0)),
                pl.BlockSpec((1, 1, d), lambda i, *_: (i // per_b, 0, 0)),
                pl.BlockSpec((1, 1, d), lambda i, *_: (i // per_b, 0, 0)),
            ],
            out_specs=pl.BlockSpec(memory_space=pl.ANY),
            scratch_shapes=[pltpu.VMEM((2, tm, d), F32),
                            pltpu.SemaphoreType.DMA((2,)), pltpu.SemaphoreType.DMA(())],
        ),
        out_shape=jax.ShapeDtypeStruct((n_tiles * TM_MOE, d), F32),
        compiler_params=_cparams("arbitrary"),
        name="expert_dispatch",
    )(pos, pad_start, pad_len, n_active, x2, g_pre, shift, scale)


def _expert_first_kernel(ft_ref, has_ref, *refs):
    h_refs = refs[:FIRST_TILES]
    wg_ref, wu_ref, wd_ref, y_ref, qg_ref, qu_ref, qd_ref, hb_scr = refs[FIRST_TILES:]
    e, j = pl.program_id(0), pl.program_id(1)
    tm = h_refs[0].shape[0]
    wg, wu, wd = _mxu(wg_ref[0]), _mxu(wu_ref[0]), _mxu(wd_ref[0])
    qg_ref[0] = wg
    qu_ref[0] = wu
    qd_ref[0] = wd

    @pl.when(j == 0)
    def _():
        y_ref[...] = jnp.zeros(y_ref.shape, F32)
        for k, h_ref in enumerate(h_refs):
            hb_scr[k * tm:(k + 1) * tm] = _mxu(h_ref[...])

    @pl.when(has_ref[e] > 0)
    def _():
        y_ref[...] += _swiglu_tile(hb_scr[...], wg, wu, wd)


def _expert_first(hs, first_tile, has_rows, w_gate, w_up, w_down):
    n_rows, d = hs.shape
    n_exp, _, d_ff = w_gate.shape
    tf = TF_FIRST
    last_tile = n_rows // TM_MOE - 1
    col = lambda e, j, ft, has: (e, 0, j)
    rowb = lambda e, j, ft, has: (e, j, 0)
    cast = lambda a: jax.ShapeDtypeStruct(a.shape, MXU_DTYPE)

    def h_spec(k):
        return pl.BlockSpec((TM_MOE, d), lambda e, j, ft, has: (jnp.minimum(ft[e] + k, last_tile), 0),
                            pipeline_mode=pl.Buffered(1))

    return pl.pallas_call(
        _expert_first_kernel,
        grid_spec=pltpu.PrefetchScalarGridSpec(
            num_scalar_prefetch=2,
            grid=(n_exp, d_ff // tf),
            in_specs=[h_spec(k) for k in range(FIRST_TILES)] + [
                pl.BlockSpec((1, d, tf), col),
                pl.BlockSpec((1, d, tf), col),
                pl.BlockSpec((1, tf, d), rowb),
            ],
            out_specs=[
                pl.BlockSpec((FIRST_TILES * TM_MOE, d), lambda e, j, ft, has: (e, 0)),
                pl.BlockSpec((1, d, tf), col),
                pl.BlockSpec((1, d, tf), col),
                pl.BlockSpec((1, tf, d), rowb),
            ],
            scratch_shapes=[pltpu.VMEM((FIRST_TILES * TM_MOE, d), MXU_DTYPE)],
        ),
        out_shape=[jax.ShapeDtypeStruct((n_exp * FIRST_TILES * TM_MOE, d), F32),
                   cast(w_gate), cast(w_up), cast(w_down)],
        compiler_params=_cparams("arbitrary", "arbitrary"),
        name="expert_first",
    )(first_tile, has_rows, *([hs] * FIRST_TILES), w_gate, w_up, w_down)


def _expert_kernel(te_ref, nact_ref, first_ref, fblk_ref, frozen_ref, we_ref, h_ref, yf_ref,
                   wg_ref, wu_ref, wd_ref, o_ref, hb_scr):
    i, j = pl.program_id(0), pl.program_id(1)
    compute = frozen_ref[i] == 0

    @pl.when(j == 0)
    def _():
        o_ref[...] = jnp.zeros(o_ref.shape, F32)

    @pl.when(compute & (j == 0))
    def _():
        hb_scr[...] = _mxu(h_ref[...])

    @pl.when(compute)
    def _():
        o_ref[...] += _swiglu_tile(hb_scr[...], wg_ref[0], wu_ref[0], wd_ref[0])

    @pl.when((j == pl.num_programs(1) - 1) & (first_ref[i] != 0))
    def _():
        o_ref[...] = yf_ref[...]


def _expert_ffn(hs, y_first, tile_expert, n_active, is_first, first_block, frozen, weight_expert,
                w_gate, w_up, w_down):
    n_rows, d = hs.shape
    d_ff = w_gate.shape[2]
    n_tiles = n_rows // TM_MOE
    n_ff = d_ff // TF_FFN

    def ff_step(i, j, frozen):
        return jnp.where(frozen[i] != 0, n_ff - 1, j)

    def row_tile(i, nact):
        return jnp.minimum(i, nact[0] - 1)

    return pl.pallas_call(
        _expert_kernel,
        grid_spec=pltpu.PrefetchScalarGridSpec(
            num_scalar_prefetch=6,
            grid=(n_tiles, n_ff),
            in_specs=[
                pl.BlockSpec((TM_MOE, d), lambda i, j, te, nact, fi, fb, fr, we: (row_tile(i, nact), 0)),
                pl.BlockSpec((TM_MOE, d), lambda i, j, te, nact, fi, fb, fr, we: (fb[i], 0),
                             pipeline_mode=pl.Buffered(1)),
                pl.BlockSpec((1, d, TF_FFN), lambda i, j, te, nact, fi, fb, fr, we: (we[i], 0, ff_step(i, j, fr))),
                pl.BlockSpec((1, d, TF_FFN), lambda i, j, te, nact, fi, fb, fr, we: (we[i], 0, ff_step(i, j, fr))),
                pl.BlockSpec((1, TF_FFN, d), lambda i, j, te, nact, fi, fb, fr, we: (we[i], ff_step(i, j, fr), 0)),
            ],
            out_specs=pl.BlockSpec((TM_MOE, d), lambda i, j, te, nact, fi, fb, fr, we: (i, 0)),
            scratch_shapes=[pltpu.VMEM((TM_MOE, d), MXU_DTYPE)],
        ),
        out_shape=jax.ShapeDtypeStruct((n_rows, d), F32),
        compiler_params=_cparams("arbitrary", "arbitrary"),
        name="expert_ffn",
    )(tile_expert, n_active, is_first, first_block, frozen, weight_expert, hs, y_first, w_gate, w_up, w_down)


def _combine_kernel(pos_ref, ys_hbm, gw_ref, x_ref, g_ref, gate_ref, o_ref, buf, sem):
    i = pl.program_id(0)
    n = pl.num_programs(0)
    tm = x_ref.shape[0]
    t_total = n * tm
    cur = i % 2

    def fetch(step, b):
        def one(r):
            for s in range(TOP_K):
                _row_copy(ys_hbm, pos_ref[s * t_total + step * tm + r], buf.at[b, s], r, sem.at[b]).start()

        _for_row_groups(tm, one)

    @pl.when(i == 0)
    def _():
        fetch(0, 0)

    @pl.when(i + 1 < n)
    def _():
        fetch(i + 1, 1 - cur)

    for s in range(TOP_K):
        pltpu.make_async_copy(ys_hbm.at[pl.ds(0, tm)], buf.at[cur, s], sem.at[cur]).wait()
    gw = gw_ref[...]
    y = gw[:, 0:1] * buf[cur, 0]
    for s in range(1, TOP_K):
        y = y + gw[:, s:s + 1] * buf[cur, s]
    o_ref[...] = x_ref[...] + gate_ref[0] * _rms_norm(y, g_ref[...])


def _combine(ys, pos, gate_w, x2, g_post, gate, seq):
    t, d = x2.shape
    tm = TM_CMB
    per_b = seq // tm
    return pl.pallas_call(
        _combine_kernel,
        grid_spec=pltpu.PrefetchScalarGridSpec(
            num_scalar_prefetch=1,
            grid=(t // tm,),
            in_specs=[
                pl.BlockSpec(memory_space=pl.ANY),
                pl.BlockSpec((tm, LANES), lambda i, pos: (i, 0)),
                pl.BlockSpec((tm, d), lambda i, pos: (i, 0)),
                pl.BlockSpec((1, d), lambda i, pos: (0, 0)),
                pl.BlockSpec((1, 1, d), lambda i, pos: (i // per_b, 0, 0)),
            ],
            out_specs=pl.BlockSpec((tm, d), lambda i, pos: (i, 0)),
            scratch_shapes=[pltpu.VMEM((2, TOP_K, tm, d), F32), pltpu.SemaphoreType.DMA((2,))],
        ),
        out_shape=jax.ShapeDtypeStruct((t, d), F32),
        compiler_params=_cparams("arbitrary"),
        name="expert_combine",
    )(pos, ys, gate_w, x2, g_post, gate)


def _moe_ffn(x2, g_pre, shift, scale, w_router, b_router, w_gate, w_up, w_down, g_post, gate, seq):
    t = x2.shape[0]
    n_tiles = (TOP_K * t) // TM_MOE + N_EXPERTS
    gate_w, top = _router(x2, g_pre, shift, scale, w_router, b_router, seq)
    (pos, tile_expert, n_active, pad_start, pad_len, first_tile, has_rows, is_first, first_block, frozen,
     weight_expert) = _routing_tables(top[:, :TOP_K], n_tiles)
    hs = _dispatch(x2, g_pre, shift, scale, pos, pad_start, pad_len, n_active, n_tiles, seq)
    y_first, q_gate, q_up, q_down = _expert_first(hs, first_tile, has_rows, w_gate, w_up, w_down)
    ys = _expert_ffn(hs, y_first, tile_expert, n_active, is_first, first_block, frozen, weight_expert,
                     q_gate, q_up, q_down)
    return _combine(ys, pos, gate_w, x2, g_post, gate, seq)


def kernel(x, c, positions, ln_mix_pre, ln_mix_post, ln_ffn_pre, ln_ffn_post, w_mod, b_mod, w_in, b_fgate, w_out, w_ffn_gate, w_ffn_up, w_ffn_down, w_router, b_router, w_exp_gate, w_exp_up, w_exp_down):
    bsz, seq, d = x.shape
    depth = w_mod.shape[0]
    n_qkv = 3 * N_HEADS * HEAD_DIM
    x2 = x.reshape(bsz * seq, d)
    mod = _modulation(c, w_mod, b_mod).reshape(depth, bsz, N_MOD, 1, d)
    cos, sin = _rope_tables(positions)
    for layer in range(depth):
        sh_m, sc_m, g_m, sh_f, sc_f, g_f = (mod[layer, :, k] for k in range(N_MOD))
        row = lambda a: a[layer].reshape(1, d)
        w_qkv = _mxu(w_in[layer, :, :n_qkv])
        w_f = _mxu(jnp.pad(w_in[layer, :, n_qkv:], ((0, 0), (0, LANES - N_HEADS_FOX))))
        qkv, fg = _in_proj(x2, row(ln_mix_pre), sh_m, sc_m, w_qkv, w_f, seq)
        qkv3 = qkv.reshape(bsz, seq, n_qkv)
        cum_f = _forget_cumsum(fg.reshape(bsz, seq, LANES), b_fgate[layer])
        o_dil = _dilated_attention(qkv3, cos, sin).reshape(bsz * seq, -1)
        o_moba = _moba_attention(qkv3, cos, sin).reshape(bsz * seq, -1)
        o_fox = _fox_attention(qkv3, cum_f).reshape(bsz * seq, -1)
        x2 = _out_proj(o_dil, o_moba, o_fox, _mxu(w_out[layer]), x2, row(ln_mix_post), g_m, seq)
        j = layer // 2
        if layer % 2 == 0:
            x2 = _dense_ffn(x2, row(ln_ffn_pre), sh_f, sc_f, _mxu(w_ffn_gate[j]), _mxu(w_ffn_up[j]),
                            _mxu(w_ffn_down[j]), row(ln_ffn_post), g_f, seq)
        else:
            x2 = _moe_ffn(x2, row(ln_ffn_pre), sh_f, sc_f, w_router[j], b_router[j],
                          w_exp_gate[j], w_exp_up[j], w_exp_down[j], row(ln_ffn_post), g_f, seq)
    return x2.reshape(bsz, seq, d)
```

```python
import functools

import numpy as np
import jax
import jax.numpy as jnp
from jax import lax
from jax.experimental import pallas as pl
from jax.experimental.pallas import tpu as pltpu

F32 = jnp.float32
MXU_DTYPE = jnp.bfloat16

HEAD_DIM = 128
N_HEADS = 16
N_HEADS_DIL = 6
N_HEADS_MOBA = 4
N_HEADS_FOX = 6
DILATED_BRANCHES = ((128, 1), (512, 4), (2048, 16))
MOBA_BLOCK = 256
MOBA_TOPK = 3
ROPE_THETA = 500000.0
ROT_DIM = HEAD_DIM // 4
ROT_HALF = ROT_DIM // 2
N_EXPERTS = 8
TOP_K = 2
N_MOD = 6
RMS_EPS = 1e-6
NEG_INF = -1e30
ATTN_SCALE = HEAD_DIM ** -0.5
LANES = 128

VMEM_LIMIT_BYTES = 56 * 1024 * 1024

TM_IN = 1024
TN_IN = 1536
TM_OUT = 512
TM_FFN = 512
TF_FFN = 1024
TM_DSP = 256
TF_FIRST = 256
FIRST_TILES = 2
DMA_UNROLL = 8
TM_MOE = 512
TM_CMB = 256
TQ_FOX = 256
TQ_DIL = 128
TN_MOD = 1024


def _cparams(*sem):
    return pltpu.CompilerParams(dimension_semantics=sem, vmem_limit_bytes=VMEM_LIMIT_BYTES)


def _mxu(a):
    return a.astype(MXU_DTYPE)


def _dot(a, b):
    return jnp.dot(a, b, preferred_element_type=F32)


def _dot_nt(a, b):
    return lax.dot_general(a, b, (((1,), (1,)), ((), ())), preferred_element_type=F32)


def _split_hi_lo(a):
    hi = a.astype(MXU_DTYPE)
    lo = (a - hi.astype(F32)).astype(MXU_DTYPE)
    return hi, lo


def _rms_norm(x, g):
    ms = jnp.mean(x * x, axis=-1, keepdims=True)
    return x * lax.rsqrt(ms + RMS_EPS) * g


def _sigmoid(x):
    return 1.0 / (1.0 + jnp.exp(-x))


def _mod_kernel(c_ref, w_ref, b_ref, o_ref):
    c = c_ref[...]
    cond = c * _sigmoid(c)
    o_ref[0] = _dot(_mxu(cond), _mxu(w_ref[0])) + b_ref[0]


def _modulation(c, w_mod, b_mod):
    depth, d, n = w_mod.shape
    bsz = c.shape[0]
    rows = max(16, bsz)
    c_pad = jnp.pad(c, ((0, rows - bsz), (0, 0)))
    out = pl.pallas_call(
        _mod_kernel,
        grid=(depth, n // TN_MOD),
        in_specs=[
            pl.BlockSpec((rows, d), lambda l, j: (0, 0)),
            pl.BlockSpec((1, d, TN_MOD), lambda l, j: (l, 0, j)),
            pl.BlockSpec((1, 1, TN_MOD), lambda l, j: (l, 0, j)),
        ],
        out_specs=pl.BlockSpec((1, rows, TN_MOD), lambda l, j: (l, 0, j)),
        out_shape=jax.ShapeDtypeStruct((depth, rows, n), F32),
        compiler_params=_cparams("parallel", "parallel"),
        name="modulation",
    )(c_pad, w_mod, b_mod.reshape(depth, 1, n))
    return out[:, :bsz]


def _rope_table_kernel(pos_ref, freq_ref, cos_ref, sin_ref):
    ang = pos_ref[0].astype(F32) * freq_ref[...]
    lane = lax.broadcasted_iota(jnp.int32, ang.shape, 1)
    cos_ref[0] = jnp.where(lane < ROT_DIM, jnp.cos(ang), 1.0)
    sn = jnp.sin(ang)
    sin_ref[0] = jnp.where(lane < ROT_HALF, -sn, jnp.where(lane < ROT_DIM, sn, 0.0))


def _rope_tables(positions):
    bsz, seq = positions.shape
    inv_freq = ROPE_THETA ** (-np.arange(ROT_HALF, dtype=np.float32) / ROT_HALF)
    freq = np.zeros((1, LANES), np.float32)
    freq[0, :ROT_HALF] = inv_freq
    freq[0, ROT_HALF:ROT_DIM] = inv_freq
    tab = jax.ShapeDtypeStruct((bsz, seq, LANES), F32)
    return pl.pallas_call(
        _rope_table_kernel,
        grid=(bsz,),
        in_specs=[
            pl.BlockSpec((1, seq, 1), lambda b: (b, 0, 0)),
            pl.BlockSpec((1, LANES), lambda b: (0, 0)),
        ],
        out_specs=[pl.BlockSpec((1, seq, LANES), lambda b: (b, 0, 0))] * 2,
        out_shape=[tab, tab],
        compiler_params=_cparams("parallel"),
        name="rope_tables",
    )(positions.reshape(bsz, seq, 1), jnp.asarray(freq))


def _rope(t, cos, sin_signed):
    lane = lax.broadcasted_iota(jnp.int32, t.shape, 1)
    partner = jnp.where(lane < ROT_HALF,
                        pltpu.roll(t, LANES - ROT_HALF, 1),
                        pltpu.roll(t, ROT_HALF, 1))
    return t * cos + partner * sin_signed


def _in_proj_kernel(x_ref, g_ref, sh_ref, sc_ref, w_ref, wf_ref, qkv_ref, fg_ref, h_scr):
    @pl.when(pl.program_id(1) == 0)
    def _():
        h = _rms_norm(x_ref[...], g_ref[...]) * (1.0 + sc_ref[0]) + sh_ref[0]
        hb = _mxu(h)
        h_scr[...] = hb
        fg_ref[...] = _dot(hb, wf_ref[...])

    qkv_ref[...] = _dot(h_scr[...], w_ref[0]).astype(qkv_ref.dtype)


def _in_proj(x2, g, shift, scale, w_in, layer, n, w_f, seq):
    t, d = x2.shape
    per_b = seq // TM_IN
    bmap = lambda i, j: (i // per_b, 0, 0)
    return pl.pallas_call(
        _in_proj_kernel,
        grid=(t // TM_IN, n // TN_IN),
        in_specs=[
            pl.BlockSpec((TM_IN, d), lambda i, j: (i, 0)),
            pl.BlockSpec((1, d), lambda i, j: (0, 0)),
            pl.BlockSpec((1, 1, d), bmap),
            pl.BlockSpec((1, 1, d), bmap),
            pl.BlockSpec((1, d, TN_IN), lambda i, j: (layer, 0, j)),
            pl.BlockSpec((d, LANES), lambda i, j: (0, 0)),
        ],
        out_specs=[
            pl.BlockSpec((TM_IN, TN_IN), lambda i, j: (i, j)),
            pl.BlockSpec((TM_IN, LANES), lambda i, j: (i, 0)),
        ],
        out_shape=[
            jax.ShapeDtypeStruct((t, n), MXU_DTYPE),
            jax.ShapeDtypeStruct((t, LANES), F32),
        ],
        scratch_shapes=[pltpu.VMEM((TM_IN, d), MXU_DTYPE)],
        compiler_params=_cparams("parallel", "arbitrary"),
        name="in_proj",
    )(x2, g, shift, scale, w_in, w_f)


def _forget_kernel(fg_ref, b_ref, f_ref):
    z = fg_ref[0] + b_ref[...]
    log_f = jnp.minimum(z, 0.0) - jnp.log(1.0 + jnp.exp(-jnp.abs(z)))
    cum = log_f.T[0:8]
    seq = cum.shape[1]
    lane = lax.broadcasted_iota(jnp.int32, cum.shape, 1)
    shift = 1
    while shift < seq:
        cum = cum + jnp.where(lane >= shift, pltpu.roll(cum, shift, 1), 0.0)
        shift *= 2
    f_ref[0] = cum


def _forget_cumsum(fg3, b_fgate):
    bsz, seq, _ = fg3.shape
    b_pad = jnp.pad(b_fgate, (0, LANES - b_fgate.shape[0])).reshape(1, LANES)
    return pl.pallas_call(
        _forget_kernel,
        grid=(bsz,),
        in_specs=[
            pl.BlockSpec((1, seq, LANES), lambda b: (b, 0, 0)),
            pl.BlockSpec((1, LANES), lambda b: (0, 0)),
        ],
        out_specs=pl.BlockSpec((1, 8, seq), lambda b: (b, 0, 0)),
        out_shape=jax.ShapeDtypeStruct((bsz, 8, seq), F32),
        compiler_params=_cparams("parallel"),
        name="forget_cumsum",
    )(fg3, b_pad)


LOG2E = 1.4426950408889634
Q_SCALE = ATTN_SCALE * LOG2E


def _softmax_pv(pieces, v):
    m = pieces[0]
    for s in pieces[1:]:
        m = jnp.maximum(m, s)
    m = jnp.max(m, axis=-1, keepdims=True)
    ps = [jnp.exp2(s - m) for s in pieces]
    l = ps[0]
    for p in ps[1:]:
        l = l + p
    l = jnp.sum(l, axis=-1, keepdims=True)
    p_all = _mxu(ps[0]) if len(ps) == 1 else jnp.concatenate([_mxu(p) for p in ps], axis=1)
    return _dot(p_all, v) / l


def _causal_mask(s):
    row = lax.broadcasted_iota(jnp.int32, s.shape, 0)
    col = lax.broadcasted_iota(jnp.int32, s.shape, 1)
    return jnp.where(col <= row, s, NEG_INF)


def _qkv_specs(seq, head0):
    return [
        pl.BlockSpec((1, seq, HEAD_DIM), lambda b, h: (b, 0, head0 + h)),
        pl.BlockSpec((1, seq, HEAD_DIM), lambda b, h: (b, 0, N_HEADS + head0 + h)),
        pl.BlockSpec((1, seq, HEAD_DIM), lambda b, h: (b, 0, 2 * N_HEADS + head0 + h)),
    ]


def _rope_k_into(k_ref, cos_ref, sin_ref, k_scr, blk):
    seq = k_scr.shape[0]
    for j in range(seq // blk):
        rows = slice(j * blk, (j + 1) * blk)
        k_scr[rows] = _mxu(_rope(k_ref[0, rows].astype(F32), cos_ref[0, rows], sin_ref[0, rows]))


def _dilated_bias(tq):
    n_off = 2048 // tq
    off = jnp.arange(n_off, dtype=jnp.int32)[:, None, None] * tq
    delta = off + jnp.arange(tq, dtype=jnp.int32)[None, :, None] - jnp.arange(tq, dtype=jnp.int32)[None, None, :]
    count = jnp.zeros(delta.shape, F32)
    for window, dil in DILATED_BRANCHES:
        count = count + ((delta >= 0) & (delta % dil == 0) & (delta <= window)).astype(F32)
    return jnp.where(count > 0, jnp.log2(jnp.maximum(count, 1.0)), NEG_INF)


def _dilated_kernel(q_ref, k_ref, v_ref, cos_ref, sin_ref, bias_ref, o_ref, k_scr):
    tq = TQ_DIL
    seq = k_scr.shape[0]
    _rope_k_into(k_ref, cos_ref, sin_ref, k_scr, 512)

    def logits(qi):
        rows = slice(qi * tq, (qi + 1) * tq)
        q = _rope(q_ref[0, rows].astype(F32), cos_ref[0, rows], sin_ref[0, rows])
        return _dot_nt(_mxu(q * Q_SCALE), k_scr[:(qi + 1) * tq])

    n_q = seq // tq
    s_next = logits(0)
    for qi in range(n_q):
        s = s_next
        if qi + 1 < n_q:
            s_next = logits(qi + 1)
        pieces = [s[:, kj * tq:(kj + 1) * tq] + bias_ref[qi - kj] for kj in range(qi + 1)]
        o = _softmax_pv(pieces, v_ref[0, :(qi + 1) * tq])
        o_ref[0, qi * tq:(qi + 1) * tq] = o.astype(o_ref.dtype)


def _dilated_attention(qkv3, cos, sin):
    bsz, seq, _ = qkv3.shape
    assert seq == 2048, "dilated windows are laid out for a 2048-token sequence"
    bias = _dilated_bias(TQ_DIL)
    tab = pl.BlockSpec((1, seq, LANES), lambda b, h: (b, 0, 0))
    return pl.pallas_call(
        _dilated_kernel,
        grid=(bsz, N_HEADS_DIL),
        in_specs=_qkv_specs(seq, 0) + [tab, tab,
                                       pl.BlockSpec(bias.shape, lambda b, h: (0, 0, 0))],
        out_specs=pl.BlockSpec((1, seq, HEAD_DIM), lambda b, h: (b, 0, h)),
        out_shape=jax.ShapeDtypeStruct((bsz, seq, N_HEADS_DIL * HEAD_DIM), MXU_DTYPE),
        scratch_shapes=[pltpu.VMEM((seq, HEAD_DIM), MXU_DTYPE)],
        compiler_params=_cparams("parallel", "parallel"),
        name="dilated_attention",
    )(qkv3, qkv3, qkv3, cos, sin, bias)


def _moba_kernel(q_ref, k_ref, v_ref, cos_ref, sin_ref, o_ref, k_scr, km_scr):
    blk = MOBA_BLOCK
    seq = k_scr.shape[0]
    n_blk = seq // blk
    km_scr[...] = jnp.zeros(km_scr.shape, F32)
    for j in range(n_blk):
        rows = slice(j * blk, (j + 1) * blk)
        kr = _rope(k_ref[0, rows].astype(F32), cos_ref[0, rows], sin_ref[0, rows])
        k_scr[rows] = _mxu(kr)
        km_scr[j:j + 1, :] = jnp.mean(kr, axis=0, keepdims=True)
    km_hi, km_lo = _split_hi_lo(km_scr[...])

    def logits(qi):
        rows = slice(qi * blk, (qi + 1) * blk)
        qf = _rope(q_ref[0, rows].astype(F32), cos_ref[0, rows], sin_ref[0, rows])
        s = _dot_nt(_mxu(qf * Q_SCALE), k_scr[:(qi + 1) * blk])
        if qi == 0:
            return s, None
        q_hi, q_lo = _split_hi_lo(qf)
        gate = (_dot_nt(km_hi, q_hi) + _dot_nt(km_hi, q_lo) + _dot_nt(km_lo, q_hi))[:n_blk]
        row = lax.broadcasted_iota(jnp.int32, gate.shape, 0)
        gate = jnp.where(row < qi, gate, NEG_INF)
        rank = jnp.zeros(gate.shape, F32)
        for jp in range(qi):
            g_jp = gate[jp:jp + 1, :]
            tie_ahead = jnp.where(row > jp, 1.0, 0.0)
            rank = rank + jnp.where(g_jp > gate, 1.0, 0.0) + jnp.where(g_jp == gate, tie_ahead, 0.0)
        bias_t = jnp.where(row < qi, jnp.where(rank < MOBA_TOPK, 0.0, NEG_INF), NEG_INF)
        bias_t = jnp.concatenate([bias_t, jnp.full((LANES - n_blk, blk), NEG_INF, F32)], axis=0)
        return s, bias_t.T

    nxt = logits(0)
    for qi in range(n_blk):
        s, sel_bias = nxt
        if qi + 1 < n_blk:
            nxt = logits(qi + 1)
        pieces = [s[:, kj * blk:(kj + 1) * blk] + sel_bias[:, kj:kj + 1] for kj in range(qi)]
        pieces.append(_causal_mask(s[:, qi * blk:(qi + 1) * blk]))
        o = _softmax_pv(pieces, v_ref[0, :(qi + 1) * blk])
        o_ref[0, qi * blk:(qi + 1) * blk] = o.astype(o_ref.dtype)


def _moba_attention(qkv3, cos, sin):
    bsz, seq, _ = qkv3.shape
    assert seq % MOBA_BLOCK == 0 and seq // MOBA_BLOCK <= LANES
    tab = pl.BlockSpec((1, seq, LANES), lambda b, h: (b, 0, 0))
    return pl.pallas_call(
        _moba_kernel,
        grid=(bsz, N_HEADS_MOBA),
        in_specs=_qkv_specs(seq, N_HEADS_DIL) + [tab, tab],
        out_specs=pl.BlockSpec((1, seq, HEAD_DIM), lambda b, h: (b, 0, h)),
        out_shape=jax.ShapeDtypeStruct((bsz, seq, N_HEADS_MOBA * HEAD_DIM), MXU_DTYPE),
        scratch_shapes=[pltpu.VMEM((seq, HEAD_DIM), MXU_DTYPE),
                        pltpu.VMEM((LANES, HEAD_DIM), F32)],
        compiler_params=_cparams("parallel", "parallel"),
        name="moba_attention",
    )(qkv3, qkv3, qkv3, cos, sin)


def _fox_kernel(q_ref, k_ref, v_ref, f_ref, o_ref):
    tq = TQ_FOX
    seq = q_ref.shape[1]
    def logits(qi):
        q = _mxu(q_ref[0, qi * tq:(qi + 1) * tq].astype(F32) * Q_SCALE)
        return _dot_nt(q, k_ref[0, :(qi + 1) * tq])

    n_q = seq // tq
    s_next = logits(0)
    for qi in range(n_q):
        past = slice(0, (qi + 1) * tq)
        s = s_next - f_ref[0, 0, :, past] * LOG2E
        if qi + 1 < n_q:
            s_next = logits(qi + 1)
        pieces = [s[:, kj * tq:(kj + 1) * tq] for kj in range(qi)]
        pieces.append(_causal_mask(s[:, qi * tq:(qi + 1) * tq]))
        o_ref[0, qi * tq:(qi + 1) * tq] = _softmax_pv(pieces, v_ref[0, past]).astype(o_ref.dtype)


def _fox_attention(qkv3, cum_f):
    bsz, seq, _ = qkv3.shape
    f4 = cum_f.reshape(bsz, cum_f.shape[1], 1, seq)
    return pl.pallas_call(
        _fox_kernel,
        grid=(bsz, N_HEADS_FOX),
        in_specs=_qkv_specs(seq, N_HEADS_DIL + N_HEADS_MOBA)
        + [pl.BlockSpec((1, 1, 1, seq), lambda b, h: (b, h, 0, 0))],
        out_specs=pl.BlockSpec((1, seq, HEAD_DIM), lambda b, h: (b, 0, h)),
        out_shape=jax.ShapeDtypeStruct((bsz, seq, N_HEADS_FOX * HEAD_DIM), MXU_DTYPE),
        compiler_params=_cparams("parallel", "parallel"),
        name="fox_attention",
    )(qkv3, qkv3, qkv3, f4)


def _out_proj_kernel(od_ref, om_ref, of_ref, wd_ref, wm_ref, wf_ref, x_ref, g_ref, gate_ref, *rest):
    y = _dot(od_ref[...], wd_ref[...]) + _dot(om_ref[...], wm_ref[...]) + _dot(of_ref[...], wf_ref[...])
    x_new = x_ref[...] + gate_ref[0] * _rms_norm(y, g_ref[...])
    if len(rest) == 1:
        rest[0][...] = x_new
        return
    gpre_ref, sh_ref, sc_ref, whi_ref, wlo_ref, br_ref, o_ref, gw_ref, gi_ref = rest
    o_ref[...] = x_new
    h = _rms_norm(x_new, gpre_ref[...]) * (1.0 + sc_ref[0]) + sh_ref[0]
    _route(h, whi_ref, wlo_ref, br_ref, gw_ref, gi_ref)


def _out_proj(o_dil, o_moba, o_fox, w_out, x2, g, gate, seq, route=None):
    t, d = x2.shape
    nd, nm = o_dil.shape[1], o_moba.shape[1]
    w_d, w_m, w_f = w_out[:nd], w_out[nd:nd + nm], w_out[nd + nm:]
    per_b = seq // TM_OUT
    row = lambda i: (i, 0)
    fixed = lambda i: (0, 0)
    bmap = lambda i: (i // per_b, 0, 0)
    in_specs = [
        pl.BlockSpec((TM_OUT, nd), row),
        pl.BlockSpec((TM_OUT, nm), row),
        pl.BlockSpec((TM_OUT, o_fox.shape[1]), row),
        pl.BlockSpec(w_d.shape, fixed),
        pl.BlockSpec(w_m.shape, fixed),
        pl.BlockSpec(w_f.shape, fixed),
        pl.BlockSpec((TM_OUT, d), row),
        pl.BlockSpec((1, d), fixed),
        pl.BlockSpec((1, 1, d), bmap),
    ]
    out_specs = [pl.BlockSpec((TM_OUT, d), row)]
    out_shape = [jax.ShapeDtypeStruct((t, d), F32)]
    args = [o_dil, o_moba, o_fox, w_d, w_m, w_f, x2, g, gate]
    if route is not None:
        in_specs += [pl.BlockSpec((1, d), fixed), pl.BlockSpec((1, 1, d), bmap), pl.BlockSpec((1, 1, d), bmap),
                     pl.BlockSpec((d, LANES), fixed), pl.BlockSpec((d, LANES), fixed),
                     pl.BlockSpec((1, LANES), fixed)]
        out_specs += [pl.BlockSpec((TM_OUT, LANES), row)] * 2
        out_shape += [jax.ShapeDtypeStruct((t, LANES), F32), jax.ShapeDtypeStruct((t, LANES), jnp.int32)]
        args += list(route)
    out = pl.pallas_call(
        _out_proj_kernel,
        grid=(t // TM_OUT,),
        in_specs=in_specs,
        out_specs=out_specs,
        out_shape=out_shape,
        compiler_params=_cparams("parallel"),
        name="out_proj",
    )(*args)
    return out[0] if route is None else out


def _swiglu_tile(h, wg, wu, wd):
    g = _dot(h, wg)
    u = _dot(h, wu)
    return _dot(_mxu(g * _sigmoid(g) * u), wd)


def _ffn_first_kernel(x_ref, gpre_ref, sh_ref, sc_ref, wg_ref, wu_ref, wd_ref, gpost_ref, gate_ref,
                      o_ref, qg_ref, qu_ref, qd_ref, h_scr):
    j = pl.program_id(0)
    wg, wu, wd = _mxu(wg_ref[...]), _mxu(wu_ref[...]), _mxu(wd_ref[...])
    qg_ref[...] = wg
    qu_ref[...] = wu
    qd_ref[...] = wd

    @pl.when(j == 0)
    def _():
        h = _rms_norm(x_ref[...], gpre_ref[...]) * (1.0 + sc_ref[0]) + sh_ref[0]
        h_scr[...] = _mxu(h)
        o_ref[...] = jnp.zeros(o_ref.shape, F32)

    o_ref[...] += _swiglu_tile(h_scr[...], wg, wu, wd)

    @pl.when(j == pl.num_programs(0) - 1)
    def _():
        o_ref[...] = x_ref[...] + gate_ref[0] * _rms_norm(o_ref[...], gpost_ref[...])


def _dense_first(x2, g_pre, shift, scale, w_gate, w_up, w_down, g_post, gate, seq):
    t, d = x2.shape
    d_ff = w_gate.shape[1]
    tm, tf = FIRST_TILES * TM_FFN, TF_FIRST
    assert seq % tm == 0
    fixed = lambda j: (0, 0)
    bfix = lambda j: (0, 0, 0)
    once = dict(pipeline_mode=pl.Buffered(1))
    cast = lambda a: jax.ShapeDtypeStruct(a.shape, MXU_DTYPE)
    return pl.pallas_call(
        _ffn_first_kernel,
        grid=(d_ff // tf,),
        in_specs=[
            pl.BlockSpec((tm, d), fixed, **once),
            pl.BlockSpec((1, d), fixed),
            pl.BlockSpec((1, 1, d), bfix),
            pl.BlockSpec((1, 1, d), bfix),
            pl.BlockSpec((d, tf), lambda j: (0, j)),
            pl.BlockSpec((d, tf), lambda j: (0, j)),
            pl.BlockSpec((tf, d), lambda j: (j, 0)),
            pl.BlockSpec((1, d), fixed),
            pl.BlockSpec((1, 1, d), bfix),
        ],
        out_specs=[
            pl.BlockSpec((tm, d), fixed),
            pl.BlockSpec((d, tf), lambda j: (0, j)),
            pl.BlockSpec((d, tf), lambda j: (0, j)),
            pl.BlockSpec((tf, d), lambda j: (j, 0)),
        ],
        out_shape=[jax.ShapeDtypeStruct((tm, d), F32), cast(w_gate), cast(w_up), cast(w_down)],
        scratch_shapes=[pltpu.VMEM((tm, d), MXU_DTYPE)],
        compiler_params=_cparams("arbitrary"),
        name="dense_first",
    )(x2, g_pre, shift, scale, w_gate, w_up, w_down, g_post, gate)


def _ffn_kernel(x_ref, gpre_ref, sh_ref, sc_ref, wg_ref, wu_ref, wd_ref, gpost_ref, gate_ref, first_ref,
                o_ref, h_scr, acc_scr):
    i, j = pl.program_id(0), pl.program_id(1)
    done = i < FIRST_TILES

    @pl.when(jnp.logical_not(done) & (j == 0))
    def _():
        h = _rms_norm(x_ref[...], gpre_ref[...]) * (1.0 + sc_ref[0]) + sh_ref[0]
        h_scr[...] = _mxu(h)
        acc_scr[...] = jnp.zeros(acc_scr.shape, F32)

    @pl.when(jnp.logical_not(done))
    def _():
        acc_scr[...] += _swiglu_tile(h_scr[...], wg_ref[...], wu_ref[...], wd_ref[...])

    last = j == pl.num_programs(1) - 1

    @pl.when(jnp.logical_not(done) & last)
    def _():
        o_ref[...] = x_ref[...] + gate_ref[0] * _rms_norm(acc_scr[...], gpost_ref[...])

    @pl.when(done & last)
    def _():
        o_ref[...] = first_ref[...]


def _dense_ffn(x2, g_pre, shift, scale, w_gate, w_up, w_down, g_post, gate, seq):
    t, d = x2.shape
    d_ff = w_gate.shape[1]
    x_first, q_gate, q_up, q_down = _dense_first(x2, g_pre, shift, scale, w_gate, w_up, w_down,
                                                 g_post, gate, seq)
    per_b = seq // TM_FFN
    row = lambda i, j: (i, 0)
    fixed = lambda i, j: (0, 0)
    bmap = lambda i, j: (i // per_b, 0, 0)
    ff = lambda i, j: jnp.where(i < FIRST_TILES, 0, j)
    return pl.pallas_call(
        _ffn_kernel,
        grid=(t // TM_FFN, d_ff // TF_FFN),
        in_specs=[
            pl.BlockSpec((TM_FFN, d), row),
            pl.BlockSpec((1, d), fixed),
            pl.BlockSpec((1, 1, d), bmap),
            pl.BlockSpec((1, 1, d), bmap),
            pl.BlockSpec((d, TF_FFN), lambda i, j: (0, ff(i, j))),
            pl.BlockSpec((d, TF_FFN), lambda i, j: (0, ff(i, j))),
            pl.BlockSpec((TF_FFN, d), lambda i, j: (ff(i, j), 0)),
            pl.BlockSpec((1, d), fixed),
            pl.BlockSpec((1, 1, d), bmap),
            pl.BlockSpec((TM_FFN, d), lambda i, j: (jnp.minimum(i, FIRST_TILES - 1), 0),
                         pipeline_mode=pl.Buffered(1)),
        ],
        out_specs=pl.BlockSpec((TM_FFN, d), row),
        out_shape=jax.ShapeDtypeStruct((t, d), F32),
        scratch_shapes=[pltpu.VMEM((TM_FFN, d), MXU_DTYPE), pltpu.VMEM((TM_FFN, d), F32)],
        compiler_params=_cparams("arbitrary", "arbitrary"),
        name="dense_ffn",
    )(x2, g_pre, shift, scale, q_gate, q_up, q_down, g_post, gate, x_first)


def _route(h, whi_ref, wlo_ref, br_ref, gw_ref, gi_ref):
    h_hi, h_lo = _split_hi_lo(h)
    logits = (_dot(h_hi, whi_ref[...]) + _dot(h_lo, whi_ref[...]) + _dot(h_hi, wlo_ref[...])
              + br_ref[...])
    lane = lax.broadcasted_iota(jnp.int32, logits.shape, 1)
    logits = jnp.where(lane < N_EXPERTS, logits, NEG_INF)
    m1 = jnp.max(logits, axis=-1, keepdims=True)
    i1 = jnp.min(jnp.where(logits == m1, lane, LANES), axis=-1, keepdims=True)
    rest = jnp.where(lane == i1, NEG_INF, logits)
    m2 = jnp.max(rest, axis=-1, keepdims=True)
    i2 = jnp.min(jnp.where(rest == m2, lane, LANES), axis=-1, keepdims=True)
    e2 = jnp.exp(m2 - m1)
    p1 = 1.0 / (1.0 + e2)
    p2 = e2 * p1
    gw_ref[...] = jnp.where(lane == 0, p1, jnp.where(lane == 1, p2, 0.0))
    gi_ref[...] = jnp.where(lane == 0, i1, jnp.where(lane == 1, i2, 0))


def _router_params(w_router, b_router):
    w_pad = jnp.pad(w_router, ((0, 0), (0, LANES - N_EXPERTS)))
    w_hi = w_pad.astype(MXU_DTYPE)
    w_lo = (w_pad - w_hi.astype(F32)).astype(MXU_DTYPE)
    b_pad = jnp.pad(b_router, (0, LANES - N_EXPERTS)).reshape(1, LANES)
    return w_hi, w_lo, b_pad


def _routing_tables(top_idx, n_tiles):
    flat_e = top_idx.T.reshape(-1)
    experts = jnp.arange(N_EXPERTS, dtype=jnp.int32)
    onehot = (flat_e[:, None] == experts[None, :]).astype(jnp.int32)
    csum = jnp.cumsum(onehot, axis=0)
    rank = jnp.sum(onehot * csum, axis=1) - 1
    counts = csum[-1]
    padded = ((counts + TM_MOE - 1) // TM_MOE) * TM_MOE
    ends = jnp.cumsum(padded)
    starts = ends - padded
    pos = (jnp.sum(onehot * starts[None, :], axis=1) + rank).astype(jnp.int32)
    tile_start = jnp.arange(n_tiles, dtype=jnp.int32) * TM_MOE
    tile_expert = jnp.minimum(jnp.sum((tile_start[:, None] >= ends[None, :]).astype(jnp.int32), axis=1),
                              N_EXPERTS - 1).astype(jnp.int32)
    n_active = (ends[-1] // TM_MOE).astype(jnp.int32).reshape(1)
    pad_start = (starts + counts).astype(jnp.int32)
    pad_len = (padded - counts).astype(jnp.int32)
    tiles = jnp.arange(n_tiles, dtype=jnp.int32)
    first_tile = (starts // TM_MOE).astype(jnp.int32)
    has_rows = (counts > 0).astype(jnp.int32)
    active = tiles < n_active[0]
    first_of_tile = jnp.sum(jnp.where(tile_expert[:, None] == experts[None, :], first_tile[None, :], 0), axis=1)
    nth_of_expert = tiles - first_of_tile
    is_first = active & (nth_of_expert < FIRST_TILES)
    first_block = tile_expert * FIRST_TILES + jnp.clip(nth_of_expert, 0, FIRST_TILES - 1)
    frozen = is_first | ~active
    prev_live = jnp.max(jnp.where((tiles[None, :] <= tiles[:, None]) & ~frozen[None, :], tiles[None, :], -1),
                        axis=1)
    weight_expert = jnp.sum(jnp.where(tiles[None, :] == jnp.maximum(prev_live, 0)[:, None],
                                      tile_expert[None, :], 0), axis=1)
    return (pos, tile_expert, n_active, pad_start, pad_len, first_tile, has_rows,
            is_first.astype(jnp.int32), first_block.astype(jnp.int32), frozen.astype(jnp.int32),
            weight_expert.astype(jnp.int32))


def _row_copy(src, src_row, dst, dst_row, sem):
    return pltpu.make_async_copy(src.at[pl.ds(src_row, 1)], dst.at[pl.ds(dst_row, 1)], sem)


def _for_row_groups(n_rows, body):
    def trip(g, c):
        r0 = pl.multiple_of(g * DMA_UNROLL, DMA_UNROLL)
        for k in range(DMA_UNROLL):
            body(r0 + k)
        return c

    lax.fori_loop(0, n_rows // DMA_UNROLL, trip, 0)


def _dispatch_kernel(pos_ref, pstart_ref, plen_ref, nact_ref, x_ref, g_ref, sh_ref, sc_ref, hs_hbm,
                     hbuf, sem, zsem):
    i = pl.program_id(0)
    n = pl.num_programs(0)
    tm = x_ref.shape[0]
    t_total = n * tm
    slot = i % 2

    def wait_slot(s):
        for _ in range(TOP_K):
            pltpu.make_async_copy(hbuf.at[s], hs_hbm.at[pl.ds(0, tm)], sem.at[s]).wait()

    @pl.when(i >= 2)
    def _():
        wait_slot(slot)

    hbuf[slot] = _rms_norm(x_ref[...], g_ref[...]) * (1.0 + sc_ref[0]) + sh_ref[0]

    def send(r):
        for s in range(TOP_K):
            _row_copy(hbuf.at[slot], r, hs_hbm, pos_ref[s * t_total + i * tm + r], sem.at[slot]).start()

    _for_row_groups(tm, send)

    @pl.when(i == n - 1)
    def _():
        wait_slot(slot)

        @pl.when(n >= 2)
        def _():
            wait_slot(1 - slot)

        zeros = hbuf.at[0]
        zeros[...] = jnp.zeros(zeros.shape, F32)
        for e in range(N_EXPERTS):
            def zero_row(k, c):
                _row_copy(zeros, 0, hs_hbm, pstart_ref[e] + k, zsem).start()
                return c

            def zero_row_wait(k, c):
                _row_copy(zeros, 0, hs_hbm, 0, zsem).wait()
                return c

            lax.fori_loop(0, plen_ref[e], zero_row, 0)
            lax.fori_loop(0, plen_ref[e], zero_row_wait, 0)

        tail0 = nact_ref[0] * TM_MOE
        n_tail = (hs_hbm.shape[0] - tail0) // tm

        def zero_block(k, c):
            pltpu.make_async_copy(zeros, hs_hbm.at[pl.ds(tail0 + k * tm, tm)], zsem).start()
            return c

        def zero_block_wait(k, c):
            pltpu.make_async_copy(zeros, hs_hbm.at[pl.ds(0, tm)], zsem).wait()
            return c

        lax.fori_loop(0, n_tail, zero_block, 0)
        lax.fori_loop(0, n_tail, zero_block_wait, 0)


def _dispatch(x2, g_pre, shift, scale, pos, pad_start, pad_len, n_active, n_tiles, seq):
    t, d = x2.shape
    tm = TM_DSP
    assert TM_MOE % tm == 0
    per_b = seq // tm
    return pl.pallas_call(
        _dispatch_kernel,
        grid_spec=pltpu.PrefetchScalarGridSpec(
            num_scalar_prefetch=4,
            grid=(t // tm,),
            in_specs=[
                pl.BlockSpec((tm, d), lambda i, *_: (i, 0)),
                pl.BlockSpec((1, d), lambda i, *_: (0, 0)),
                pl.BlockSpec((1, 1, d), lambda i, *_: (i // per_b, 0, 0)),
                pl.BlockSpec((1, 1, d), lambda i, *_: (i // per_b, 0, 0)),
            ],
            out_specs=pl.BlockSpec(memory_space=pl.ANY),
            scratch_shapes=[pltpu.VMEM((2, tm, d), F32),
                            pltpu.SemaphoreType.DMA((2,)), pltpu.SemaphoreType.DMA(())],
        ),
        out_shape=jax.ShapeDtypeStruct((n_tiles * TM_MOE, d), F32),
        compiler_params=_cparams("arbitrary"),
        name="expert_dispatch",
    )(pos, pad_start, pad_len, n_active, x2, g_pre, shift, scale)


def _expert_first_kernel(ft_ref, has_ref, *refs):
    h_refs = refs[:FIRST_TILES]
    wg_ref, wu_ref, wd_ref, y_ref, qg_ref, qu_ref, qd_ref, hb_scr = refs[FIRST_TILES:]
    e, j = pl.program_id(0), pl.program_id(1)
    tm = h_refs[0].shape[0]
    wg, wu, wd = _mxu(wg_ref[0]), _mxu(wu_ref[0]), _mxu(wd_ref[0])
    qg_ref[0] = wg
    qu_ref[0] = wu
    qd_ref[0] = wd

    @pl.when(j == 0)
    def _():
        y_ref[...] = jnp.zeros(y_ref.shape, F32)
        for k, h_ref in enumerate(h_refs):
            hb_scr[k * tm:(k + 1) * tm] = _mxu(h_ref[...])

    @pl.when(has_ref[e] > 0)
    def _():
        y_ref[...] += _swiglu_tile(hb_scr[...], wg, wu, wd)


def _expert_first(hs, first_tile, has_rows, w_gate, w_up, w_down):
    n_rows, d = hs.shape
    n_exp, _, d_ff = w_gate.shape
    tf = TF_FIRST
    last_tile = n_rows // TM_MOE - 1
    col = lambda e, j, ft, has: (e, 0, j)
    rowb = lambda e, j, ft, has: (e, j, 0)
    cast = lambda a: jax.ShapeDtypeStruct(a.shape, MXU_DTYPE)

    def h_spec(k):
        return pl.BlockSpec((TM_MOE, d), lambda e, j, ft, has: (jnp.minimum(ft[e] + k, last_tile), 0),
                            pipeline_mode=pl.Buffered(1))

    return pl.pallas_call(
        _expert_first_kernel,
        grid_spec=pltpu.PrefetchScalarGridSpec(
            num_scalar_prefetch=2,
            grid=(n_exp, d_ff // tf),
            in_specs=[h_spec(k) for k in range(FIRST_TILES)] + [
                pl.BlockSpec((1, d, tf), col),
                pl.BlockSpec((1, d, tf), col),
                pl.BlockSpec((1, tf, d), rowb),
            ],
            out_specs=[
                pl.BlockSpec((FIRST_TILES * TM_MOE, d), lambda e, j, ft, has: (e, 0)),
                pl.BlockSpec((1, d, tf), col),
                pl.BlockSpec((1, d, tf), col),
                pl.BlockSpec((1, tf, d), rowb),
            ],
            scratch_shapes=[pltpu.VMEM((FIRST_TILES * TM_MOE, d), MXU_DTYPE)],
        ),
        out_shape=[jax.ShapeDtypeStruct((n_exp * FIRST_TILES * TM_MOE, d), F32),
                   cast(w_gate), cast(w_up), cast(w_down)],
        compiler_params=_cparams("arbitrary", "arbitrary"),
        name="expert_first",
    )(first_tile, has_rows, *([hs] * FIRST_TILES), w_gate, w_up, w_down)


def _expert_kernel(te_ref, nact_ref, first_ref, fblk_ref, frozen_ref, we_ref, h_ref, yf_ref,
                   wg_ref, wu_ref, wd_ref, o_ref, hb_scr):
    i, j = pl.program_id(0), pl.program_id(1)
    compute = frozen_ref[i] == 0

    @pl.when(j == 0)
    def _():
        o_ref[...] = jnp.zeros(o_ref.shape, F32)

    @pl.when(compute & (j == 0))
    def _():
        hb_scr[...] = _mxu(h_ref[...])

    @pl.when(compute)
    def _():
        o_ref[...] += _swiglu_tile(hb_scr[...], wg_ref[0], wu_ref[0], wd_ref[0])

    @pl.when((j == pl.num_programs(1) - 1) & (first_ref[i] != 0))
    def _():
        o_ref[...] = yf_ref[...]


def _expert_ffn(hs, y_first, tile_expert, n_active, is_first, first_block, frozen, weight_expert,
                w_gate, w_up, w_down):
    n_rows, d = hs.shape
    d_ff = w_gate.shape[2]
    n_tiles = n_rows // TM_MOE
    n_ff = d_ff // TF_FFN

    def ff_step(i, j, frozen):
        return jnp.where(frozen[i] != 0, n_ff - 1, j)

    def row_tile(i, nact):
        return jnp.minimum(i, nact[0] - 1)

    return pl.pallas_call(
        _expert_kernel,
        grid_spec=pltpu.PrefetchScalarGridSpec(
            num_scalar_prefetch=6,
            grid=(n_tiles, n_ff),
            in_specs=[
                pl.BlockSpec((TM_MOE, d), lambda i, j, te, nact, fi, fb, fr, we: (row_tile(i, nact), 0)),
                pl.BlockSpec((TM_MOE, d), lambda i, j, te, nact, fi, fb, fr, we: (fb[i], 0),
                             pipeline_mode=pl.Buffered(1)),
                pl.BlockSpec((1, d, TF_FFN), lambda i, j, te, nact, fi, fb, fr, we: (we[i], 0, ff_step(i, j, fr))),
                pl.BlockSpec((1, d, TF_FFN), lambda i, j, te, nact, fi, fb, fr, we: (we[i], 0, ff_step(i, j, fr))),
                pl.BlockSpec((1, TF_FFN, d), lambda i, j, te, nact, fi, fb, fr, we: (we[i], ff_step(i, j, fr), 0)),
            ],
            out_specs=pl.BlockSpec((TM_MOE, d), lambda i, j, te, nact, fi, fb, fr, we: (i, 0)),
            scratch_shapes=[pltpu.VMEM((TM_MOE, d), MXU_DTYPE)],
        ),
        out_shape=jax.ShapeDtypeStruct((n_rows, d), F32),
        compiler_params=_cparams("arbitrary", "arbitrary"),
        name="expert_ffn",
    )(tile_expert, n_active, is_first, first_block, frozen, weight_expert, hs, y_first, w_gate, w_up, w_down)


def _combine_kernel(pos_ref, ys_hbm, gw_ref, x_ref, g_ref, gate_ref, o_ref, buf, sem):
    i = pl.program_id(0)
    n = pl.num_programs(0)
    tm = x_ref.shape[0]
    t_total = n * tm
    cur = i % 2

    def fetch(step, b):
        def one(r):
            for s in range(TOP_K):
                _row_copy(ys_hbm, pos_ref[s * t_total + step * tm + r], buf.at[b, s], r, sem.at[b]).start()

        _for_row_groups(tm, one)

    @pl.when(i == 0)
    def _():
        fetch(0, 0)

    @pl.when(i + 1 < n)
    def _():
        fetch(i + 1, 1 - cur)

    for s in range(TOP_K):
        pltpu.make_async_copy(ys_hbm.at[pl.ds(0, tm)], buf.at[cur, s], sem.at[cur]).wait()
    gw = gw_ref[...]
    y = gw[:, 0:1] * buf[cur, 0]
    for s in range(1, TOP_K):
        y = y + gw[:, s:s + 1] * buf[cur, s]
    o_ref[...] = x_ref[...] + gate_ref[0] * _rms_norm(y, g_ref[...])


def _combine(ys, pos, gate_w, x2, g_post, gate, seq):
    t, d = x2.shape
    tm = TM_CMB
    per_b = seq // tm
    return pl.pallas_call(
        _combine_kernel,
        grid_spec=pltpu.PrefetchScalarGridSpec(
            num_scalar_prefetch=1,
            grid=(t // tm,),
            in_specs=[
                pl.BlockSpec(memory_space=pl.ANY),
                pl.BlockSpec((tm, LANES), lambda i, pos: (i, 0)),
                pl.BlockSpec((tm, d), lambda i, pos: (i, 0)),
                pl.BlockSpec((1, d), lambda i, pos: (0, 0)),
                pl.BlockSpec((1, 1, d), lambda i, pos: (i // per_b, 0, 0)),
            ],
            out_specs=pl.BlockSpec((tm, d), lambda i, pos: (i, 0)),
            scratch_shapes=[pltpu.VMEM((2, TOP_K, tm, d), F32), pltpu.SemaphoreType.DMA((2,))],
        ),
        out_shape=jax.ShapeDtypeStruct((t, d), F32),
        compiler_params=_cparams("arbitrary"),
        name="expert_combine",
    )(pos, ys, gate_w, x2, g_post, gate)


def _moe_ffn(x2, gate_w, top, g_pre, shift, scale, w_gate, w_up, w_down, g_post, gate, seq):
    t = x2.shape[0]
    n_tiles = (TOP_K * t) // TM_MOE + N_EXPERTS
    (pos, tile_expert, n_active, pad_start, pad_len, first_tile, has_rows, is_first, first_block, frozen,
     weight_expert) = _routing_tables(top[:, :TOP_K], n_tiles)
    hs = _dispatch(x2, g_pre, shift, scale, pos, pad_start, pad_len, n_active, n_tiles, seq)
    y_first, q_gate, q_up, q_down = _expert_first(hs, first_tile, has_rows, w_gate, w_up, w_down)
    ys = _expert_ffn(hs, y_first, tile_expert, n_active, is_first, first_block, frozen, weight_expert,
                     q_gate, q_up, q_down)
    return _combine(ys, pos, gate_w, x2, g_post, gate, seq)


def kernel(x, c, positions, ln_mix_pre, ln_mix_post, ln_ffn_pre, ln_ffn_post, w_mod, b_mod, w_in, b_fgate, w_out, w_ffn_gate, w_ffn_up, w_ffn_down, w_router, b_router, w_exp_gate, w_exp_up, w_exp_down):
    bsz, seq, d = x.shape
    depth = w_mod.shape[0]
    n_qkv = 3 * N_HEADS * HEAD_DIM
    x2 = x.reshape(bsz * seq, d)
    mod = _modulation(c, w_mod, b_mod).reshape(depth, bsz, N_MOD, 1, d)
    cos, sin = _rope_tables(positions)
    w_in_b = _mxu(w_in)
    for layer in range(depth):
        sh_m, sc_m, g_m, sh_f, sc_f, g_f = (mod[layer, :, k] for k in range(N_MOD))
        row = lambda a: a[layer].reshape(1, d)
        w_f = jnp.pad(w_in_b[layer, :, n_qkv:], ((0, 0), (0, LANES - N_HEADS_FOX)))
        qkv, fg = _in_proj(x2, row(ln_mix_pre), sh_m, sc_m, w_in_b, layer, n_qkv, w_f, seq)
        qkv3 = qkv.reshape(bsz, seq, n_qkv)
        cum_f = _forget_cumsum(fg.reshape(bsz, seq, LANES), b_fgate[layer])
        o_dil = _dilated_attention(qkv3, cos, sin).reshape(bsz * seq, -1)
        o_moba = _moba_attention(qkv3, cos, sin).reshape(bsz * seq, -1)
        o_fox = _fox_attention(qkv3, cum_f).reshape(bsz * seq, -1)
        j = layer // 2
        attn = (o_dil, o_moba, o_fox, _mxu(w_out[layer]), x2, row(ln_mix_post), g_m, seq)
        if layer % 2 == 0:
            x2 = _out_proj(*attn)
            x2 = _dense_ffn(x2, row(ln_ffn_pre), sh_f, sc_f, w_ffn_gate[j], w_ffn_up[j], w_ffn_down[j],
                            row(ln_ffn_post), g_f, seq)
        else:
            route = (row(ln_ffn_pre), sh_f, sc_f) + _router_params(w_router[j], b_router[j])
            x2, gate_w, top = _out_proj(*attn, route=route)
            x2 = _moe_ffn(x2, gate_w, top, row(ln_ffn_pre), sh_f, sc_f,
                          w_exp_gate[j], w_exp_up[j], w_exp_down[j], row(ln_ffn_post), g_f, seq)
    return x2.reshape(bsz, seq, d)
```

```python
import functools

import numpy as np
import jax
import jax.numpy as jnp
from jax import lax
from jax.experimental import pallas as pl
from jax.experimental.pallas import tpu as pltpu

F32 = jnp.float32
MXU_DTYPE = jnp.bfloat16

HEAD_DIM = 128
N_HEADS = 16
N_HEADS_DIL = 6
N_HEADS_MOBA = 4
N_HEADS_FOX = 6
DILATED_BRANCHES = ((128, 1), (512, 4), (2048, 16))
MOBA_BLOCK = 256
MOBA_TOPK = 3
ROPE_THETA = 500000.0
ROT_DIM = HEAD_DIM // 4
ROT_HALF = ROT_DIM // 2
N_EXPERTS = 8
TOP_K = 2
N_MOD = 6
RMS_EPS = 1e-6
NEG_INF = -1e30
ATTN_SCALE = HEAD_DIM ** -0.5
LANES = 128

VMEM_LIMIT_BYTES = 56 * 1024 * 1024

TM_IN = 1024
TN_IN = 1536
TM_OUT = 512
TM_FFN = 512
TF_FFN = 1024
TM_DSP = 512
TF_FIRST = 256
FIRST_TILES = 2
DMA_UNROLL = 8
TM_MOE = 512
TM_CMB = 512
TQ_FOX = 256
HEADS_PER_STEP = 2
TQ_DIL = 128
TN_MOD = 1024


def _cparams(*sem):
    return pltpu.CompilerParams(dimension_semantics=sem, vmem_limit_bytes=VMEM_LIMIT_BYTES)


def _mxu(a):
    return a.astype(MXU_DTYPE)


def _dot(a, b):
    return jnp.dot(a, b, preferred_element_type=F32)


def _dot_nt(a, b):
    return lax.dot_general(a, b, (((1,), (1,)), ((), ())), preferred_element_type=F32)


def _split_hi_lo(a):
    hi = a.astype(MXU_DTYPE)
    lo = (a - hi.astype(F32)).astype(MXU_DTYPE)
    return hi, lo


def _rms_norm(x, g):
    ms = jnp.mean(x * x, axis=-1, keepdims=True)
    return x * lax.rsqrt(ms + RMS_EPS) * g


def _sigmoid(x):
    return 1.0 / (1.0 + jnp.exp(-x))


def _mod_kernel(c_ref, w_ref, b_ref, o_ref):
    c = c_ref[...]
    cond = c * _sigmoid(c)
    o_ref[0] = _dot(_mxu(cond), _mxu(w_ref[0])) + b_ref[0]


def _modulation(c, w_mod, b_mod):
    depth, d, n = w_mod.shape
    bsz = c.shape[0]
    rows = max(16, bsz)
    c_pad = jnp.pad(c, ((0, rows - bsz), (0, 0)))
    out = pl.pallas_call(
        _mod_kernel,
        grid=(depth, n // TN_MOD),
        in_specs=[
            pl.BlockSpec((rows, d), lambda l, j: (0, 0)),
            pl.BlockSpec((1, d, TN_MOD), lambda l, j: (l, 0, j)),
            pl.BlockSpec((1, 1, TN_MOD), lambda l, j: (l, 0, j)),
        ],
        out_specs=pl.BlockSpec((1, rows, TN_MOD), lambda l, j: (l, 0, j)),
        out_shape=jax.ShapeDtypeStruct((depth, rows, n), F32),
        compiler_params=_cparams("parallel", "parallel"),
        name="modulation",
    )(c_pad, w_mod, b_mod.reshape(depth, 1, n))
    return out[:, :bsz]


def _rope_table_kernel(pos_ref, freq_ref, cos_ref, sin_ref):
    ang = pos_ref[0].astype(F32) * freq_ref[...]
    lane = lax.broadcasted_iota(jnp.int32, ang.shape, 1)
    cos_ref[0] = jnp.where(lane < ROT_DIM, jnp.cos(ang), 1.0)
    sn = jnp.sin(ang)
    sin_ref[0] = jnp.where(lane < ROT_HALF, -sn, jnp.where(lane < ROT_DIM, sn, 0.0))


def _rope_tables(positions):
    bsz, seq = positions.shape
    inv_freq = ROPE_THETA ** (-np.arange(ROT_HALF, dtype=np.float32) / ROT_HALF)
    freq = np.zeros((1, LANES), np.float32)
    freq[0, :ROT_HALF] = inv_freq
    freq[0, ROT_HALF:ROT_DIM] = inv_freq
    tab = jax.ShapeDtypeStruct((bsz, seq, LANES), F32)
    return pl.pallas_call(
        _rope_table_kernel,
        grid=(bsz,),
        in_specs=[
            pl.BlockSpec((1, seq, 1), lambda b: (b, 0, 0)),
            pl.BlockSpec((1, LANES), lambda b: (0, 0)),
        ],
        out_specs=[pl.BlockSpec((1, seq, LANES), lambda b: (b, 0, 0))] * 2,
        out_shape=[tab, tab],
        compiler_params=_cparams("parallel"),
        name="rope_tables",
    )(positions.reshape(bsz, seq, 1), jnp.asarray(freq))


def _rope(t, cos, sin_signed):
    lane = lax.broadcasted_iota(jnp.int32, t.shape, 1)
    partner = jnp.where(lane < ROT_HALF,
                        pltpu.roll(t, LANES - ROT_HALF, 1),
                        pltpu.roll(t, ROT_HALF, 1))
    return t * cos + partner * sin_signed


def _in_proj_kernel(x_ref, g_ref, sh_ref, sc_ref, w_ref, wf_ref, qkv_ref, fg_ref, h_scr):
    @pl.when(pl.program_id(1) == 0)
    def _():
        h = _rms_norm(x_ref[...], g_ref[...]) * (1.0 + sc_ref[0]) + sh_ref[0]
        hb = _mxu(h)
        h_scr[...] = hb
        fg_ref[...] = _dot(hb, wf_ref[...])

    qkv_ref[...] = _dot(h_scr[...], w_ref[0]).astype(qkv_ref.dtype)


def _in_proj(x2, g, shift, scale, w_in, layer, n, w_f, seq):
    t, d = x2.shape
    per_b = seq // TM_IN
    bmap = lambda i, j: (i // per_b, 0, 0)
    return pl.pallas_call(
        _in_proj_kernel,
        grid=(t // TM_IN, n // TN_IN),
        in_specs=[
            pl.BlockSpec((TM_IN, d), lambda i, j: (i, 0)),
            pl.BlockSpec((1, d), lambda i, j: (0, 0)),
            pl.BlockSpec((1, 1, d), bmap),
            pl.BlockSpec((1, 1, d), bmap),
            pl.BlockSpec((1, d, TN_IN), lambda i, j: (layer, 0, j)),
            pl.BlockSpec((d, LANES), lambda i, j: (0, 0)),
        ],
        out_specs=[
            pl.BlockSpec((TM_IN, TN_IN), lambda i, j: (i, j)),
            pl.BlockSpec((TM_IN, LANES), lambda i, j: (i, 0)),
        ],
        out_shape=[
            jax.ShapeDtypeStruct((t, n), MXU_DTYPE),
            jax.ShapeDtypeStruct((t, LANES), F32),
        ],
        scratch_shapes=[pltpu.VMEM((TM_IN, d), MXU_DTYPE)],
        compiler_params=_cparams("parallel", "arbitrary"),
        name="in_proj",
    )(x2, g, shift, scale, w_in, w_f)


def _forget_kernel(fg_ref, b_ref, f_ref):
    z = fg_ref[0] + b_ref[...]
    log_f = jnp.minimum(z, 0.0) - jnp.log(1.0 + jnp.exp(-jnp.abs(z)))
    cum = log_f.T[0:8]
    seq = cum.shape[1]
    lane = lax.broadcasted_iota(jnp.int32, cum.shape, 1)
    shift = 1
    while shift < seq:
        cum = cum + jnp.where(lane >= shift, pltpu.roll(cum, shift, 1), 0.0)
        shift *= 2
    f_ref[0] = cum


def _forget_cumsum(fg3, b_fgate):
    bsz, seq, _ = fg3.shape
    b_pad = jnp.pad(b_fgate, (0, LANES - b_fgate.shape[0])).reshape(1, LANES)
    return pl.pallas_call(
        _forget_kernel,
        grid=(bsz,),
        in_specs=[
            pl.BlockSpec((1, seq, LANES), lambda b: (b, 0, 0)),
            pl.BlockSpec((1, LANES), lambda b: (0, 0)),
        ],
        out_specs=pl.BlockSpec((1, 8, seq), lambda b: (b, 0, 0)),
        out_shape=jax.ShapeDtypeStruct((bsz, 8, seq), F32),
        compiler_params=_cparams("parallel"),
        name="forget_cumsum",
    )(fg3, b_pad)


LOG2E = 1.4426950408889634
Q_SCALE = ATTN_SCALE * LOG2E


def _softmax_pv(pieces, v):
    m = pieces[0]
    for s in pieces[1:]:
        m = jnp.maximum(m, s)
    m = jnp.max(m, axis=-1, keepdims=True)
    ps = [jnp.exp2(s - m) for s in pieces]
    l = ps[0]
    for p in ps[1:]:
        l = l + p
    l = jnp.sum(l, axis=-1, keepdims=True)
    p_all = _mxu(ps[0]) if len(ps) == 1 else jnp.concatenate([_mxu(p) for p in ps], axis=1)
    return _dot(p_all, v) / l


def _causal_mask(s):
    row = lax.broadcasted_iota(jnp.int32, s.shape, 0)
    col = lax.broadcasted_iota(jnp.int32, s.shape, 1)
    return jnp.where(col <= row, s, NEG_INF)


def _qkv_specs(seq, head0):
    return [
        pl.BlockSpec((1, seq, HEAD_DIM), lambda b, h: (b, 0, head0 + h)),
        pl.BlockSpec((1, seq, HEAD_DIM), lambda b, h: (b, 0, N_HEADS + head0 + h)),
        pl.BlockSpec((1, seq, HEAD_DIM), lambda b, h: (b, 0, 2 * N_HEADS + head0 + h)),
    ]


def _rope_k_into(k_ref, cos_ref, sin_ref, k_scr, blk):
    seq = k_scr.shape[0]
    for hh in range(k_scr.shape[1] // HEAD_DIM):
        lanes = slice(hh * HEAD_DIM, (hh + 1) * HEAD_DIM)
        for j in range(seq // blk):
            rows = slice(j * blk, (j + 1) * blk)
            k_scr[rows, lanes] = _mxu(_rope(k_ref[0, rows, lanes].astype(F32), cos_ref[0, rows], sin_ref[0, rows]))


def _head_pair_specs(seq, head0):
    hp = HEADS_PER_STEP
    assert head0 % hp == 0 and N_HEADS % hp == 0
    return [pl.BlockSpec((1, seq, hp * HEAD_DIM),
                         lambda b, h, which=which: (b, 0, (which * N_HEADS + head0) // hp + h))
            for which in range(3)]


def _dilated_bias(tq):
    n_off = 2048 // tq
    off = jnp.arange(n_off, dtype=jnp.int32)[:, None, None] * tq
    delta = off + jnp.arange(tq, dtype=jnp.int32)[None, :, None] - jnp.arange(tq, dtype=jnp.int32)[None, None, :]
    count = jnp.zeros(delta.shape, F32)
    for window, dil in DILATED_BRANCHES:
        count = count + ((delta >= 0) & (delta % dil == 0) & (delta <= window)).astype(F32)
    return jnp.where(count > 0, jnp.log2(jnp.maximum(count, 1.0)), NEG_INF)


def _dilated_kernel(q_ref, k_ref, v_ref, cos_ref, sin_ref, bias_ref, o_ref, k_scr):
    tq = TQ_DIL
    seq = k_scr.shape[0]
    _rope_k_into(k_ref, cos_ref, sin_ref, k_scr, 512)

    def logits(hh, qi):
        rows = slice(qi * tq, (qi + 1) * tq)
        lanes = slice(hh * HEAD_DIM, (hh + 1) * HEAD_DIM)
        q = _rope(q_ref[0, rows, lanes].astype(F32), cos_ref[0, rows], sin_ref[0, rows])
        return _dot_nt(_mxu(q * Q_SCALE), k_scr[:(qi + 1) * tq, lanes])

    n_q = seq // tq
    order = [(hh, qi) for qi in range(n_q) for hh in range(k_scr.shape[1] // HEAD_DIM)]
    s_next = logits(*order[0])
    for step, (hh, qi) in enumerate(order):
        lanes = slice(hh * HEAD_DIM, (hh + 1) * HEAD_DIM)
        s = s_next
        if step + 1 < len(order):
            s_next = logits(*order[step + 1])
        pieces = [s[:, kj * tq:(kj + 1) * tq] + bias_ref[qi - kj] for kj in range(qi + 1)]
        o = _softmax_pv(pieces, v_ref[0, :(qi + 1) * tq, lanes])
        o_ref[0, qi * tq:(qi + 1) * tq, lanes] = o.astype(o_ref.dtype)


def _dilated_attention(qkv3, cos, sin):
    bsz, seq, _ = qkv3.shape
    assert seq == 2048, "dilated windows are laid out for a 2048-token sequence"
    bias = _dilated_bias(TQ_DIL)
    tab = pl.BlockSpec((1, seq, LANES), lambda b, h: (b, 0, 0))
    return pl.pallas_call(
        _dilated_kernel,
        grid=(bsz, N_HEADS_DIL // HEADS_PER_STEP),
        in_specs=_head_pair_specs(seq, 0) + [tab, tab,
                                             pl.BlockSpec(bias.shape, lambda b, h: (0, 0, 0))],
        out_specs=pl.BlockSpec((1, seq, HEADS_PER_STEP * HEAD_DIM), lambda b, h: (b, 0, h)),
        out_shape=jax.ShapeDtypeStruct((bsz, seq, N_HEADS_DIL * HEAD_DIM), MXU_DTYPE),
        scratch_shapes=[pltpu.VMEM((seq, HEADS_PER_STEP * HEAD_DIM), MXU_DTYPE)],
        compiler_params=_cparams("parallel", "parallel"),
        name="dilated_attention",
    )(qkv3, qkv3, qkv3, cos, sin, bias)


def _moba_kernel(q_ref, k_ref, v_ref, cos_ref, sin_ref, o_ref, k_scr, km_scr):
    blk = MOBA_BLOCK
    seq = k_scr.shape[0]
    n_blk = seq // blk
    km_scr[...] = jnp.zeros(km_scr.shape, F32)
    for j in range(n_blk):
        rows = slice(j * blk, (j + 1) * blk)
        kr = _rope(k_ref[0, rows].astype(F32), cos_ref[0, rows], sin_ref[0, rows])
        k_scr[rows] = _mxu(kr)
        km_scr[j:j + 1, :] = jnp.mean(kr, axis=0, keepdims=True)
    km_hi, km_lo = _split_hi_lo(km_scr[...])

    def logits(qi):
        rows = slice(qi * blk, (qi + 1) * blk)
        qf = _rope(q_ref[0, rows].astype(F32), cos_ref[0, rows], sin_ref[0, rows])
        s = _dot_nt(_mxu(qf * Q_SCALE), k_scr[:(qi + 1) * blk])
        if qi == 0:
            return s, None
        q_hi, q_lo = _split_hi_lo(qf)
        gate = (_dot_nt(km_hi, q_hi) + _dot_nt(km_hi, q_lo) + _dot_nt(km_lo, q_hi))[:n_blk]
        row = lax.broadcasted_iota(jnp.int32, gate.shape, 0)
        gate = jnp.where(row < qi, gate, NEG_INF)
        rank = jnp.zeros(gate.shape, F32)
        for jp in range(qi):
            g_jp = gate[jp:jp + 1, :]
            tie_ahead = jnp.where(row > jp, 1.0, 0.0)
            rank = rank + jnp.where(g_jp > gate, 1.0, 0.0) + jnp.where(g_jp == gate, tie_ahead, 0.0)
        bias_t = jnp.where(row < qi, jnp.where(rank < MOBA_TOPK, 0.0, NEG_INF), NEG_INF)
        bias_t = jnp.concatenate([bias_t, jnp.full((LANES - n_blk, blk), NEG_INF, F32)], axis=0)
        return s, bias_t.T

    nxt = logits(0)
    for qi in range(n_blk):
        s, sel_bias = nxt
        if qi + 1 < n_blk:
            nxt = logits(qi + 1)
        pieces = [s[:, kj * blk:(kj + 1) * blk] + sel_bias[:, kj:kj + 1] for kj in range(qi)]
        pieces.append(_causal_mask(s[:, qi * blk:(qi + 1) * blk]))
        o = _softmax_pv(pieces, v_ref[0, :(qi + 1) * blk])
        o_ref[0, qi * blk:(qi + 1) * blk] = o.astype(o_ref.dtype)


def _moba_attention(qkv3, cos, sin):
    bsz, seq, _ = qkv3.shape
    assert seq % MOBA_BLOCK == 0 and seq // MOBA_BLOCK <= LANES
    tab = pl.BlockSpec((1, seq, LANES), lambda b, h: (b, 0, 0))
    return pl.pallas_call(
        _moba_kernel,
        grid=(bsz, N_HEADS_MOBA),
        in_specs=_qkv_specs(seq, N_HEADS_DIL) + [tab, tab],
        out_specs=pl.BlockSpec((1, seq, HEAD_DIM), lambda b, h: (b, 0, h)),
        out_shape=jax.ShapeDtypeStruct((bsz, seq, N_HEADS_MOBA * HEAD_DIM), MXU_DTYPE),
        scratch_shapes=[pltpu.VMEM((seq, HEAD_DIM), MXU_DTYPE),
                        pltpu.VMEM((LANES, HEAD_DIM), F32)],
        compiler_params=_cparams("parallel", "parallel"),
        name="moba_attention",
    )(qkv3, qkv3, qkv3, cos, sin)


def _fox_kernel(q_ref, k_ref, v_ref, f_ref, o_ref):
    tq = TQ_FOX
    seq = q_ref.shape[1]
    n_q = seq // tq
    heads = q_ref.shape[2] // HEAD_DIM

    def logits(hh, qi):
        lanes = slice(hh * HEAD_DIM, (hh + 1) * HEAD_DIM)
        q = _mxu(q_ref[0, qi * tq:(qi + 1) * tq, lanes].astype(F32) * Q_SCALE)
        return _dot_nt(q, k_ref[0, :(qi + 1) * tq, lanes])

    order = [(hh, qi) for qi in range(n_q) for hh in range(heads)]
    s_next = logits(*order[0])
    for step, (hh, qi) in enumerate(order):
        lanes = slice(hh * HEAD_DIM, (hh + 1) * HEAD_DIM)
        past = slice(0, (qi + 1) * tq)
        s = s_next - f_ref[0, hh, :, past] * LOG2E
        if step + 1 < len(order):
            s_next = logits(*order[step + 1])
        pieces = [s[:, kj * tq:(kj + 1) * tq] for kj in range(qi)]
        pieces.append(_causal_mask(s[:, qi * tq:(qi + 1) * tq]))
        o = _softmax_pv(pieces, v_ref[0, past, lanes])
        o_ref[0, qi * tq:(qi + 1) * tq, lanes] = o.astype(o_ref.dtype)


def _fox_attention(qkv3, cum_f):
    bsz, seq, _ = qkv3.shape
    hp = HEADS_PER_STEP
    head0 = N_HEADS_DIL + N_HEADS_MOBA
    assert head0 % hp == 0 and N_HEADS_FOX % hp == 0 and N_HEADS % hp == 0
    f4 = cum_f.reshape(bsz, cum_f.shape[1], 1, seq)
    blk = lambda which: pl.BlockSpec((1, seq, hp * HEAD_DIM),
                                     lambda b, h: (b, 0, (which * N_HEADS + head0) // hp + h))
    return pl.pallas_call(
        _fox_kernel,
        grid=(bsz, N_HEADS_FOX // hp),
        in_specs=[blk(0), blk(1), blk(2), pl.BlockSpec((1, hp, 1, seq), lambda b, h: (b, h, 0, 0))],
        out_specs=pl.BlockSpec((1, seq, hp * HEAD_DIM), lambda b, h: (b, 0, h)),
        out_shape=jax.ShapeDtypeStruct((bsz, seq, N_HEADS_FOX * HEAD_DIM), MXU_DTYPE),
        compiler_params=_cparams("parallel", "parallel"),
        name="fox_attention",
    )(qkv3, qkv3, qkv3, f4)


def _out_proj_kernel(od_ref, om_ref, of_ref, wd_ref, wm_ref, wf_ref, x_ref, g_ref, gate_ref, *rest):
    y = _dot(od_ref[...], wd_ref[...]) + _dot(om_ref[...], wm_ref[...]) + _dot(of_ref[...], wf_ref[...])
    x_new = x_ref[...] + gate_ref[0] * _rms_norm(y, g_ref[...])
    if len(rest) == 1:
        rest[0][...] = x_new
        return
    gpre_ref, sh_ref, sc_ref, whi_ref, wlo_ref, br_ref, o_ref, gw_ref, gi_ref = rest
    o_ref[...] = x_new
    h = _rms_norm(x_new, gpre_ref[...]) * (1.0 + sc_ref[0]) + sh_ref[0]
    _route(h, whi_ref, wlo_ref, br_ref, gw_ref, gi_ref)


def _out_proj(o_dil, o_moba, o_fox, w_out, x2, g, gate, seq, route=None):
    t, d = x2.shape
    nd, nm = o_dil.shape[1], o_moba.shape[1]
    w_d, w_m, w_f = w_out[:nd], w_out[nd:nd + nm], w_out[nd + nm:]
    per_b = seq // TM_OUT
    row = lambda i: (i, 0)
    fixed = lambda i: (0, 0)
    bmap = lambda i: (i // per_b, 0, 0)
    in_specs = [
        pl.BlockSpec((TM_OUT, nd), row),
        pl.BlockSpec((TM_OUT, nm), row),
        pl.BlockSpec((TM_OUT, o_fox.shape[1]), row),
        pl.BlockSpec(w_d.shape, fixed),
        pl.BlockSpec(w_m.shape, fixed),
        pl.BlockSpec(w_f.shape, fixed),
        pl.BlockSpec((TM_OUT, d), row),
        pl.BlockSpec((1, d), fixed),
        pl.BlockSpec((1, 1, d), bmap),
    ]
    out_specs = [pl.BlockSpec((TM_OUT, d), row)]
    out_shape = [jax.ShapeDtypeStruct((t, d), F32)]
    args = [o_dil, o_moba, o_fox, w_d, w_m, w_f, x2, g, gate]
    if route is not None:
        in_specs += [pl.BlockSpec((1, d), fixed), pl.BlockSpec((1, 1, d), bmap), pl.BlockSpec((1, 1, d), bmap),
                     pl.BlockSpec((d, LANES), fixed), pl.BlockSpec((d, LANES), fixed),
                     pl.BlockSpec((1, LANES), fixed)]
        out_specs += [pl.BlockSpec((TM_OUT, LANES), row)] * 2
        out_shape += [jax.ShapeDtypeStruct((t, LANES), F32), jax.ShapeDtypeStruct((t, LANES), jnp.int32)]
        args += list(route)
    out = pl.pallas_call(
        _out_proj_kernel,
        grid=(t // TM_OUT,),
        in_specs=in_specs,
        out_specs=out_specs,
        out_shape=out_shape,
        compiler_params=_cparams("parallel"),
        name="out_proj",
    )(*args)
    return out[0] if route is None else out


def _swiglu_tile(h, wg, wu, wd):
    g = _dot(h, wg)
    u = _dot(h, wu)
    return _dot(_mxu(g * _sigmoid(g) * u), wd)


def _ffn_first_kernel(x_ref, gpre_ref, sh_ref, sc_ref, wg_ref, wu_ref, wd_ref, gpost_ref, gate_ref,
                      o_ref, qg_ref, qu_ref, qd_ref, h_scr):
    j = pl.program_id(0)
    wg, wu, wd = _mxu(wg_ref[...]), _mxu(wu_ref[...]), _mxu(wd_ref[...])
    qg_ref[...] = wg
    qu_ref[...] = wu
    qd_ref[...] = wd

    @pl.when(j == 0)
    def _():
        h = _rms_norm(x_ref[...], gpre_ref[...]) * (1.0 + sc_ref[0]) + sh_ref[0]
        h_scr[...] = _mxu(h)
        o_ref[...] = jnp.zeros(o_ref.shape, F32)

    o_ref[...] += _swiglu_tile(h_scr[...], wg, wu, wd)

    @pl.when(j == pl.num_programs(0) - 1)
    def _():
        o_ref[...] = x_ref[...] + gate_ref[0] * _rms_norm(o_ref[...], gpost_ref[...])


def _dense_first(x2, g_pre, shift, scale, w_gate, w_up, w_down, g_post, gate, seq):
    t, d = x2.shape
    d_ff = w_gate.shape[1]
    tm, tf = FIRST_TILES * TM_FFN, TF_FIRST
    assert seq % tm == 0
    fixed = lambda j: (0, 0)
    bfix = lambda j: (0, 0, 0)
    once = dict(pipeline_mode=pl.Buffered(1))
    cast = lambda a: jax.ShapeDtypeStruct(a.shape, MXU_DTYPE)
    return pl.pallas_call(
        _ffn_first_kernel,
        grid=(d_ff // tf,),
        in_specs=[
            pl.BlockSpec((tm, d), fixed, **once),
            pl.BlockSpec((1, d), fixed),
            pl.BlockSpec((1, 1, d), bfix),
            pl.BlockSpec((1, 1, d), bfix),
            pl.BlockSpec((d, tf), lambda j: (0, j)),
            pl.BlockSpec((d, tf), lambda j: (0, j)),
            pl.BlockSpec((tf, d), lambda j: (j, 0)),
            pl.BlockSpec((1, d), fixed),
            pl.BlockSpec((1, 1, d), bfix),
        ],
        out_specs=[
            pl.BlockSpec((tm, d), fixed),
            pl.BlockSpec((d, tf), lambda j: (0, j)),
            pl.BlockSpec((d, tf), lambda j: (0, j)),
            pl.BlockSpec((tf, d), lambda j: (j, 0)),
        ],
        out_shape=[jax.ShapeDtypeStruct((tm, d), F32), cast(w_gate), cast(w_up), cast(w_down)],
        scratch_shapes=[pltpu.VMEM((tm, d), MXU_DTYPE)],
        compiler_params=_cparams("arbitrary"),
        name="dense_first",
    )(x2, g_pre, shift, scale, w_gate, w_up, w_down, g_post, gate)


def _ffn_kernel(x_ref, gpre_ref, sh_ref, sc_ref, wg_ref, wu_ref, wd_ref, gpost_ref, gate_ref, first_ref,
                o_ref, h_scr, acc_scr):
    i, j = pl.program_id(0), pl.program_id(1)
    done = i < FIRST_TILES

    @pl.when(jnp.logical_not(done) & (j == 0))
    def _():
        h = _rms_norm(x_ref[...], gpre_ref[...]) * (1.0 + sc_ref[0]) + sh_ref[0]
        h_scr[...] = _mxu(h)
        acc_scr[...] = jnp.zeros(acc_scr.shape, F32)

    @pl.when(jnp.logical_not(done))
    def _():
        acc_scr[...] += _swiglu_tile(h_scr[...], wg_ref[...], wu_ref[...], wd_ref[...])

    last = j == pl.num_programs(1) - 1

    @pl.when(jnp.logical_not(done) & last)
    def _():
        o_ref[...] = x_ref[...] + gate_ref[0] * _rms_norm(acc_scr[...], gpost_ref[...])

    @pl.when(done & last)
    def _():
        o_ref[...] = first_ref[...]


def _dense_ffn(x2, g_pre, shift, scale, w_gate, w_up, w_down, g_post, gate, seq):
    t, d = x2.shape
    d_ff = w_gate.shape[1]
    x_first, q_gate, q_up, q_down = _dense_first(x2, g_pre, shift, scale, w_gate, w_up, w_down,
                                                 g_post, gate, seq)
    per_b = seq // TM_FFN
    row = lambda i, j: (i, 0)
    fixed = lambda i, j: (0, 0)
    bmap = lambda i, j: (i // per_b, 0, 0)
    ff = lambda i, j: jnp.where(i < FIRST_TILES, 0, j)
    return pl.pallas_call(
        _ffn_kernel,
        grid=(t // TM_FFN, d_ff // TF_FFN),
        in_specs=[
            pl.BlockSpec((TM_FFN, d), row),
            pl.BlockSpec((1, d), fixed),
            pl.BlockSpec((1, 1, d), bmap),
            pl.BlockSpec((1, 1, d), bmap),
            pl.BlockSpec((d, TF_FFN), lambda i, j: (0, ff(i, j))),
            pl.BlockSpec((d, TF_FFN), lambda i, j: (0, ff(i, j))),
            pl.BlockSpec((TF_FFN, d), lambda i, j: (ff(i, j), 0)),
            pl.BlockSpec((1, d), fixed),
            pl.BlockSpec((1, 1, d), bmap),
            pl.BlockSpec((TM_FFN, d), lambda i, j: (jnp.minimum(i, FIRST_TILES - 1), 0),
                         pipeline_mode=pl.Buffered(1)),
        ],
        out_specs=pl.BlockSpec((TM_FFN, d), row),
        out_shape=jax.ShapeDtypeStruct((t, d), F32),
        scratch_shapes=[pltpu.VMEM((TM_FFN, d), MXU_DTYPE), pltpu.VMEM((TM_FFN, d), F32)],
        compiler_params=_cparams("arbitrary", "arbitrary"),
        name="dense_ffn",
    )(x2, g_pre, shift, scale, q_gate, q_up, q_down, g_post, gate, x_first)


def _route(h, whi_ref, wlo_ref, br_ref, gw_ref, gi_ref):
    h_hi, h_lo = _split_hi_lo(h)
    logits = (_dot(h_hi, whi_ref[...]) + _dot(h_lo, whi_ref[...]) + _dot(h_hi, wlo_ref[...])
              + br_ref[...])
    lane = lax.broadcasted_iota(jnp.int32, logits.shape, 1)
    logits = jnp.where(lane < N_EXPERTS, logits, NEG_INF)
    m1 = jnp.max(logits, axis=-1, keepdims=True)
    i1 = jnp.min(jnp.where(logits == m1, lane, LANES), axis=-1, keepdims=True)
    rest = jnp.where(lane == i1, NEG_INF, logits)
    m2 = jnp.max(rest, axis=-1, keepdims=True)
    i2 = jnp.min(jnp.where(rest == m2, lane, LANES), axis=-1, keepdims=True)
    e2 = jnp.exp(m2 - m1)
    p1 = 1.0 / (1.0 + e2)
    p2 = e2 * p1
    gw_ref[...] = jnp.where(lane == 0, p1, jnp.where(lane == 1, p2, 0.0))
    gi_ref[...] = jnp.where(lane == 0, i1, jnp.where(lane == 1, i2, 0))


def _router_params(w_router, b_router):
    w_pad = jnp.pad(w_router, ((0, 0), (0, LANES - N_EXPERTS)))
    w_hi = w_pad.astype(MXU_DTYPE)
    w_lo = (w_pad - w_hi.astype(F32)).astype(MXU_DTYPE)
    b_pad = jnp.pad(b_router, (0, LANES - N_EXPERTS)).reshape(1, LANES)
    return w_hi, w_lo, b_pad


def _routing_tables(top_idx, n_tiles):
    flat_e = top_idx.T.reshape(-1)
    experts = jnp.arange(N_EXPERTS, dtype=jnp.int32)
    onehot = (flat_e[:, None] == experts[None, :]).astype(jnp.int32)
    csum = jnp.cumsum(onehot, axis=0)
    rank = jnp.sum(onehot * csum, axis=1) - 1
    counts = csum[-1]
    padded = ((counts + TM_MOE - 1) // TM_MOE) * TM_MOE
    ends = jnp.cumsum(padded)
    starts = ends - padded
    pos = (jnp.sum(onehot * starts[None, :], axis=1) + rank).astype(jnp.int32)
    tile_start = jnp.arange(n_tiles, dtype=jnp.int32) * TM_MOE
    tile_expert = jnp.minimum(jnp.sum((tile_start[:, None] >= ends[None, :]).astype(jnp.int32), axis=1),
                              N_EXPERTS - 1).astype(jnp.int32)
    n_active = (ends[-1] // TM_MOE).astype(jnp.int32).reshape(1)
    pad_start = (starts + counts).astype(jnp.int32)
    pad_len = (padded - counts).astype(jnp.int32)
    tiles = jnp.arange(n_tiles, dtype=jnp.int32)
    first_tile = (starts // TM_MOE).astype(jnp.int32)
    has_rows = (counts > 0).astype(jnp.int32)
    active = tiles < n_active[0]
    first_of_tile = jnp.sum(jnp.where(tile_expert[:, None] == experts[None, :], first_tile[None, :], 0), axis=1)
    nth_of_expert = tiles - first_of_tile
    is_first = active & (nth_of_expert < FIRST_TILES)
    first_block = tile_expert * FIRST_TILES + jnp.clip(nth_of_expert, 0, FIRST_TILES - 1)
    frozen = is_first | ~active
    prev_live = jnp.max(jnp.where((tiles[None, :] <= tiles[:, None]) & ~frozen[None, :], tiles[None, :], -1),
                        axis=1)
    weight_expert = jnp.sum(jnp.where(tiles[None, :] == jnp.maximum(prev_live, 0)[:, None],
                                      tile_expert[None, :], 0), axis=1)
    return (pos, tile_expert, n_active, pad_start, pad_len, first_tile, has_rows,
            is_first.astype(jnp.int32), first_block.astype(jnp.int32), frozen.astype(jnp.int32),
            weight_expert.astype(jnp.int32))


def _row_copy(src, src_row, dst, dst_row, sem):
    return pltpu.make_async_copy(src.at[pl.ds(src_row, 1)], dst.at[pl.ds(dst_row, 1)], sem)


def _for_row_groups(n_rows, body):
    def trip(g, c):
        r0 = pl.multiple_of(g * DMA_UNROLL, DMA_UNROLL)
        for k in range(DMA_UNROLL):
            body(r0 + k)
        return c

    lax.fori_loop(0, n_rows // DMA_UNROLL, trip, 0)


def _dispatch_kernel(pos_ref, pstart_ref, plen_ref, nact_ref, x_ref, g_ref, sh_ref, sc_ref, hs_hbm,
                     hbuf, sem, zsem):
    i = pl.program_id(0)
    n = pl.num_programs(0)
    tm = x_ref.shape[0]
    t_total = n * tm
    slot = i % 2

    def wait_slot(s):
        for _ in range(TOP_K):
            pltpu.make_async_copy(hbuf.at[s], hs_hbm.at[pl.ds(0, tm)], sem.at[s]).wait()

    @pl.when(i >= 2)
    def _():
        wait_slot(slot)

    hbuf[slot] = _rms_norm(x_ref[...], g_ref[...]) * (1.0 + sc_ref[0]) + sh_ref[0]

    def send(r):
        for s in range(TOP_K):
            _row_copy(hbuf.at[slot], r, hs_hbm, pos_ref[s * t_total + i * tm + r], sem.at[slot]).start()

    _for_row_groups(tm, send)

    @pl.when(i == n - 1)
    def _():
        wait_slot(slot)

        @pl.when(n >= 2)
        def _():
            wait_slot(1 - slot)

        zeros = hbuf.at[0]
        zeros[...] = jnp.zeros(zeros.shape, F32)
        for e in range(N_EXPERTS):
            def zero_row(k, c):
                _row_copy(zeros, 0, hs_hbm, pstart_ref[e] + k, zsem).start()
                return c

            def zero_row_wait(k, c):
                _row_copy(zeros, 0, hs_hbm, 0, zsem).wait()
                return c

            lax.fori_loop(0, plen_ref[e], zero_row, 0)
            lax.fori_loop(0, plen_ref[e], zero_row_wait, 0)

        tail0 = nact_ref[0] * TM_MOE
        n_tail = (hs_hbm.shape[0] - tail0) // tm

        def zero_block(k, c):
            pltpu.make_async_copy(zeros, hs_hbm.at[pl.ds(tail0 + k * tm, tm)], zsem).start()
            return c

        def zero_block_wait(k, c):
            pltpu.make_async_copy(zeros, hs_hbm.at[pl.ds(0, tm)], zsem).wait()
            return c

        lax.fori_loop(0, n_tail, zero_block, 0)
        lax.fori_loop(0, n_tail, zero_block_wait, 0)


def _dispatch(x2, g_pre, shift, scale, pos, pad_start, pad_len, n_active, n_tiles, seq):
    t, d = x2.shape
    tm = TM_DSP
    assert TM_MOE % tm == 0
    per_b = seq // tm
    return pl.pallas_call(
        _dispatch_kernel,
        grid_spec=pltpu.PrefetchScalarGridSpec(
            num_scalar_prefetch=4,
            grid=(t // tm,),
            in_specs=[
                pl.BlockSpec((tm, d), lambda i, *_: (i, 0)),
                pl.BlockSpec((1, d), lambda i, *_: (0, 0)),
                pl.BlockSpec((1, 1, d), lambda i, *_: (i // per_b, 0, 0)),
                pl.BlockSpec((1, 1, d), lambda i, *_: (i // per_b, 0, 0)),
            ],
            out_specs=pl.BlockSpec(memory_space=pl.ANY),
            scratch_shapes=[pltpu.VMEM((2, tm, d), F32),
                            pltpu.SemaphoreType.DMA((2,)), pltpu.SemaphoreType.DMA(())],
        ),
        out_shape=jax.ShapeDtypeStruct((n_tiles * TM_MOE, d), F32),
        compiler_params=_cparams("arbitrary"),
        name="expert_dispatch",
    )(pos, pad_start, pad_len, n_active, x2, g_pre, shift, scale)


def _expert_first_kernel(ft_ref, has_ref, *refs):
    h_refs = refs[:FIRST_TILES]
    wg_ref, wu_ref, wd_ref, y_ref, qg_ref, qu_ref, qd_ref, hb_scr = refs[FIRST_TILES:]
    e, j = pl.program_id(0), pl.program_id(1)
    tm = h_refs[0].shape[0]
    wg, wu, wd = _mxu(wg_ref[0]), _mxu(wu_ref[0]), _mxu(wd_ref[0])
    qg_ref[0] = wg
    qu_ref[0] = wu
    qd_ref[0] = wd

    @pl.when(j == 0)
    def _():
        y_ref[...] = jnp.zeros(y_ref.shape, F32)
        for k, h_ref in enumerate(h_refs):
            hb_scr[k * tm:(k + 1) * tm] = _mxu(h_ref[...])

    @pl.when(has_ref[e] > 0)
    def _():
        y_ref[...] += _swiglu_tile(hb_scr[...], wg, wu, wd)


def _expert_first(hs, first_tile, has_rows, w_gate, w_up, w_down):
    n_rows, d = hs.shape
    n_exp, _, d_ff = w_gate.shape
    tf = TF_FIRST
    last_tile = n_rows // TM_MOE - 1
    col = lambda e, j, ft, has: (e, 0, j)
    rowb = lambda e, j, ft, has: (e, j, 0)
    cast = lambda a: jax.ShapeDtypeStruct(a.shape, MXU_DTYPE)

    def h_spec(k):
        return pl.BlockSpec((TM_MOE, d), lambda e, j, ft, has: (jnp.minimum(ft[e] + k, last_tile), 0),
                            pipeline_mode=pl.Buffered(1))

    return pl.pallas_call(
        _expert_first_kernel,
        grid_spec=pltpu.PrefetchScalarGridSpec(
            num_scalar_prefetch=2,
            grid=(n_exp, d_ff // tf),
            in_specs=[h_spec(k) for k in range(FIRST_TILES)] + [
                pl.BlockSpec((1, d, tf), col),
                pl.BlockSpec((1, d, tf), col),
                pl.BlockSpec((1, tf, d), rowb),
            ],
            out_specs=[
                pl.BlockSpec((FIRST_TILES * TM_MOE, d), lambda e, j, ft, has: (e, 0)),
                pl.BlockSpec((1, d, tf), col),
                pl.BlockSpec((1, d, tf), col),
                pl.BlockSpec((1, tf, d), rowb),
            ],
            scratch_shapes=[pltpu.VMEM((FIRST_TILES * TM_MOE, d), MXU_DTYPE)],
        ),
        out_shape=[jax.ShapeDtypeStruct((n_exp * FIRST_TILES * TM_MOE, d), F32),
                   cast(w_gate), cast(w_up), cast(w_down)],
        compiler_params=_cparams("arbitrary", "arbitrary"),
        name="expert_first",
    )(first_tile, has_rows, *([hs] * FIRST_TILES), w_gate, w_up, w_down)


def _expert_kernel(te_ref, nact_ref, first_ref, fblk_ref, frozen_ref, we_ref, h_ref, yf_ref,
                   wg_ref, wu_ref, wd_ref, o_ref, hb_scr):
    i, j = pl.program_id(0), pl.program_id(1)
    compute = frozen_ref[i] == 0

    @pl.when(j == 0)
    def _():
        o_ref[...] = jnp.zeros(o_ref.shape, F32)

    @pl.when(compute & (j == 0))
    def _():
        hb_scr[...] = _mxu(h_ref[...])

    @pl.when(compute)
    def _():
        o_ref[...] += _swiglu_tile(hb_scr[...], wg_ref[0], wu_ref[0], wd_ref[0])

    @pl.when((j == pl.num_programs(1) - 1) & (first_ref[i] != 0))
    def _():
        o_ref[...] = yf_ref[...]


def _expert_ffn(hs, y_first, tile_expert, n_active, is_first, first_block, frozen, weight_expert,
                w_gate, w_up, w_down):
    n_rows, d = hs.shape
    d_ff = w_gate.shape[2]
    n_tiles = n_rows // TM_MOE
    n_ff = d_ff // TF_FFN

    def ff_step(i, j, frozen):
        return jnp.where(frozen[i] != 0, n_ff - 1, j)

    def row_tile(i, nact):
        return jnp.minimum(i, nact[0] - 1)

    return pl.pallas_call(
        _expert_kernel,
        grid_spec=pltpu.PrefetchScalarGridSpec(
            num_scalar_prefetch=6,
            grid=(n_tiles, n_ff),
            in_specs=[
                pl.BlockSpec((TM_MOE, d), lambda i, j, te, nact, fi, fb, fr, we: (row_tile(i, nact), 0)),
                pl.BlockSpec((TM_MOE, d), lambda i, j, te, nact, fi, fb, fr, we: (fb[i], 0),
                             pipeline_mode=pl.Buffered(1)),
                pl.BlockSpec((1, d, TF_FFN), lambda i, j, te, nact, fi, fb, fr, we: (we[i], 0, ff_step(i, j, fr))),
                pl.BlockSpec((1, d, TF_FFN), lambda i, j, te, nact, fi, fb, fr, we: (we[i], 0, ff_step(i, j, fr))),
                pl.BlockSpec((1, TF_FFN, d), lambda i, j, te, nact, fi, fb, fr, we: (we[i], ff_step(i, j, fr), 0)),
            ],
            out_specs=pl.BlockSpec((TM_MOE, d), lambda i, j, te, nact, fi, fb, fr, we: (i, 0)),
            scratch_shapes=[pltpu.VMEM((TM_MOE, d), MXU_DTYPE)],
        ),
        out_shape=jax.ShapeDtypeStruct((n_rows, d), F32),
        compiler_params=_cparams("arbitrary", "arbitrary"),
        name="expert_ffn",
    )(tile_expert, n_active, is_first, first_block, frozen, weight_expert, hs, y_first, w_gate, w_up, w_down)


def _combine_kernel(pos_ref, ys_hbm, gw_ref, x_ref, g_ref, gate_ref, o_ref, buf, sem):
    i = pl.program_id(0)
    n = pl.num_programs(0)
    tm = x_ref.shape[0]
    t_total = n * tm
    cur = i % 2

    def fetch(step, b):
        def one(r):
            for s in range(TOP_K):
                _row_copy(ys_hbm, pos_ref[s * t_total + step * tm + r], buf.at[b, s], r, sem.at[b]).start()

        _for_row_groups(tm, one)

    @pl.when(i == 0)
    def _():
        fetch(0, 0)

    @pl.when(i + 1 < n)
    def _():
        fetch(i + 1, 1 - cur)

    for s in range(TOP_K):
        pltpu.make_async_copy(ys_hbm.at[pl.ds(0, tm)], buf.at[cur, s], sem.at[cur]).wait()
    gw = gw_ref[...]
    y = gw[:, 0:1] * buf[cur, 0]
    for s in range(1, TOP_K):
        y = y + gw[:, s:s + 1] * buf[cur, s]
    o_ref[...] = x_ref[...] + gate_ref[0] * _rms_norm(y, g_ref[...])


def _combine(ys, pos, gate_w, x2, g_post, gate, seq):
    t, d = x2.shape
    tm = TM_CMB
    per_b = seq // tm
    return pl.pallas_call(
        _combine_kernel,
        grid_spec=pltpu.PrefetchScalarGridSpec(
            num_scalar_prefetch=1,
            grid=(t // tm,),
            in_specs=[
                pl.BlockSpec(memory_space=pl.ANY),
                pl.BlockSpec((tm, LANES), lambda i, pos: (i, 0)),
                pl.BlockSpec((tm, d), lambda i, pos: (i, 0)),
                pl.BlockSpec((1, d), lambda i, pos: (0, 0)),
                pl.BlockSpec((1, 1, d), lambda i, pos: (i // per_b, 0, 0)),
            ],
            out_specs=pl.BlockSpec((tm, d), lambda i, pos: (i, 0)),
            scratch_shapes=[pltpu.VMEM((2, TOP_K, tm, d), F32), pltpu.SemaphoreType.DMA((2,))],
        ),
        out_shape=jax.ShapeDtypeStruct((t, d), F32),
        compiler_params=_cparams("arbitrary"),
        name="expert_combine",
    )(pos, ys, gate_w, x2, g_post, gate)


def _moe_ffn(x2, gate_w, top, g_pre, shift, scale, w_gate, w_up, w_down, g_post, gate, seq):
    t = x2.shape[0]
    n_tiles = (TOP_K * t) // TM_MOE + N_EXPERTS
    (pos, tile_expert, n_active, pad_start, pad_len, first_tile, has_rows, is_first, first_block, frozen,
     weight_expert) = _routing_tables(top[:, :TOP_K], n_tiles)
    hs = _dispatch(x2, g_pre, shift, scale, pos, pad_start, pad_len, n_active, n_tiles, seq)
    y_first, q_gate, q_up, q_down = _expert_first(hs, first_tile, has_rows, w_gate, w_up, w_down)
    ys = _expert_ffn(hs, y_first, tile_expert, n_active, is_first, first_block, frozen, weight_expert,
                     q_gate, q_up, q_down)
    return _combine(ys, pos, gate_w, x2, g_post, gate, seq)


def kernel(x, c, positions, ln_mix_pre, ln_mix_post, ln_ffn_pre, ln_ffn_post, w_mod, b_mod, w_in, b_fgate, w_out, w_ffn_gate, w_ffn_up, w_ffn_down, w_router, b_router, w_exp_gate, w_exp_up, w_exp_down):
    bsz, seq, d = x.shape
    depth = w_mod.shape[0]
    n_qkv = 3 * N_HEADS * HEAD_DIM
    x2 = x.reshape(bsz * seq, d)
    mod = _modulation(c, w_mod, b_mod).reshape(depth, bsz, N_MOD, 1, d)
    cos, sin = _rope_tables(positions)
    w_in_b = _mxu(w_in)
    for layer in range(depth):
        sh_m, sc_m, g_m, sh_f, sc_f, g_f = (mod[layer, :, k] for k in range(N_MOD))
        row = lambda a: a[layer].reshape(1, d)
        w_f = jnp.pad(w_in_b[layer, :, n_qkv:], ((0, 0), (0, LANES - N_HEADS_FOX)))
        qkv, fg = _in_proj(x2, row(ln_mix_pre), sh_m, sc_m, w_in_b, layer, n_qkv, w_f, seq)
        qkv3 = qkv.reshape(bsz, seq, n_qkv)
        cum_f = _forget_cumsum(fg.reshape(bsz, seq, LANES), b_fgate[layer])
        o_dil = _dilated_attention(qkv3, cos, sin).reshape(bsz * seq, -1)
        o_moba = _moba_attention(qkv3, cos, sin).reshape(bsz * seq, -1)
        o_fox = _fox_attention(qkv3, cum_f).reshape(bsz * seq, -1)
        j = layer // 2
        attn = (o_dil, o_moba, o_fox, _mxu(w_out[layer]), x2, row(ln_mix_post), g_m, seq)
        if layer % 2 == 0:
            x2 = _out_proj(*attn)
            x2 = _dense_ffn(x2, row(ln_ffn_pre), sh_f, sc_f, w_ffn_gate[j], w_ffn_up[j], w_ffn_down[j],
                            row(ln_ffn_post), g_f, seq)
        else:
            route = (row(ln_ffn_pre), sh_f, sc_f) + _router_params(w_router[j], b_router[j])
            x2, gate_w, top = _out_proj(*attn, route=route)
            x2 = _moe_ffn(x2, gate_w, top, row(ln_ffn_pre), sh_f, sc_f,
                          w_exp_gate[j], w_exp_up[j], w_exp_down[j], row(ln_ffn_post), g_f, seq)
    return x2.reshape(bsz, seq, d)
```

```python
import functools

import numpy as np
import jax
import jax.numpy as jnp
from jax import lax
from jax.experimental import pallas as pl
from jax.experimental.pallas import tpu as pltpu

F32 = jnp.float32
MXU_DTYPE = jnp.bfloat16

HEAD_DIM = 128
N_HEADS = 16
N_HEADS_DIL = 6
N_HEADS_MOBA = 4
N_HEADS_FOX = 6
DILATED_BRANCHES = ((128, 1), (512, 4), (2048, 16))
MOBA_BLOCK = 256
MOBA_TOPK = 3
ROPE_THETA = 500000.0
ROT_DIM = HEAD_DIM // 4
ROT_HALF = ROT_DIM // 2
N_EXPERTS = 8
TOP_K = 2
N_MOD = 6
RMS_EPS = 1e-6
NEG_INF = -1e30
ATTN_SCALE = HEAD_DIM ** -0.5
LANES = 128

VMEM_LIMIT_BYTES = 56 * 1024 * 1024

TM_IN = 1024
TN_IN = 1536
TM_OUT = 512
TM_FFN = 512
TF_FFN = 1024
TM_DSP = 512
TF_FIRST = 256
FIRST_TILES = 2
DMA_UNROLL = 8
TM_MOE = 512
TM_CMB = 256
TQ_FOX = 256
HEADS_PER_STEP = 2
assert N_HEADS_DIL % HEADS_PER_STEP == 0 and N_HEADS_MOBA % HEADS_PER_STEP == 0 and N_HEADS_FOX % HEADS_PER_STEP == 0
TQ_DIL = 128
TN_MOD = 1024


def _cparams(*sem):
    return pltpu.CompilerParams(dimension_semantics=sem, vmem_limit_bytes=VMEM_LIMIT_BYTES)


def _mxu(a):
    return a.astype(MXU_DTYPE)


def _dot(a, b):
    return jnp.dot(a, b, preferred_element_type=F32)


def _dot_nt(a, b):
    return lax.dot_general(a, b, (((1,), (1,)), ((), ())), preferred_element_type=F32)


def _split_hi_lo(a):
    hi = a.astype(MXU_DTYPE)
    lo = (a - hi.astype(F32)).astype(MXU_DTYPE)
    return hi, lo


def _rms_norm(x, g):
    ms = jnp.mean(x * x, axis=-1, keepdims=True)
    return x * lax.rsqrt(ms + RMS_EPS) * g


def _sigmoid(x):
    return 1.0 / (1.0 + jnp.exp(-x))


def _mod_kernel(c_ref, w_ref, b_ref, o_ref):
    c = c_ref[...]
    cond = c * _sigmoid(c)
    o_ref[0] = _dot(_mxu(cond), _mxu(w_ref[0])) + b_ref[0]


def _modulation(c, w_mod, b_mod):
    depth, d, n = w_mod.shape
    bsz = c.shape[0]
    rows = max(16, bsz)
    c_pad = jnp.pad(c, ((0, rows - bsz), (0, 0)))
    out = pl.pallas_call(
        _mod_kernel,
        grid=(depth, n // TN_MOD),
        in_specs=[
            pl.BlockSpec((rows, d), lambda l, j: (0, 0)),
            pl.BlockSpec((1, d, TN_MOD), lambda l, j: (l, 0, j)),
            pl.BlockSpec((1, 1, TN_MOD), lambda l, j: (l, 0, j)),
        ],
        out_specs=pl.BlockSpec((1, rows, TN_MOD), lambda l, j: (l, 0, j)),
        out_shape=jax.ShapeDtypeStruct((depth, rows, n), F32),
        compiler_params=_cparams("parallel", "parallel"),
        name="modulation",
    )(c_pad, w_mod, b_mod.reshape(depth, 1, n))
    return out[:, :bsz]


def _rope_table_kernel(pos_ref, freq_ref, cos_ref, sin_ref):
    ang = pos_ref[0].astype(F32) * freq_ref[...]
    lane = lax.broadcasted_iota(jnp.int32, ang.shape, 1)
    cos_ref[0] = jnp.where(lane < ROT_DIM, jnp.cos(ang), 1.0)
    sn = jnp.sin(ang)
    sin_ref[0] = jnp.where(lane < ROT_HALF, -sn, jnp.where(lane < ROT_DIM, sn, 0.0))


def _rope_tables(positions):
    bsz, seq = positions.shape
    inv_freq = ROPE_THETA ** (-np.arange(ROT_HALF, dtype=np.float32) / ROT_HALF)
    freq = np.zeros((1, LANES), np.float32)
    freq[0, :ROT_HALF] = inv_freq
    freq[0, ROT_HALF:ROT_DIM] = inv_freq
    tab = jax.ShapeDtypeStruct((bsz, seq, LANES), F32)
    return pl.pallas_call(
        _rope_table_kernel,
        grid=(bsz,),
        in_specs=[
            pl.BlockSpec((1, seq, 1), lambda b: (b, 0, 0)),
            pl.BlockSpec((1, LANES), lambda b: (0, 0)),
        ],
        out_specs=[pl.BlockSpec((1, seq, LANES), lambda b: (b, 0, 0))] * 2,
        out_shape=[tab, tab],
        compiler_params=_cparams("parallel"),
        name="rope_tables",
    )(positions.reshape(bsz, seq, 1), jnp.asarray(freq))


def _rope(t, cos, sin_signed):
    lane = lax.broadcasted_iota(jnp.int32, t.shape, 1)
    partner = jnp.where(lane < ROT_HALF,
                        pltpu.roll(t, LANES - ROT_HALF, 1),
                        pltpu.roll(t, ROT_HALF, 1))
    return t * cos + partner * sin_signed


def _in_proj_kernel(x_ref, g_ref, sh_ref, sc_ref, w_ref, wf_ref, qkv_ref, fg_ref, h_scr):
    @pl.when(pl.program_id(1) == 0)
    def _():
        h = _rms_norm(x_ref[...], g_ref[...]) * (1.0 + sc_ref[0]) + sh_ref[0]
        hb = _mxu(h)
        h_scr[...] = hb
        fg_ref[...] = _dot(hb, wf_ref[...])

    qkv_ref[...] = _dot(h_scr[...], w_ref[0]).astype(qkv_ref.dtype)


def _in_proj(x2, g, shift, scale, w_in, layer, n, w_f, seq):
    t, d = x2.shape
    per_b = seq // TM_IN
    bmap = lambda i, j: (i // per_b, 0, 0)
    return pl.pallas_call(
        _in_proj_kernel,
        grid=(t // TM_IN, n // TN_IN),
        in_specs=[
            pl.BlockSpec((TM_IN, d), lambda i, j: (i, 0)),
            pl.BlockSpec((1, d), lambda i, j: (0, 0)),
            pl.BlockSpec((1, 1, d), bmap),
            pl.BlockSpec((1, 1, d), bmap),
            pl.BlockSpec((1, d, TN_IN), lambda i, j: (layer, 0, j)),
            pl.BlockSpec((d, LANES), lambda i, j: (0, 0)),
        ],
        out_specs=[
            pl.BlockSpec((TM_IN, TN_IN), lambda i, j: (i, j)),
            pl.BlockSpec((TM_IN, LANES), lambda i, j: (i, 0)),
        ],
        out_shape=[
            jax.ShapeDtypeStruct((t, n), MXU_DTYPE),
            jax.ShapeDtypeStruct((t, LANES), F32),
        ],
        scratch_shapes=[pltpu.VMEM((TM_IN, d), MXU_DTYPE)],
        compiler_params=_cparams("parallel", "arbitrary"),
        name="in_proj",
    )(x2, g, shift, scale, w_in, w_f)


def _forget_kernel(fg_ref, b_ref, f_ref):
    z = fg_ref[0] + b_ref[...]
    log_f = jnp.minimum(z, 0.0) - jnp.log(1.0 + jnp.exp(-jnp.abs(z)))
    cum = log_f.T[0:8]
    seq = cum.shape[1]
    lane = lax.broadcasted_iota(jnp.int32, cum.shape, 1)
    shift = 1
    while shift < seq:
        cum = cum + jnp.where(lane >= shift, pltpu.roll(cum, shift, 1), 0.0)
        shift *= 2
    f_ref[0] = cum


def _forget_cumsum(fg3, b_fgate):
    bsz, seq, _ = fg3.shape
    b_pad = jnp.pad(b_fgate, (0, LANES - b_fgate.shape[0])).reshape(1, LANES)
    return pl.pallas_call(
        _forget_kernel,
        grid=(bsz,),
        in_specs=[
            pl.BlockSpec((1, seq, LANES), lambda b: (b, 0, 0)),
            pl.BlockSpec((1, LANES), lambda b: (0, 0)),
        ],
        out_specs=pl.BlockSpec((1, 8, seq), lambda b: (b, 0, 0)),
        out_shape=jax.ShapeDtypeStruct((bsz, 8, seq), F32),
        compiler_params=_cparams("parallel"),
        name="forget_cumsum",
    )(fg3, b_pad)


LOG2E = 1.4426950408889634
Q_SCALE = ATTN_SCALE * LOG2E


def _softmax_pv(pieces, v):
    m = pieces[0]
    for s in pieces[1:]:
        m = jnp.maximum(m, s)
    m = jnp.max(m, axis=-1, keepdims=True)
    ps = [jnp.exp2(s - m) for s in pieces]
    l = ps[0]
    for p in ps[1:]:
        l = l + p
    l = jnp.sum(l, axis=-1, keepdims=True)
    p_all = _mxu(ps[0]) if len(ps) == 1 else jnp.concatenate([_mxu(p) for p in ps], axis=1)
    return _dot(p_all, v) / l


def _causal_mask(s):
    row = lax.broadcasted_iota(jnp.int32, s.shape, 0)
    col = lax.broadcasted_iota(jnp.int32, s.shape, 1)
    return jnp.where(col <= row, s, NEG_INF)


def _rope_k_into(k_ref, cos_ref, sin_ref, k_scr, blk):
    seq = k_scr.shape[0]
    for hh in range(k_scr.shape[1] // HEAD_DIM):
        lanes = slice(hh * HEAD_DIM, (hh + 1) * HEAD_DIM)
        for j in range(seq // blk):
            rows = slice(j * blk, (j + 1) * blk)
            k_scr[rows, lanes] = _mxu(_rope(k_ref[0, rows, lanes].astype(F32), cos_ref[0, rows], sin_ref[0, rows]))


def _head_pair_specs(seq, head0):
    hp = HEADS_PER_STEP
    assert head0 % hp == 0 and N_HEADS % hp == 0
    return [pl.BlockSpec((1, seq, hp * HEAD_DIM),
                         lambda b, h, which=which: (b, 0, (which * N_HEADS + head0) // hp + h))
            for which in range(3)]


def _dilated_bias(tq):
    n_off = 2048 // tq
    off = jnp.arange(n_off, dtype=jnp.int32)[:, None, None] * tq
    delta = off + jnp.arange(tq, dtype=jnp.int32)[None, :, None] - jnp.arange(tq, dtype=jnp.int32)[None, None, :]
    count = jnp.zeros(delta.shape, F32)
    for window, dil in DILATED_BRANCHES:
        count = count + ((delta >= 0) & (delta % dil == 0) & (delta <= window)).astype(F32)
    return jnp.where(count > 0, jnp.log2(jnp.maximum(count, 1.0)), NEG_INF)


def _dilated_kernel(q_ref, k_ref, v_ref, cos_ref, sin_ref, bias_ref, o_ref, k_scr):
    tq = TQ_DIL
    seq = k_scr.shape[0]
    _rope_k_into(k_ref, cos_ref, sin_ref, k_scr, 512)

    def logits(hh, qi):
        rows = slice(qi * tq, (qi + 1) * tq)
        lanes = slice(hh * HEAD_DIM, (hh + 1) * HEAD_DIM)
        q = _rope(q_ref[0, rows, lanes].astype(F32), cos_ref[0, rows], sin_ref[0, rows])
        return _dot_nt(_mxu(q * Q_SCALE), k_scr[:(qi + 1) * tq, lanes])

    n_q = seq // tq
    order = [(hh, qi) for qi in range(n_q) for hh in range(k_scr.shape[1] // HEAD_DIM)]
    s_next = logits(*order[0])
    for step, (hh, qi) in enumerate(order):
        lanes = slice(hh * HEAD_DIM, (hh + 1) * HEAD_DIM)
        s = s_next
        if step + 1 < len(order):
            s_next = logits(*order[step + 1])
        pieces = [s[:, kj * tq:(kj + 1) * tq] + bias_ref[qi - kj] for kj in range(qi + 1)]
        o = _softmax_pv(pieces, v_ref[0, :(qi + 1) * tq, lanes])
        o_ref[0, qi * tq:(qi + 1) * tq, lanes] = o.astype(o_ref.dtype)


def _dilated_attention(qkv3, cos, sin):
    bsz, seq, _ = qkv3.shape
    assert seq == 2048, "dilated windows are laid out for a 2048-token sequence"
    bias = _dilated_bias(TQ_DIL)
    tab = pl.BlockSpec((1, seq, LANES), lambda b, h: (b, 0, 0))
    return pl.pallas_call(
        _dilated_kernel,
        grid=(bsz, N_HEADS_DIL // HEADS_PER_STEP),
        in_specs=_head_pair_specs(seq, 0) + [tab, tab,
                                             pl.BlockSpec(bias.shape, lambda b, h: (0, 0, 0))],
        out_specs=pl.BlockSpec((1, seq, HEADS_PER_STEP * HEAD_DIM), lambda b, h: (b, 0, h)),
        out_shape=jax.ShapeDtypeStruct((bsz, seq, N_HEADS_DIL * HEAD_DIM), MXU_DTYPE),
        scratch_shapes=[pltpu.VMEM((seq, HEADS_PER_STEP * HEAD_DIM), MXU_DTYPE)],
        compiler_params=_cparams("parallel", "parallel"),
        name="dilated_attention",
    )(qkv3, qkv3, qkv3, cos, sin, bias)


def _moba_kernel(q_ref, k_ref, v_ref, cos_ref, sin_ref, o_ref, k_scr, km_scr):
    blk = MOBA_BLOCK
    seq = k_scr.shape[0]
    n_blk = seq // blk
    heads = k_scr.shape[1] // HEAD_DIM
    km_scr[...] = jnp.zeros(km_scr.shape, F32)
    for hh in range(heads):
        lanes = slice(hh * HEAD_DIM, (hh + 1) * HEAD_DIM)
        for j in range(n_blk):
            rows = slice(j * blk, (j + 1) * blk)
            kr = _rope(k_ref[0, rows, lanes].astype(F32), cos_ref[0, rows], sin_ref[0, rows])
            k_scr[rows, lanes] = _mxu(kr)
            km_scr[hh * LANES + j:hh * LANES + j + 1, :] = jnp.mean(kr, axis=0, keepdims=True)
    km_split = [_split_hi_lo(km_scr[hh * LANES:(hh + 1) * LANES]) for hh in range(heads)]

    def logits(hh, qi):
        rows = slice(qi * blk, (qi + 1) * blk)
        lanes = slice(hh * HEAD_DIM, (hh + 1) * HEAD_DIM)
        km_hi, km_lo = km_split[hh]
        qf = _rope(q_ref[0, rows, lanes].astype(F32), cos_ref[0, rows], sin_ref[0, rows])
        s = _dot_nt(_mxu(qf * Q_SCALE), k_scr[:(qi + 1) * blk, lanes])
        if qi == 0:
            return s, None
        q_hi, q_lo = _split_hi_lo(qf)
        gate = (_dot_nt(km_hi, q_hi) + _dot_nt(km_hi, q_lo) + _dot_nt(km_lo, q_hi))[:n_blk]
        row = lax.broadcasted_iota(jnp.int32, gate.shape, 0)
        gate = jnp.where(row < qi, gate, NEG_INF)
        rank = jnp.zeros(gate.shape, F32)
        for jp in range(qi):
            g_jp = gate[jp:jp + 1, :]
            tie_ahead = jnp.where(row > jp, 1.0, 0.0)
            rank = rank + jnp.where(g_jp > gate, 1.0, 0.0) + jnp.where(g_jp == gate, tie_ahead, 0.0)
        bias_t = jnp.where(row < qi, jnp.where(rank < MOBA_TOPK, 0.0, NEG_INF), NEG_INF)
        bias_t = jnp.concatenate([bias_t, jnp.full((LANES - n_blk, blk), NEG_INF, F32)], axis=0)
        return s, bias_t.T

    order = [(hh, qi) for qi in range(n_blk) for hh in range(heads)]
    nxt = logits(*order[0])
    for step, (hh, qi) in enumerate(order):
        lanes = slice(hh * HEAD_DIM, (hh + 1) * HEAD_DIM)
        s, sel_bias = nxt
        if step + 1 < len(order):
            nxt = logits(*order[step + 1])
        pieces = [s[:, kj * blk:(kj + 1) * blk] + sel_bias[:, kj:kj + 1] for kj in range(qi)]
        pieces.append(_causal_mask(s[:, qi * blk:(qi + 1) * blk]))
        o = _softmax_pv(pieces, v_ref[0, :(qi + 1) * blk, lanes])
        o_ref[0, qi * blk:(qi + 1) * blk, lanes] = o.astype(o_ref.dtype)


def _moba_attention(qkv3, cos, sin):
    bsz, seq, _ = qkv3.shape
    assert seq % MOBA_BLOCK == 0 and seq // MOBA_BLOCK <= LANES
    tab = pl.BlockSpec((1, seq, LANES), lambda b, h: (b, 0, 0))
    return pl.pallas_call(
        _moba_kernel,
        grid=(bsz, N_HEADS_MOBA // HEADS_PER_STEP),
        in_specs=_head_pair_specs(seq, N_HEADS_DIL) + [tab, tab],
        out_specs=pl.BlockSpec((1, seq, HEADS_PER_STEP * HEAD_DIM), lambda b, h: (b, 0, h)),
        out_shape=jax.ShapeDtypeStruct((bsz, seq, N_HEADS_MOBA * HEAD_DIM), MXU_DTYPE),
        scratch_shapes=[pltpu.VMEM((seq, HEADS_PER_STEP * HEAD_DIM), MXU_DTYPE),
                        pltpu.VMEM((HEADS_PER_STEP * LANES, HEAD_DIM), F32)],
        compiler_params=_cparams("parallel", "parallel"),
        name="moba_attention",
    )(qkv3, qkv3, qkv3, cos, sin)


def _fox_kernel(q_ref, k_ref, v_ref, f_ref, o_ref):
    tq = TQ_FOX
    seq = q_ref.shape[1]
    n_q = seq // tq
    heads = q_ref.shape[2] // HEAD_DIM

    def logits(hh, qi):
        lanes = slice(hh * HEAD_DIM, (hh + 1) * HEAD_DIM)
        q = _mxu(q_ref[0, qi * tq:(qi + 1) * tq, lanes].astype(F32) * Q_SCALE)
        return _dot_nt(q, k_ref[0, :(qi + 1) * tq, lanes])

    order = [(hh, qi) for qi in range(n_q) for hh in range(heads)]
    s_next = logits(*order[0])
    for step, (hh, qi) in enumerate(order):
        lanes = slice(hh * HEAD_DIM, (hh + 1) * HEAD_DIM)
        past = slice(0, (qi + 1) * tq)
        s = s_next - f_ref[0, hh, :, past] * LOG2E
        if step + 1 < len(order):
            s_next = logits(*order[step + 1])
        pieces = [s[:, kj * tq:(kj + 1) * tq] for kj in range(qi)]
        pieces.append(_causal_mask(s[:, qi * tq:(qi + 1) * tq]))
        o = _softmax_pv(pieces, v_ref[0, past, lanes])
        o_ref[0, qi * tq:(qi + 1) * tq, lanes] = o.astype(o_ref.dtype)


def _fox_attention(qkv3, cum_f):
    bsz, seq, _ = qkv3.shape
    hp = HEADS_PER_STEP
    f4 = cum_f.reshape(bsz, cum_f.shape[1], 1, seq)
    return pl.pallas_call(
        _fox_kernel,
        grid=(bsz, N_HEADS_FOX // hp),
        in_specs=_head_pair_specs(seq, N_HEADS_DIL + N_HEADS_MOBA)
        + [pl.BlockSpec((1, hp, 1, seq), lambda b, h: (b, h, 0, 0))],
        out_specs=pl.BlockSpec((1, seq, hp * HEAD_DIM), lambda b, h: (b, 0, h)),
        out_shape=jax.ShapeDtypeStruct((bsz, seq, N_HEADS_FOX * HEAD_DIM), MXU_DTYPE),
        compiler_params=_cparams("parallel", "parallel"),
        name="fox_attention",
    )(qkv3, qkv3, qkv3, f4)


def _out_proj_kernel(od_ref, om_ref, of_ref, wd_ref, wm_ref, wf_ref, x_ref, g_ref, gate_ref, *rest):
    y = _dot(od_ref[...], wd_ref[...]) + _dot(om_ref[...], wm_ref[...]) + _dot(of_ref[...], wf_ref[...])
    x_new = x_ref[...] + gate_ref[0] * _rms_norm(y, g_ref[...])
    if len(rest) == 1:
        rest[0][...] = x_new
        return
    gpre_ref, sh_ref, sc_ref, whi_ref, wlo_ref, br_ref, o_ref, gw_ref, gi_ref = rest
    o_ref[...] = x_new
    h = _rms_norm(x_new, gpre_ref[...]) * (1.0 + sc_ref[0]) + sh_ref[0]
    _route(h, whi_ref, wlo_ref, br_ref, gw_ref, gi_ref)


def _out_proj(o_dil, o_moba, o_fox, w_out, x2, g, gate, seq, route=None):
    t, d = x2.shape
    nd, nm = o_dil.shape[1], o_moba.shape[1]
    w_d, w_m, w_f = w_out[:nd], w_out[nd:nd + nm], w_out[nd + nm:]
    per_b = seq // TM_OUT
    row = lambda i: (i, 0)
    fixed = lambda i: (0, 0)
    bmap = lambda i: (i // per_b, 0, 0)
    in_specs = [
        pl.BlockSpec((TM_OUT, nd), row),
        pl.BlockSpec((TM_OUT, nm), row),
        pl.BlockSpec((TM_OUT, o_fox.shape[1]), row),
        pl.BlockSpec(w_d.shape, fixed),
        pl.BlockSpec(w_m.shape, fixed),
        pl.BlockSpec(w_f.shape, fixed),
        pl.BlockSpec((TM_OUT, d), row),
        pl.BlockSpec((1, d), fixed),
        pl.BlockSpec((1, 1, d), bmap),
    ]
    out_specs = [pl.BlockSpec((TM_OUT, d), row)]
    out_shape = [jax.ShapeDtypeStruct((t, d), F32)]
    args = [o_dil, o_moba, o_fox, w_d, w_m, w_f, x2, g, gate]
    if route is not None:
        in_specs += [pl.BlockSpec((1, d), fixed), pl.BlockSpec((1, 1, d), bmap), pl.BlockSpec((1, 1, d), bmap),
                     pl.BlockSpec((d, LANES), fixed), pl.BlockSpec((d, LANES), fixed),
                     pl.BlockSpec((1, LANES), fixed)]
        out_specs += [pl.BlockSpec((TM_OUT, LANES), row)] * 2
        out_shape += [jax.ShapeDtypeStruct((t, LANES), F32), jax.ShapeDtypeStruct((t, LANES), jnp.int32)]
        args += list(route)
    out = pl.pallas_call(
        _out_proj_kernel,
        grid=(t // TM_OUT,),
        in_specs=in_specs,
        out_specs=out_specs,
        out_shape=out_shape,
        compiler_params=_cparams("parallel"),
        name="out_proj",
    )(*args)
    return out[0] if route is None else out


def _swiglu_tile(h, wg, wu, wd):
    g = _dot(h, wg)
    u = _dot(h, wu)
    return _dot(_mxu(g * _sigmoid(g) * u), wd)


def _ffn_first_kernel(x_ref, gpre_ref, sh_ref, sc_ref, wg_ref, wu_ref, wd_ref, gpost_ref, gate_ref,
                      o_ref, qg_ref, qu_ref, qd_ref, h_scr):
    j = pl.program_id(0)
    wg, wu, wd = _mxu(wg_ref[...]), _mxu(wu_ref[...]), _mxu(wd_ref[...])
    qg_ref[...] = wg
    qu_ref[...] = wu
    qd_ref[...] = wd

    @pl.when(j == 0)
    def _():
        h = _rms_norm(x_ref[...], gpre_ref[...]) * (1.0 + sc_ref[0]) + sh_ref[0]
        h_scr[...] = _mxu(h)
        o_ref[...] = jnp.zeros(o_ref.shape, F32)

    o_ref[...] += _swiglu_tile(h_scr[...], wg, wu, wd)

    @pl.when(j == pl.num_programs(0) - 1)
    def _():
        o_ref[...] = x_ref[...] + gate_ref[0] * _rms_norm(o_ref[...], gpost_ref[...])


def _dense_first(x2, g_pre, shift, scale, w_gate, w_up, w_down, g_post, gate, seq):
    t, d = x2.shape
    d_ff = w_gate.shape[1]
    tm, tf = FIRST_TILES * TM_FFN, TF_FIRST
    assert seq % tm == 0
    fixed = lambda j: (0, 0)
    bfix = lambda j: (0, 0, 0)
    once = dict(pipeline_mode=pl.Buffered(1))
    cast = lambda a: jax.ShapeDtypeStruct(a.shape, MXU_DTYPE)
    return pl.pallas_call(
        _ffn_first_kernel,
        grid=(d_ff // tf,),
        in_specs=[
            pl.BlockSpec((tm, d), fixed, **once),
            pl.BlockSpec((1, d), fixed),
            pl.BlockSpec((1, 1, d), bfix),
            pl.BlockSpec((1, 1, d), bfix),
            pl.BlockSpec((d, tf), lambda j: (0, j)),
            pl.BlockSpec((d, tf), lambda j: (0, j)),
            pl.BlockSpec((tf, d), lambda j: (j, 0)),
            pl.BlockSpec((1, d), fixed),
            pl.BlockSpec((1, 1, d), bfix),
        ],
        out_specs=[
            pl.BlockSpec((tm, d), fixed),
            pl.BlockSpec((d, tf), lambda j: (0, j)),
            pl.BlockSpec((d, tf), lambda j: (0, j)),
            pl.BlockSpec((tf, d), lambda j: (j, 0)),
        ],
        out_shape=[jax.ShapeDtypeStruct((tm, d), F32), cast(w_gate), cast(w_up), cast(w_down)],
        scratch_shapes=[pltpu.VMEM((tm, d), MXU_DTYPE)],
        compiler_params=_cparams("arbitrary"),
        name="dense_first",
    )(x2, g_pre, shift, scale, w_gate, w_up, w_down, g_post, gate)


def _ffn_kernel(x_ref, gpre_ref, sh_ref, sc_ref, wg_ref, wu_ref, wd_ref, gpost_ref, gate_ref, first_ref,
                o_ref, h_scr, acc_scr):
    i, j = pl.program_id(0), pl.program_id(1)
    done = i < FIRST_TILES

    @pl.when(jnp.logical_not(done) & (j == 0))
    def _():
        h = _rms_norm(x_ref[...], gpre_ref[...]) * (1.0 + sc_ref[0]) + sh_ref[0]
        h_scr[...] = _mxu(h)
        acc_scr[...] = jnp.zeros(acc_scr.shape, F32)

    @pl.when(jnp.logical_not(done))
    def _():
        acc_scr[...] += _swiglu_tile(h_scr[...], wg_ref[...], wu_ref[...], wd_ref[...])

    last = j == pl.num_programs(1) - 1

    @pl.when(jnp.logical_not(done) & last)
    def _():
        o_ref[...] = x_ref[...] + gate_ref[0] * _rms_norm(acc_scr[...], gpost_ref[...])

    @pl.when(done & last)
    def _():
        o_ref[...] = first_ref[...]


def _dense_ffn(x2, g_pre, shift, scale, w_gate, w_up, w_down, g_post, gate, seq):
    t, d = x2.shape
    d_ff = w_gate.shape[1]
    x_first, q_gate, q_up, q_down = _dense_first(x2, g_pre, shift, scale, w_gate, w_up, w_down,
                                                 g_post, gate, seq)
    per_b = seq // TM_FFN
    row = lambda i, j: (i, 0)
    fixed = lambda i, j: (0, 0)
    bmap = lambda i, j: (i // per_b, 0, 0)
    ff = lambda i, j: jnp.where(i < FIRST_TILES, 0, j)
    return pl.pallas_call(
        _ffn_kernel,
        grid=(t // TM_FFN, d_ff // TF_FFN),
        in_specs=[
            pl.BlockSpec((TM_FFN, d), row),
            pl.BlockSpec((1, d), fixed),
            pl.BlockSpec((1, 1, d), bmap),
            pl.BlockSpec((1, 1, d), bmap),
            pl.BlockSpec((d, TF_FFN), lambda i, j: (0, ff(i, j))),
            pl.BlockSpec((d, TF_FFN), lambda i, j: (0, ff(i, j))),
            pl.BlockSpec((TF_FFN, d), lambda i, j: (ff(i, j), 0)),
            pl.BlockSpec((1, d), fixed),
            pl.BlockSpec((1, 1, d), bmap),
            pl.BlockSpec((TM_FFN, d), lambda i, j: (jnp.minimum(i, FIRST_TILES - 1), 0),
                         pipeline_mode=pl.Buffered(1)),
        ],
        out_specs=pl.BlockSpec((TM_FFN, d), row),
        out_shape=jax.ShapeDtypeStruct((t, d), F32),
        scratch_shapes=[pltpu.VMEM((TM_FFN, d), MXU_DTYPE), pltpu.VMEM((TM_FFN, d), F32)],
        compiler_params=_cparams("arbitrary", "arbitrary"),
        name="dense_ffn",
    )(x2, g_pre, shift, scale, q_gate, q_up, q_down, g_post, gate, x_first)


def _route(h, whi_ref, wlo_ref, br_ref, gw_ref, gi_ref):
    h_hi, h_lo = _split_hi_lo(h)
    logits = (_dot(h_hi, whi_ref[...]) + _dot(h_lo, whi_ref[...]) + _dot(h_hi, wlo_ref[...])
              + br_ref[...])
    lane = lax.broadcasted_iota(jnp.int32, logits.shape, 1)
    logits = jnp.where(lane < N_EXPERTS, logits, NEG_INF)
    m1 = jnp.max(logits, axis=-1, keepdims=True)
    i1 = jnp.min(jnp.where(logits == m1, lane, LANES), axis=-1, keepdims=True)
    rest = jnp.where(lane == i1, NEG_INF, logits)
    m2 = jnp.max(rest, axis=-1, keepdims=True)
    i2 = jnp.min(jnp.where(rest == m2, lane, LANES), axis=-1, keepdims=True)
    e2 = jnp.exp(m2 - m1)
    p1 = 1.0 / (1.0 + e2)
    p2 = e2 * p1
    gw_ref[...] = jnp.where(lane == 0, p1, jnp.where(lane == 1, p2, 0.0))
    gi_ref[...] = jnp.where(lane == 0, i1, jnp.where(lane == 1, i2, 0))


def _router_params(w_router, b_router):
    w_pad = jnp.pad(w_router, ((0, 0), (0, LANES - N_EXPERTS)))
    w_hi = w_pad.astype(MXU_DTYPE)
    w_lo = (w_pad - w_hi.astype(F32)).astype(MXU_DTYPE)
    b_pad = jnp.pad(b_router, (0, LANES - N_EXPERTS)).reshape(1, LANES)
    return w_hi, w_lo, b_pad


def _routing_tables(top_idx, n_tiles):
    flat_e = top_idx.T.reshape(-1)
    experts = jnp.arange(N_EXPERTS, dtype=jnp.int32)
    onehot = (flat_e[:, None] == experts[None, :]).astype(jnp.int32)
    csum = jnp.cumsum(onehot, axis=0)
    rank = jnp.sum(onehot * csum, axis=1) - 1
    counts = csum[-1]
    padded = ((counts + TM_MOE - 1) // TM_MOE) * TM_MOE
    ends = jnp.cumsum(padded)
    starts = ends - padded
    pos = (jnp.sum(onehot * starts[None, :], axis=1) + rank).astype(jnp.int32)
    tile_start = jnp.arange(n_tiles, dtype=jnp.int32) * TM_MOE
    tile_expert = jnp.minimum(jnp.sum((tile_start[:, None] >= ends[None, :]).astype(jnp.int32), axis=1),
                              N_EXPERTS - 1).astype(jnp.int32)
    n_active = (ends[-1] // TM_MOE).astype(jnp.int32).reshape(1)
    pad_start = (starts + counts).astype(jnp.int32)
    pad_len = (padded - counts).astype(jnp.int32)
    tiles = jnp.arange(n_tiles, dtype=jnp.int32)
    first_tile = (starts // TM_MOE).astype(jnp.int32)
    has_rows = (counts > 0).astype(jnp.int32)
    active = tiles < n_active[0]
    first_of_tile = jnp.sum(jnp.where(tile_expert[:, None] == experts[None, :], first_tile[None, :], 0), axis=1)
    nth_of_expert = tiles - first_of_tile
    is_first = active & (nth_of_expert < FIRST_TILES)
    first_block = tile_expert * FIRST_TILES + jnp.clip(nth_of_expert, 0, FIRST_TILES - 1)
    frozen = is_first | ~active
    prev_live = jnp.max(jnp.where((tiles[None, :] <= tiles[:, None]) & ~frozen[None, :], tiles[None, :], -1),
                        axis=1)
    weight_expert = jnp.sum(jnp.where(tiles[None, :] == jnp.maximum(prev_live, 0)[:, None],
                                      tile_expert[None, :], 0), axis=1)
    return (pos, tile_expert, n_active, pad_start, pad_len, first_tile, has_rows,
            is_first.astype(jnp.int32), first_block.astype(jnp.int32), frozen.astype(jnp.int32),
            weight_expert.astype(jnp.int32))


def _row_copy(src, src_row, dst, dst_row, sem):
    return pltpu.make_async_copy(src.at[pl.ds(src_row, 1)], dst.at[pl.ds(dst_row, 1)], sem)


def _for_row_groups(n_rows, body):
    def trip(g, c):
        r0 = pl.multiple_of(g * DMA_UNROLL, DMA_UNROLL)
        for k in range(DMA_UNROLL):
            body(r0 + k)
        return c

    lax.fori_loop(0, n_rows // DMA_UNROLL, trip, 0)


def _dispatch_kernel(pos_ref, pstart_ref, plen_ref, nact_ref, x_ref, g_ref, sh_ref, sc_ref, hs_hbm,
                     hbuf, sem, zsem):
    i = pl.program_id(0)
    n = pl.num_programs(0)
    tm = x_ref.shape[0]
    t_total = n * tm
    slot = i % 2

    def wait_slot(s):
        for _ in range(TOP_K):
            pltpu.make_async_copy(hbuf.at[s], hs_hbm.at[pl.ds(0, tm)], sem.at[s]).wait()

    @pl.when(i >= 2)
    def _():
        wait_slot(slot)

    hbuf[slot] = _rms_norm(x_ref[...], g_ref[...]) * (1.0 + sc_ref[0]) + sh_ref[0]

    def send(r):
        for s in range(TOP_K):
            _row_copy(hbuf.at[slot], r, hs_hbm, pos_ref[s * t_total + i * tm + r], sem.at[slot]).start()

    _for_row_groups(tm, send)

    @pl.when(i == n - 1)
    def _():
        wait_slot(slot)

        @pl.when(n >= 2)
        def _():
            wait_slot(1 - slot)

        zeros = hbuf.at[0]
        zeros[...] = jnp.zeros(zeros.shape, F32)
        for e in range(N_EXPERTS):
            def zero_row(k, c):
                _row_copy(zeros, 0, hs_hbm, pstart_ref[e] + k, zsem).start()
                return c

            def zero_row_wait(k, c):
                _row_copy(zeros, 0, hs_hbm, 0, zsem).wait()
                return c

            lax.fori_loop(0, plen_ref[e], zero_row, 0)
            lax.fori_loop(0, plen_ref[e], zero_row_wait, 0)

        tail0 = nact_ref[0] * TM_MOE
        n_tail = (hs_hbm.shape[0] - tail0) // tm

        def zero_block(k, c):
            pltpu.make_async_copy(zeros, hs_hbm.at[pl.ds(tail0 + k * tm, tm)], zsem).start()
            return c

        def zero_block_wait(k, c):
            pltpu.make_async_copy(zeros, hs_hbm.at[pl.ds(0, tm)], zsem).wait()
            return c

        lax.fori_loop(0, n_tail, zero_block, 0)
        lax.fori_loop(0, n_tail, zero_block_wait, 0)


def _dispatch(x2, g_pre, shift, scale, pos, pad_start, pad_len, n_active, n_tiles, seq):
    t, d = x2.shape
    tm = TM_DSP
    assert TM_MOE % tm == 0
    per_b = seq // tm
    return pl.pallas_call(
        _dispatch_kernel,
        grid_spec=pltpu.PrefetchScalarGridSpec(
            num_scalar_prefetch=4,
            grid=(t // tm,),
            in_specs=[
                pl.BlockSpec((tm, d), lambda i, *_: (i, 0)),
                pl.BlockSpec((1, d), lambda i, *_: (0, 0)),
                pl.BlockSpec((1, 1, d), lambda i, *_: (i // per_b, 0, 0)),
                pl.BlockSpec((1, 1, d), lambda i, *_: (i // per_b, 0, 0)),
            ],
            out_specs=pl.BlockSpec(memory_space=pl.ANY),
            scratch_shapes=[pltpu.VMEM((2, tm, d), F32),
                            pltpu.SemaphoreType.DMA((2,)), pltpu.SemaphoreType.DMA(())],
        ),
        out_shape=jax.ShapeDtypeStruct((n_tiles * TM_MOE, d), F32),
        compiler_params=_cparams("arbitrary"),
        name="expert_dispatch",
    )(pos, pad_start, pad_len, n_active, x2, g_pre, shift, scale)


def _expert_first_kernel(ft_ref, has_ref, *refs):
    h_refs = refs[:FIRST_TILES]
    wg_ref, wu_ref, wd_ref, y_ref, qg_ref, qu_ref, qd_ref, hb_scr = refs[FIRST_TILES:]
    e, j = pl.program_id(0), pl.program_id(1)
    tm = h_refs[0].shape[0]
    wg, wu, wd = _mxu(wg_ref[0]), _mxu(wu_ref[0]), _mxu(wd_ref[0])
    qg_ref[0] = wg
    qu_ref[0] = wu
    qd_ref[0] = wd

    @pl.when(j == 0)
    def _():
        y_ref[...] = jnp.zeros(y_ref.shape, F32)
        for k, h_ref in enumerate(h_refs):
            hb_scr[k * tm:(k + 1) * tm] = _mxu(h_ref[...])

    @pl.when(has_ref[e] > 0)
    def _():
        y_ref[...] += _swiglu_tile(hb_scr[...], wg, wu, wd)


def _expert_first(hs, first_tile, has_rows, w_gate, w_up, w_down):
    n_rows, d = hs.shape
    n_exp, _, d_ff = w_gate.shape
    tf = TF_FIRST
    last_tile = n_rows // TM_MOE - 1
    col = lambda e, j, ft, has: (e, 0, j)
    rowb = lambda e, j, ft, has: (e, j, 0)
    cast = lambda a: jax.ShapeDtypeStruct(a.shape, MXU_DTYPE)

    def h_spec(k):
        return pl.BlockSpec((TM_MOE, d), lambda e, j, ft, has: (jnp.minimum(ft[e] + k, last_tile), 0),
                            pipeline_mode=pl.Buffered(1))

    return pl.pallas_call(
        _expert_first_kernel,
        grid_spec=pltpu.PrefetchScalarGridSpec(
            num_scalar_prefetch=2,
            grid=(n_exp, d_ff // tf),
            in_specs=[h_spec(k) for k in range(FIRST_TILES)] + [
                pl.BlockSpec((1, d, tf), col),
                pl.BlockSpec((1, d, tf), col),
                pl.BlockSpec((1, tf, d), rowb),
            ],
            out_specs=[
                pl.BlockSpec((FIRST_TILES * TM_MOE, d), lambda e, j, ft, has: (e, 0)),
                pl.BlockSpec((1, d, tf), col),
                pl.BlockSpec((1, d, tf), col),
                pl.BlockSpec((1, tf, d), rowb),
            ],
            scratch_shapes=[pltpu.VMEM((FIRST_TILES * TM_MOE, d), MXU_DTYPE)],
        ),
        out_shape=[jax.ShapeDtypeStruct((n_exp * FIRST_TILES * TM_MOE, d), F32),
                   cast(w_gate), cast(w_up), cast(w_down)],
        compiler_params=_cparams("arbitrary", "arbitrary"),
        name="expert_first",
    )(first_tile, has_rows, *([hs] * FIRST_TILES), w_gate, w_up, w_down)


def _expert_kernel(te_ref, nact_ref, first_ref, fblk_ref, frozen_ref, we_ref, h_ref, yf_ref,
                   wg_ref, wu_ref, wd_ref, o_ref, hb_scr):
    i, j = pl.program_id(0), pl.program_id(1)
    compute = frozen_ref[i] == 0

    @pl.when(j == 0)
    def _():
        o_ref[...] = jnp.zeros(o_ref.shape, F32)

    @pl.when(compute & (j == 0))
    def _():
        hb_scr[...] = _mxu(h_ref[...])

    @pl.when(compute)
    def _():
        o_ref[...] += _swiglu_tile(hb_scr[...], wg_ref[0], wu_ref[0], wd_ref[0])

    @pl.when((j == pl.num_programs(1) - 1) & (first_ref[i] != 0))
    def _():
        o_ref[...] = yf_ref[...]


def _expert_ffn(hs, y_first, tile_expert, n_active, is_first, first_block, frozen, weight_expert,
                w_gate, w_up, w_down):
    n_rows, d = hs.shape
    d_ff = w_gate.shape[2]
    n_tiles = n_rows // TM_MOE
    n_ff = d_ff // TF_FFN

    def ff_step(i, j, frozen):
        return jnp.where(frozen[i] != 0, n_ff - 1, j)

    def row_tile(i, nact):
        return jnp.minimum(i, nact[0] - 1)

    return pl.pallas_call(
        _expert_kernel,
        grid_spec=pltpu.PrefetchScalarGridSpec(
            num_scalar_prefetch=6,
            grid=(n_tiles, n_ff),
            in_specs=[
                pl.BlockSpec((TM_MOE, d), lambda i, j, te, nact, fi, fb, fr, we: (row_tile(i, nact), 0)),
                pl.BlockSpec((TM_MOE, d), lambda i, j, te, nact, fi, fb, fr, we: (fb[i], 0),
                             pipeline_mode=pl.Buffered(1)),
                pl.BlockSpec((1, d, TF_FFN), lambda i, j, te, nact, fi, fb, fr, we: (we[i], 0, ff_step(i, j, fr))),
                pl.BlockSpec((1, d, TF_FFN), lambda i, j, te, nact, fi, fb, fr, we: (we[i], 0, ff_step(i, j, fr))),
                pl.BlockSpec((1, TF_FFN, d), lambda i, j, te, nact, fi, fb, fr, we: (we[i], ff_step(i, j, fr), 0)),
            ],
            out_specs=pl.BlockSpec((TM_MOE, d), lambda i, j, te, nact, fi, fb, fr, we: (i, 0)),
            scratch_shapes=[pltpu.VMEM((TM_MOE, d), MXU_DTYPE)],
        ),
        out_shape=jax.ShapeDtypeStruct((n_rows, d), F32),
        compiler_params=_cparams("arbitrary", "arbitrary"),
        name="expert_ffn",
    )(tile_expert, n_active, is_first, first_block, frozen, weight_expert, hs, y_first, w_gate, w_up, w_down)


def _combine_kernel(pos_ref, ys_hbm, gw_ref, x_ref, g_ref, gate_ref, o_ref, buf, sem):
    i = pl.program_id(0)
    n = pl.num_programs(0)
    tm = x_ref.shape[0]
    t_total = n * tm
    cur = i % 2

    def fetch(step, b):
        def one(r):
            for s in range(TOP_K):
                _row_copy(ys_hbm, pos_ref[s * t_total + step * tm + r], buf.at[b, s], r, sem.at[b]).start()

        _for_row_groups(tm, one)

    @pl.when(i == 0)
    def _():
        fetch(0, 0)

    @pl.when(i + 1 < n)
    def _():
        fetch(i + 1, 1 - cur)

    for s in range(TOP_K):
        pltpu.make_async_copy(ys_hbm.at[pl.ds(0, tm)], buf.at[cur, s], sem.at[cur]).wait()
    gw = gw_ref[...]
    y = gw[:, 0:1] * buf[cur, 0]
    for s in range(1, TOP_K):
        y = y + gw[:, s:s + 1] * buf[cur, s]
    o_ref[...] = x_ref[...] + gate_ref[0] * _rms_norm(y, g_ref[...])


def _combine(ys, pos, gate_w, x2, g_post, gate, seq):
    t, d = x2.shape
    tm = TM_CMB
    per_b = seq // tm
    return pl.pallas_call(
        _combine_kernel,
        grid_spec=pltpu.PrefetchScalarGridSpec(
            num_scalar_prefetch=1,
            grid=(t // tm,),
            in_specs=[
                pl.BlockSpec(memory_space=pl.ANY),
                pl.BlockSpec((tm, LANES), lambda i, pos: (i, 0)),
                pl.BlockSpec((tm, d), lambda i, pos: (i, 0)),
                pl.BlockSpec((1, d), lambda i, pos: (0, 0)),
                pl.BlockSpec((1, 1, d), lambda i, pos: (i // per_b, 0, 0)),
            ],
            out_specs=pl.BlockSpec((tm, d), lambda i, pos: (i, 0)),
            scratch_shapes=[pltpu.VMEM((2, TOP_K, tm, d), F32), pltpu.SemaphoreType.DMA((2,))],
        ),
        out_shape=jax.ShapeDtypeStruct((t, d), F32),
        compiler_params=_cparams("arbitrary"),
        name="expert_combine",
    )(pos, ys, gate_w, x2, g_post, gate)


def _moe_ffn(x2, gate_w, top, g_pre, shift, scale, w_gate, w_up, w_down, g_post, gate, seq):
    t = x2.shape[0]
    n_tiles = (TOP_K * t) // TM_MOE + N_EXPERTS
    (pos, tile_expert, n_active, pad_start, pad_len, first_tile, has_rows, is_first, first_block, frozen,
     weight_expert) = _routing_tables(top[:, :TOP_K], n_tiles)
    hs = _dispatch(x2, g_pre, shift, scale, pos, pad_start, pad_len, n_active, n_tiles, seq)
    y_first, q_gate, q_up, q_down = _expert_first(hs, first_tile, has_rows, w_gate, w_up, w_down)
    ys = _expert_ffn(hs, y_first, tile_expert, n_active, is_first, first_block, frozen, weight_expert,
                     q_gate, q_up, q_down)
    return _combine(ys, pos, gate_w, x2, g_post, gate, seq)


def kernel(x, c, positions, ln_mix_pre, ln_mix_post, ln_ffn_pre, ln_ffn_post, w_mod, b_mod, w_in, b_fgate, w_out, w_ffn_gate, w_ffn_up, w_ffn_down, w_router, b_router, w_exp_gate, w_exp_up, w_exp_down):
    bsz, seq, d = x.shape
    depth = w_mod.shape[0]
    n_qkv = 3 * N_HEADS * HEAD_DIM
    x2 = x.reshape(bsz * seq, d)
    mod = _modulation(c, w_mod, b_mod).reshape(depth, bsz, N_MOD, 1, d)
    cos, sin = _rope_tables(positions)
    w_in_b = _mxu(w_in)
    for layer in range(depth):
        sh_m, sc_m, g_m, sh_f, sc_f, g_f = (mod[layer, :, k] for k in range(N_MOD))
        row = lambda a: a[layer].reshape(1, d)
        w_f = jnp.pad(w_in_b[layer, :, n_qkv:], ((0, 0), (0, LANES - N_HEADS_FOX)))
        qkv, fg = _in_proj(x2, row(ln_mix_pre), sh_m, sc_m, w_in_b, layer, n_qkv, w_f, seq)
        qkv3 = qkv.reshape(bsz, seq, n_qkv)
        cum_f = _forget_cumsum(fg.reshape(bsz, seq, LANES), b_fgate[layer])
        o_dil = _dilated_attention(qkv3, cos, sin).reshape(bsz * seq, -1)
        o_moba = _moba_attention(qkv3, cos, sin).reshape(bsz * seq, -1)
        o_fox = _fox_attention(qkv3, cum_f).reshape(bsz * seq, -1)
        j = layer // 2
        attn = (o_dil, o_moba, o_fox, _mxu(w_out[layer]), x2, row(ln_mix_post), g_m, seq)
        if layer % 2 == 0:
            x2 = _out_proj(*attn)
            x2 = _dense_ffn(x2, row(ln_ffn_pre), sh_f, sc_f, w_ffn_gate[j], w_ffn_up[j], w_ffn_down[j],
                            row(ln_ffn_post), g_f, seq)
        else:
            route = (row(ln_ffn_pre), sh_f, sc_f) + _router_params(w_router[j], b_router[j])
            x2, gate_w, top = _out_proj(*attn, route=route)
            x2 = _moe_ffn(x2, gate_w, top, row(ln_ffn_pre), sh_f, sc_f,
                          w_exp_gate[j], w_exp_up[j], w_exp_down[j], row(ln_ffn_post), g_f, seq)
    return x2.reshape(bsz, seq, d)
```
